```python
import math
import jax
import jax.numpy as jnp
from jax import lax
import numpy as np

D_MODEL = 1024
BATCH = 16
SEQ = 2048
DEPTH = 2
DEC_BATCH = 32
DEC_SEQ = 8
PAST_LEN = 16384
PAGE_SIZE = 128

HEAD_DIM = 64
A_HEADS = 4
MOBA_BLOCK = 256
MOBA_TOPK = 3
B_HEADS = 4
B_QK_DIM = 32
B_V_DIM = 2 * B_QK_DIM
C_HEADS = 8
C_GROUPS = 2
C_HPG = C_HEADS // C_GROUPS
CMP_LEN = 32
CMP_STRIDE = 16
CMP_HIDDEN = 256
SEL_BLOCK = 64
SEL_TOPN = 16
WINDOW = 512
N_BRANCH = 3
N_BUCKETS = 32
MAX_EXACT = N_BUCKETS // 2
T5_MAX_DIST = 128
N_HEADS_TOTAL = A_HEADS + B_HEADS + C_HEADS
A_W = A_HEADS * HEAD_DIM
B_QK_W = B_HEADS * 2 * B_QK_DIM
B_W = B_HEADS * B_V_DIM
C_W = C_HEADS * HEAD_DIM
C_KV_W = C_GROUPS * HEAD_DIM
MIX_W = A_W + B_W + C_W
GATE_W = C_HEADS * N_BRANCH
Q_W = A_W + B_QK_W + C_W + GATE_W
KV_ROW = 2 * A_W + B_QK_W + B_W + 4 * C_KV_W
WIN_ROW = 2 * C_KV_W
IN_W = Q_W + KV_ROW + WIN_ROW
N_GROUPS = 4
EXPERTS_PER_GROUP = 8
N_EXPERTS = N_GROUPS * EXPERTS_PER_GROUP
TOP_IN_GROUP = 2
EXPERT_FF = 256
ALPHA = (2.0 * DEPTH) ** 0.25
BETA = (8.0 * DEPTH) ** -0.25
QBLK = 128
LN_EPS = 1e-5
NEG = -1e30
F32 = jnp.float32

Q_SHAPES = ((A_HEADS, HEAD_DIM), (B_HEADS, 2, B_QK_DIM), (C_GROUPS, C_HPG, HEAD_DIM), (C_GROUPS, C_HPG, N_BRANCH))
ROW_SHAPES = ((A_HEADS, HEAD_DIM), (A_HEADS, HEAD_DIM), (B_HEADS, 2, B_QK_DIM), (B_HEADS, B_V_DIM),
              (C_GROUPS, HEAD_DIM), (C_GROUPS, HEAD_DIM), (C_GROUPS, HEAD_DIM), (C_GROUPS, HEAD_DIM))
WIN_SHAPES = ((C_GROUPS, HEAD_DIM), (C_GROUPS, HEAD_DIM))

kernel_name = 'hymba_moba_diff_nsa_hmoe_step'


def layer_norm(x, g, b):
    xf = x.astype(F32)
    mu = jnp.mean(xf, -1, keepdims=True)
    var = jnp.mean(jnp.square(xf - mu), -1, keepdims=True)
    return ((xf - mu) * lax.rsqrt(var + LN_EPS) * g.astype(F32) + b.astype(F32)).astype(x.dtype)


def t5_bucket(rel):
    n = jnp.maximum(rel, 0)
    nf = jnp.maximum(n, 1).astype(F32)
    large = MAX_EXACT + (jnp.log(nf / MAX_EXACT) / math.log(T5_MAX_DIST / MAX_EXACT)
                         * (N_BUCKETS - MAX_EXACT)).astype(jnp.int32)
    return jnp.where(n < MAX_EXACT, n, jnp.minimum(large, N_BUCKETS - 1))


def masked_softmax(logits, mask):
    p = jax.nn.softmax(jnp.where(mask, logits, NEG), axis=-1)
    return jnp.where(mask, p, 0.0)


def pad_blocks(t, block, min_blocks):
    L = t.shape[1]
    nb = max(-(-L // block), min_blocks)
    return jnp.pad(t, [(0, 0), (0, nb * block - L)] + [(0, 0)] * (t.ndim - 2))


def split_cols(t, shapes):
    out, o = [], 0
    for shp in shapes:
        w = math.prod(shp)
        out.append(t[..., o:o + w].reshape(t.shape[:-1] + shp))
        o += w
    return out


def compress(t, pos_emb, w1, w2):
    B, L, G, d = t.shape
    ncmp = (L - CMP_LEN) // CMP_STRIDE + 1
    idx = np.arange(ncmp)[:, None] * CMP_STRIDE + np.arange(CMP_LEN)[None, :]
    blk = t[:, idx] + pos_emb[:, None, :]
    blk = jnp.swapaxes(blk, 2, 3).reshape(B, ncmp, G, CMP_LEN * d)
    return jax.nn.gelu(blk @ w1) @ w2


def cmp_to_sel_weights(ncmp, nsel):
    s = np.arange(ncmp)[:, None] * CMP_STRIDE
    b0 = np.arange(nsel)[None, :] * SEL_BLOCK
    ov = np.clip(np.minimum(s + CMP_LEN, b0 + SEL_BLOCK) - np.maximum(s, b0), 0, None)
    return jnp.asarray(ov.astype(np.float32) / np.float32(CMP_STRIDE))


def moba_attend(q, q_pos, k, v, tab):
    B, Q, H, d = q.shape
    nb = k.shape[1] // MOBA_BLOCK
    kb = k.reshape(B, nb, MOBA_BLOCK, H, d).transpose(0, 3, 1, 2, 4)
    vb = v.reshape(B, nb, MOBA_BLOCK, H, d).transpose(0, 3, 1, 2, 4)
    means = jnp.mean(kb.astype(F32), axis=3)
    own = q_pos // MOBA_BLOCK
    gate = jnp.einsum('bqhd,bhnd->bhqn', q.astype(F32), means)
    gate = jnp.where(jnp.arange(nb)[None, :] < own[:, None], gate, -jnp.inf)
    _, top = lax.top_k(gate, MOBA_TOPK)
    sel = jnp.concatenate([top, jnp.broadcast_to(own[None, None, :, None], (B, H, Q, 1)).astype(top.dtype)], -1)
    slot_ok = jnp.arange(MOBA_TOPK)[None, :] < own[:, None]
    bi = jnp.arange(B)[:, None, None]
    hi = jnp.arange(H)[None, :, None]
    h4 = jnp.arange(H)[None, :, None, None]
    tabf = tab.astype(F32)
    offs = jnp.arange(MOBA_BLOCK, dtype=jnp.int32)
    logits = []
    for r in range(MOBA_TOPK + 1):
        idx = sel[..., r]
        kg = kb[bi, hi, idx]
        rel = q_pos[None, None, :, None] - (idx[..., None] * MOBA_BLOCK + offs)
        lg = jnp.einsum('bqhd,bhqkd->bhqk', q, kg).astype(F32) * d ** -0.5 + tabf[h4, t5_bucket(rel)]
        ok = slot_ok[None, None, :, r, None] if r < MOBA_TOPK else rel >= 0
        logits.append(jnp.where(ok, lg, NEG))
    p = jax.nn.softmax(jnp.stack(logits, 3).reshape(B, H, Q, -1), axis=-1)
    p = p.reshape(B, H, Q, MOBA_TOPK + 1, MOBA_BLOCK).astype(v.dtype)
    return sum(jnp.einsum('bhqk,bhqkd->bqhd', p[:, :, :, r], vb[bi, hi, sel[..., r]])
               for r in range(MOBA_TOPK + 1))


def diff_attend(q, q_pos, k, v, lam, lam_init, sub_w, tab):
    L = k.shape[1]
    rel = q_pos[:, None] - jnp.arange(L, dtype=jnp.int32)[None, :]
    bias = tab.astype(F32)[:, t5_bucket(rel)]
    lg = jnp.einsum('bqhmd,bkhmd->bmhqk', q, k).astype(F32) * B_QK_DIM ** -0.5 + bias
    p = masked_softmax(lg, rel >= 0)
    attn = p[:, 0] - lam * p[:, 1]
    o = jnp.einsum('bhqk,bkhd->bqhd', attn.astype(v.dtype), v).astype(F32)
    o = o * lax.rsqrt(jnp.mean(jnp.square(o), -1, keepdims=True) + LN_EPS)
    return o * sub_w.astype(F32) * (1.0 - lam_init)


def nsa_attend(q, gates, q_pos, kc, vc, ks, vs, kw, vw, kw_pos, tab):
    B, Q, G, Hg, d = q.shape
    scale = d ** -0.5
    tabf = tab.astype(F32)
    ncmp = kc.shape[1]
    cmask = (jnp.arange(ncmp) * CMP_STRIDE + CMP_LEN - 1)[None, :] <= q_pos[:, None]
    p_cmp = masked_softmax(jnp.einsum('bqghd,bcgd->bghqc', q, kc).astype(F32) * scale, cmask)
    o_cmp = jnp.einsum('bghqc,bcgd->bqghd', p_cmp.astype(vc.dtype), vc)
    nsel = ks.shape[1] // SEL_BLOCK
    p_slc = jnp.einsum('bghqc,cn->bgqn', p_cmp, cmp_to_sel_weights(ncmp, nsel))
    own = q_pos // SEL_BLOCK
    j = jnp.arange(nsel)[None, :]
    forced = (j == 0) | (j == own[:, None]) | (j == own[:, None] - 1)
    score = jnp.where(j <= own[:, None], jnp.where(forced, jnp.inf, p_slc), -jnp.inf)
    val, top = lax.top_k(score, SEL_TOPN)
    ksb = ks.reshape(B, nsel, SEL_BLOCK, G, d).transpose(0, 3, 1, 2, 4)
    vsb = vs.reshape(B, nsel, SEL_BLOCK, G, d).transpose(0, 3, 1, 2, 4)
    bi = jnp.arange(B)[:, None, None, None]
    gi = jnp.arange(G)[None, :, None, None]
    kg = ksb[bi, gi, top]
    rel = q_pos[None, None, :, None, None] - (top[..., None] * SEL_BLOCK + jnp.arange(SEL_BLOCK, dtype=jnp.int32))
    smask = ((val > -jnp.inf)[..., None] & (rel >= 0)).reshape(B, G, 1, Q, SEL_TOPN * SEL_BLOCK)
    tab_g = tabf.reshape(G, Hg, N_BUCKETS)
    bias = tab_g[jnp.arange(G)[None, :, None, None, None, None],
                 jnp.arange(Hg)[None, None, :, None, None, None],
                 t5_bucket(rel)[:, :, None]]
    lg = jnp.einsum('bqghd,bgqnkd->bghqnk', q, kg).astype(F32) * scale + bias
    p_s = masked_softmax(lg.reshape(B, G, Hg, Q, SEL_TOPN * SEL_BLOCK), smask)
    vg = vsb[bi, gi, top].reshape(B, G, Q, SEL_TOPN * SEL_BLOCK, d)
    o_slc = jnp.einsum('bghqk,bgqkd->bqghd', p_s.astype(vs.dtype), vg)
    relw = q_pos[:, None] - kw_pos[None, :]
    wmask = (relw >= 0) & (relw < WINDOW) & (kw_pos[None, :] >= 0)
    bias_w = tabf[:, t5_bucket(relw)].reshape(G, Hg, Q, -1)
    p_w = masked_softmax(jnp.einsum('bqghd,bkgd->bghqk', q, kw).astype(F32) * scale + bias_w, wmask)
    o_win = jnp.einsum('bghqk,bkgd->bqghd', p_w.astype(vw.dtype), vw)
    g = jax.nn.sigmoid(gates.astype(F32))
    o = g[..., 0:1] * o_cmp + g[..., 1:2] * o_slc + g[..., 2:3] * o_win
    return o.reshape(B, Q, G * Hg * d)


def token_mix(qa, qb, qc, gt, q_pos, keys, win_kv, dp, tab):
    ka, va, kb, vb, kc_tok, vc_tok, ks, vs = keys
    kw, vw, kw_pos = win_kv
    lam, lam_init, sub_w = dp
    B, Q = qa.shape[:2]
    oa = moba_attend(qa, q_pos, ka, va, tab[:A_HEADS])
    ob = diff_attend(qb, q_pos, kb, vb, lam, lam_init, sub_w, tab[A_HEADS:A_HEADS + B_HEADS])
    oc = nsa_attend(qc, gt, q_pos, kc_tok, vc_tok, ks, vs, kw, vw, kw_pos, tab[A_HEADS + B_HEADS:])
    return jnp.concatenate([oa.reshape(B, Q, A_W).astype(F32), ob.reshape(B, Q, B_W), oc.astype(F32)], -1).astype(qa.dtype)


def prepare_keys(row, cmp_params):
    ka, va, kb, vb, kc, vc, ks, vs = split_cols(row, ROW_SHAPES)
    pk, pv, kw1, kw2, vw1, vw2 = cmp_params
    return (pad_blocks(ka, MOBA_BLOCK, MOBA_TOPK), pad_blocks(va, MOBA_BLOCK, MOBA_TOPK), kb, vb,
            compress(kc, pk, kw1, kw2), compress(vc, pv, vw1, vw2),
            pad_blocks(ks, SEL_BLOCK, SEL_TOPN), pad_blocks(vs, SEL_BLOCK, SEL_TOPN))


def mix_prompt(x, w_in_l, w_out_l, cmp_params, dp, tab):
    B, S, _ = x.shape
    h = x @ w_in_l
    qs, row, win = h[..., :Q_W], h[..., Q_W:Q_W + KV_ROW], h[..., Q_W + KV_ROW:]
    qa, qb, qc, gt = split_cols(qs, Q_SHAPES)
    keys = prepare_keys(row, cmp_params)
    kw, vw = split_cols(win, WIN_SHAPES)
    kw_pad = jnp.pad(kw, ((0, 0), (WINDOW, 0), (0, 0), (0, 0)))
    vw_pad = jnp.pad(vw, ((0, 0), (WINDOW, 0), (0, 0), (0, 0)))
    nqb = S // QBLK

    def blocks(t):
        return jnp.swapaxes(t.reshape((B, nqb, QBLK) + t.shape[2:]), 0, 1)

    def body(xs):
        qa_i, qb_i, qc_i, gt_i, i = xs
        start = i * QBLK
        q_pos = start + jnp.arange(QBLK, dtype=jnp.int32)
        kw_i = lax.dynamic_slice_in_dim(kw_pad, start, WINDOW + QBLK, axis=1)
        vw_i = lax.dynamic_slice_in_dim(vw_pad, start, WINDOW + QBLK, axis=1)
        kw_pos = start - WINDOW + jnp.arange(WINDOW + QBLK, dtype=jnp.int32)
        return token_mix(qa_i, qb_i, qc_i, gt_i, q_pos, keys, (kw_i, vw_i, kw_pos), dp, tab)

    o = lax.map(body, (blocks(qa), blocks(qb), blocks(qc), blocks(gt), jnp.arange(nqb, dtype=jnp.int32)))
    o = jnp.swapaxes(o, 0, 1).reshape(B, S, MIX_W)
    return o @ w_out_l, row, win[:, S - min(WINDOW, S):]


def mix_sample(x, cache_l, state_l, page_table, w_in_l, w_out_l, cmp_params, dp, tab):
    B, S, _ = x.shape
    past = page_table.shape[1] * cache_l.shape[1]
    wb = state_l.shape[1]
    h = x @ w_in_l
    qs, row, win = h[..., :Q_W], h[..., Q_W:Q_W + KV_ROW], h[..., Q_W + KV_ROW:]
    qa, qb, qc, gt = split_cols(qs, Q_SHAPES)
    past_rows = cache_l[page_table].reshape(B, past, KV_ROW)
    keys = prepare_keys(jnp.concatenate([past_rows, row.astype(past_rows.dtype)], 1), cmp_params)
    win_all = jnp.concatenate([state_l, win.astype(state_l.dtype)], 1)
    kw, vw = split_cols(win_all, WIN_SHAPES)
    kw_pos = past - wb + jnp.arange(wb + S, dtype=jnp.int32)
    q_pos = past + jnp.arange(S, dtype=jnp.int32)
    o = token_mix(qa, qb, qc, gt, q_pos, keys, (kw, vw, kw_pos), dp, tab)
    return o @ w_out_l, row, win_all[:, S:]


def hier_moe(x, w_r1, b_r1, w_r2, b_r2, w1, w3, w2):
    def per_seq(xr):
        S = xr.shape[0]
        ar = jnp.arange(S)
        lg1 = (xr @ w_r1).astype(F32) + b_r1.astype(F32)
        p1 = jax.nn.softmax(lg1, axis=-1)
        grp = jnp.argmax(lg1, axis=-1)
        pg = p1[ar, grp]
        lg2 = jnp.einsum('sd,gde->sge', xr, w_r2).astype(F32) + b_r2.astype(F32)
        lg2 = lg2[ar, grp]
        v2, i2 = lax.top_k(lg2, TOP_IN_GROUP)
        wt = jax.nn.softmax(v2, axis=-1) * pg[:, None]
        eid = grp[:, None] * EXPERTS_PER_GROUP + i2
        comb = jnp.einsum('ske,sk->se', jax.nn.one_hot(eid, N_EXPERTS, dtype=F32), wt)
        hdn = jax.nn.silu(jnp.einsum('sd,edf->sef', xr, w1)) * jnp.einsum('sd,edf->sef', xr, w3)
        return jnp.einsum('sef,efd->sd', hdn * comb[..., None].astype(hdn.dtype), w2).astype(xr.dtype)
    return lax.map(per_seq, x)


def setup_inputs(seed: int = 0) -> dict:
    key = jax.random.key(seed)
    ks = jax.random.split(key, 32)

    def nrm(k, shape, scale):
        return jax.random.normal(k, shape, F32) * scale

    n_pages = PAST_LEN // PAGE_SIZE
    n_used = DEC_BATCH * n_pages
    n_pool = n_used + n_used // 4
    wb = min(WINDOW, PAST_LEN)
    page_table = jax.random.permutation(ks[4], n_pool)[:n_used].reshape(DEC_BATCH, n_pages).astype(jnp.int32)
    return {
        'x_prompt': nrm(ks[0], (BATCH, SEQ, D_MODEL), 1.0),
        'x_sample': nrm(ks[1], (DEC_BATCH, DEC_SEQ, D_MODEL), 1.0),
        'cache_kv': nrm(ks[2], (DEPTH, n_pool, PAGE_SIZE, KV_ROW), 1.0),
        'state_win': nrm(ks[3], (DEPTH, DEC_BATCH, wb, WIN_ROW), 1.0),
        'page_table': page_table,
        'rel_bias_table': nrm(ks[5], (N_HEADS_TOTAL, N_BUCKETS), 0.5),
        'w_in': nrm(ks[6], (DEPTH, D_MODEL, IN_W), D_MODEL ** -0.5),
        'w_out': nrm(ks[7], (DEPTH, MIX_W, D_MODEL), MIX_W ** -0.5 * BETA),
        'lam_q1': nrm(ks[8], (DEPTH, B_QK_DIM), 0.1),
        'lam_k1': nrm(ks[9], (DEPTH, B_QK_DIM), 0.1),
        'lam_q2': nrm(ks[10], (DEPTH, B_QK_DIM), 0.1),
        'lam_k2': nrm(ks[11], (DEPTH, B_QK_DIM), 0.1),
        'diff_subln_w': 1.0 + nrm(ks[12], (DEPTH, B_V_DIM), 0.02),
        'cmp_pos_k': nrm(ks[13], (DEPTH, CMP_LEN, HEAD_DIM), 0.02),
        'cmp_pos_v': nrm(ks[14], (DEPTH, CMP_LEN, HEAD_DIM), 0.02),
        'cmp_k_w1': nrm(ks[15], (DEPTH, CMP_LEN * HEAD_DIM, CMP_HIDDEN), (CMP_LEN * HEAD_DIM) ** -0.5),
        'cmp_k_w2': nrm(ks[16], (DEPTH, CMP_HIDDEN, HEAD_DIM), CMP_HIDDEN ** -0.5),
        'cmp_v_w1': nrm(ks[17], (DEPTH, CMP_LEN * HEAD_DIM, CMP_HIDDEN), (CMP_LEN * HEAD_DIM) ** -0.5),
        'cmp_v_w2': nrm(ks[18], (DEPTH, CMP_HIDDEN, HEAD_DIM), CMP_HIDDEN ** -0.5),
        'ln_mix_g': 1.0 + nrm(ks[19], (DEPTH, D_MODEL), 0.02),
        'ln_mix_b': nrm(ks[20], (DEPTH, D_MODEL), 0.02),
        'ln_ffn_g': 1.0 + nrm(ks[21], (DEPTH, D_MODEL), 0.02),
        'ln_ffn_b': nrm(ks[22], (DEPTH, D_MODEL), 0.02),
        'router_w1': nrm(ks[23], (DEPTH, D_MODEL, N_GROUPS), D_MODEL ** -0.5),
        'router_b1': nrm(ks[24], (DEPTH, N_GROUPS), 0.01),
        'router_w2': nrm(ks[25], (DEPTH, N_GROUPS, D_MODEL, EXPERTS_PER_GROUP), D_MODEL ** -0.5),
        'router_b2': nrm(ks[26], (DEPTH, N_GROUPS, EXPERTS_PER_GROUP), 0.01),
        'expert_w1': nrm(ks[27], (DEPTH, N_EXPERTS, D_MODEL, EXPERT_FF), D_MODEL ** -0.5),
        'expert_w3': nrm(ks[28], (DEPTH, N_EXPERTS, D_MODEL, EXPERT_FF), D_MODEL ** -0.5),
        'expert_w2': nrm(ks[29], (DEPTH, N_EXPERTS, EXPERT_FF, D_MODEL), EXPERT_FF ** -0.5 * BETA),
    }


def reference(x_prompt, x_sample, cache_kv, state_win, page_table, rel_bias_table, w_in, w_out,
              lam_q1, lam_k1, lam_q2, lam_k2, diff_subln_w, cmp_pos_k, cmp_pos_v, cmp_k_w1, cmp_k_w2,
              cmp_v_w1, cmp_v_w2, ln_mix_g, ln_mix_b, ln_ffn_g, ln_ffn_b, router_w1, router_b1,
              router_w2, router_b2, expert_w1, expert_w3, expert_w2):
    xp, xs = x_prompt, x_sample
    kv_p, win_p, kv_s, win_s = [], [], [], []
    for l in range(DEPTH):
        lam_init = 0.8 - 0.6 * math.exp(-0.3 * l)
        lam = (jnp.exp(jnp.sum(lam_q1[l].astype(F32) * lam_k1[l].astype(F32)))
               - jnp.exp(jnp.sum(lam_q2[l].astype(F32) * lam_k2[l].astype(F32))) + lam_init)
        dp = (lam, lam_init, diff_subln_w[l])
        cmp_params = (cmp_pos_k[l], cmp_pos_v[l], cmp_k_w1[l], cmp_k_w2[l], cmp_v_w1[l], cmp_v_w2[l])
        mp, row_p, wr_p = mix_prompt(xp, w_in[l], w_out[l], cmp_params, dp, rel_bias_table)
        ms, row_s, wr_s = mix_sample(xs, cache_kv[l], state_win[l], page_table, w_in[l], w_out[l],
                                     cmp_params, dp, rel_bias_table)
        xp = layer_norm(ALPHA * xp + mp, ln_mix_g[l], ln_mix_b[l])
        xs = layer_norm(ALPHA * xs + ms, ln_mix_g[l], ln_mix_b[l])
        fp = hier_moe(xp, router_w1[l], router_b1[l], router_w2[l], router_b2[l], expert_w1[l], expert_w3[l], expert_w2[l])
        fs = hier_moe(xs, router_w1[l], router_b1[l], router_w2[l], router_b2[l], expert_w1[l], expert_w3[l], expert_w2[l])
        xp = layer_norm(ALPHA * xp + fp, ln_ffn_g[l], ln_ffn_b[l])
        xs = layer_norm(ALPHA * xs + fs, ln_ffn_g[l], ln_ffn_b[l])
        kv_p.append(row_p)
        win_p.append(wr_p)
        kv_s.append(row_s)
        win_s.append(wr_s)
    kv_prompt_new = jnp.stack(kv_p)
    win_prompt_new = jnp.stack(win_p)
    kv_sample_new = jnp.stack(kv_s)
    win_sample_new = jnp.stack(win_s)
    return (xp, xs, kv_prompt_new, win_prompt_new, kv_sample_new, win_sample_new)
```

```python
import functools
import math

import numpy as np
import jax
import jax.numpy as jnp
from jax import lax
from jax.experimental import pallas as pl
from jax.experimental.pallas import tpu as pltpu

D_MODEL = 1024
PAGE_SIZE = 128
HEAD_DIM = 64
A_HEADS = 4
MOBA_BLOCK = 256
MOBA_TOPK = 3
B_HEADS = 4
B_QK_DIM = 32
B_V_DIM = 2 * B_QK_DIM
C_HEADS = 8
C_GROUPS = 2
C_HPG = C_HEADS // C_GROUPS
CMP_LEN = 32
CMP_STRIDE = 16
CMP_HIDDEN = 256
SEL_BLOCK = 64
SEL_TOPN = 16
WINDOW = 512
N_BRANCH = 3
N_BUCKETS = 32
MAX_EXACT = N_BUCKETS // 2
T5_MAX_DIST = 128
A_W = A_HEADS * HEAD_DIM
B_QK_W = B_HEADS * 2 * B_QK_DIM
B_W = B_HEADS * B_V_DIM
C_W = C_HEADS * HEAD_DIM
C_KV_W = C_GROUPS * HEAD_DIM
MIX_W = A_W + B_W + C_W
GATE_W = C_HEADS * N_BRANCH
Q_W = A_W + B_QK_W + C_W + GATE_W
KV_ROW = 2 * A_W + B_QK_W + B_W + 4 * C_KV_W
WIN_ROW = 2 * C_KV_W
N_GROUPS = 4
EXPERTS_PER_GROUP = 8
N_EXPERTS = N_GROUPS * EXPERTS_PER_GROUP
EXPERT_FF = 256
LN_EPS = 1e-5
NEG = -1e30
F32 = jnp.float32
MXU_DTYPE = jnp.bfloat16

LANES = 128
SUBLANES = 8
VMEM_LIMIT_BYTES = 56 * 1024 * 1024

T = 256
GT_PAD = LANES
QA_OFF, QB_OFF, QC_OFF, GT_OFF = 0, A_W, A_W + B_QK_W, A_W + B_QK_W + C_W
ROW_OFF = GT_OFF + GT_PAD
WIN_OFF = ROW_OFF + KV_ROW
IN_W_PACKED = WIN_OFF + WIN_ROW
KA_OFF, VA_OFF, KB_OFF, VB_OFF = 0, A_W, 2 * A_W, 2 * A_W + B_QK_W
KC_OFF = 2 * A_W + B_QK_W + B_W
KS_OFF = KC_OFF + 2 * C_KV_W
BIAS_RANGE = 1024


def _cparams(*sem):
    return pltpu.CompilerParams(dimension_semantics=sem, vmem_limit_bytes=VMEM_LIMIT_BYTES)


def _dot(a, b):
    return jnp.dot(a, b, preferred_element_type=F32)


def _dot_nt(a, b):
    return lax.dot_general(a, b, (((1,), (1,)), ((), ())), preferred_element_type=F32)


def _lane_iota(n):
    return lax.broadcasted_iota(jnp.int32, (1, n), 1)


def _in_range(x, lo, hi):
    return (x >= lo) & (x < hi)


def _online_update(s, m, l):
    m_new = jnp.maximum(m, jnp.max(s, axis=-1, keepdims=True))
    alpha = jnp.exp(m - m_new)
    p = jnp.exp(s - m_new)
    return p, alpha, m_new, alpha * l + jnp.sum(p, axis=-1, keepdims=True)


def _layer_norm(z, g, b):
    mu = jnp.mean(z, axis=-1, keepdims=True)
    zc = z - mu
    var = jnp.mean(zc * zc, axis=-1, keepdims=True)
    return zc * lax.rsqrt(var + LN_EPS) * g + b


def _split_hi_lo(x):
    hi = x.astype(MXU_DTYPE)
    lo = (x - hi.astype(F32)).astype(MXU_DTYPE)
    return hi, lo


def _inproj_kernel(x_ref, w_ref, qa_ref, qb_ref, qc_ref, gt_ref, row_ref, win_ref):
    x = x_ref[...].astype(MXU_DTYPE)
    qa_ref[...] = _dot(x, w_ref[:, QA_OFF:QB_OFF]).astype(qa_ref.dtype)
    qb_ref[...] = _dot(x, w_ref[:, QB_OFF:QC_OFF]).astype(qb_ref.dtype)
    qc_ref[...] = _dot(x, w_ref[:, QC_OFF:GT_OFF]).astype(qc_ref.dtype)
    gt_ref[...] = _dot(x, w_ref[:, GT_OFF:ROW_OFF])
    row_ref[...] = _dot(x, w_ref[:, ROW_OFF:WIN_OFF])
    win_ref[...] = _dot(x, w_ref[:, WIN_OFF:IN_W_PACKED])


def _inproj(x2, w_packed):
    n = x2.shape[0]
    tm = min(512, n)
    assert n % tm == 0
    widths = (A_W, B_QK_W, C_W, GT_PAD, KV_ROW, WIN_ROW)
    dtypes = (MXU_DTYPE, MXU_DTYPE, MXU_DTYPE, F32, F32, F32)
    return pl.pallas_call(
        _inproj_kernel,
        grid=(n // tm,),
        in_specs=[pl.BlockSpec((tm, D_MODEL), lambda i: (i, 0)),
                  pl.BlockSpec((D_MODEL, IN_W_PACKED), lambda i: (0, 0))],
        out_specs=[pl.BlockSpec((tm, w), lambda i: (i, 0)) for w in widths],
        out_shape=[jax.ShapeDtypeStruct((n, w), d) for w, d in zip(widths, dtypes)],
        compiler_params=_cparams("arbitrary"),
        name="inproj",
    )(x2, w_packed)


def _outproj_ln_kernel(oa_ref, ob_ref, oc_ref, x_ref, w_ref, g_ref, b_ref, y_ref, *, alpha):
    acc = _dot(oa_ref[...], w_ref[0:A_W, :])
    acc = acc + _dot(ob_ref[...], w_ref[A_W:A_W + B_W, :])
    acc = acc + _dot(oc_ref[...], w_ref[A_W + B_W:MIX_W, :])
    y_ref[...] = _layer_norm(alpha * x_ref[...] + acc, g_ref[...], b_ref[...])


def _outproj_ln(oa, ob, oc, x2, w_out_p, g, b, alpha):
    n = x2.shape[0]
    tm = min(512, n)
    assert n % tm == 0
    row = lambda w: pl.BlockSpec((tm, w), lambda i: (i, 0))
    const = lambda shape: pl.BlockSpec(shape, lambda i: (0, 0))
    return pl.pallas_call(
        functools.partial(_outproj_ln_kernel, alpha=alpha),
        grid=(n // tm,),
        in_specs=[row(A_W), row(B_W), row(C_W), row(D_MODEL), const((MIX_W, D_MODEL)),
                  const((1, D_MODEL)), const((1, D_MODEL))],
        out_specs=row(D_MODEL),
        out_shape=jax.ShapeDtypeStruct((n, D_MODEL), F32),
        compiler_params=_cparams("arbitrary"),
        name="outproj_ln",
    )(oa, ob, oc, x2, w_out_p, g, b)


ROUTER_LANES = LANES


def _route(lg):
    lane = _lane_iota(ROUTER_LANES)
    big = jnp.int32(ROUTER_LANES)
    is_g = lane < N_GROUPS
    lg1 = jnp.where(is_g, lg, -jnp.inf)
    m1 = jnp.max(lg1, axis=-1, keepdims=True)
    grp = jnp.min(jnp.where(lg1 == m1, lane, big), axis=-1, keepdims=True)
    pg = 1.0 / jnp.sum(jnp.where(is_g, jnp.exp(lg1 - m1), 0.0), axis=-1, keepdims=True)
    lo = N_GROUPS + grp * EXPERTS_PER_GROUP
    lg2 = jnp.where((lane >= lo) & (lane < lo + EXPERTS_PER_GROUP), lg, -jnp.inf)
    v1 = jnp.max(lg2, axis=-1, keepdims=True)
    i1 = jnp.min(jnp.where(lg2 == v1, lane, big), axis=-1, keepdims=True)
    lg2b = jnp.where(lane == i1, -jnp.inf, lg2)
    v2 = jnp.max(lg2b, axis=-1, keepdims=True)
    i2 = jnp.min(jnp.where(lg2b == v2, lane, big), axis=-1, keepdims=True)
    e2 = jnp.exp(v2 - v1)
    w1 = pg / (1.0 + e2)
    w2 = pg * e2 / (1.0 + e2)
    return jnp.where(lane == i1, w1, jnp.where(lane == i2, w2, 0.0))


def _moe_kernel(x_ref, wrh_ref, wrl_ref, br_ref, w1_ref, w3_ref, w2_ref, g_ref, b_ref, y_ref,
                xb_ref, comb_ref, acc_ref, *, alpha):
    e = pl.program_id(1)

    @pl.when(e == 0)
    def _():
        xh, xl = _split_hi_lo(x_ref[...])
        xb_ref[...] = xh
        lg = _dot(xh, wrh_ref[...]) + _dot(xl, wrh_ref[...]) + _dot(xh, wrl_ref[...]) + br_ref[...]
        comb_ref[...] = _route(lg)
        acc_ref[...] = jnp.zeros_like(acc_ref)

    xb = xb_ref[...]
    h1 = _dot(xb, w1_ref[0])
    h3 = _dot(xb, w3_ref[0])
    lane = _lane_iota(ROUTER_LANES)
    c = jnp.sum(jnp.where(lane == e + N_GROUPS, comb_ref[...], 0.0), axis=-1, keepdims=True)
    hd = h1 * (1.0 / (1.0 + jnp.exp(-h1))) * h3 * c
    acc_ref[...] += _dot(hd.astype(MXU_DTYPE), w2_ref[0])

    @pl.when(e == pl.num_programs(1) - 1)
    def _():
        y_ref[...] = _layer_norm(alpha * x_ref[...] + acc_ref[...], g_ref[...], b_ref[...])


def _moe_ln(x2, wr_hi, wr_lo, br, w1, w3, w2, g, b, alpha):
    n = x2.shape[0]
    tm = min(1024, n)
    assert n % tm == 0
    const = lambda shape: pl.BlockSpec(shape, lambda i, e: (0,) * len(shape))
    return pl.pallas_call(
        functools.partial(_moe_kernel, alpha=alpha),
        grid=(n // tm, N_EXPERTS),
        in_specs=[pl.BlockSpec((tm, D_MODEL), lambda i, e: (i, 0)),
                  const((D_MODEL, ROUTER_LANES)), const((D_MODEL, ROUTER_LANES)), const((1, ROUTER_LANES)),
                  pl.BlockSpec((1, D_MODEL, EXPERT_FF), lambda i, e: (e, 0, 0)),
                  pl.BlockSpec((1, D_MODEL, EXPERT_FF), lambda i, e: (e, 0, 0)),
                  pl.BlockSpec((1, EXPERT_FF, D_MODEL), lambda i, e: (e, 0, 0)),
                  const((1, D_MODEL)), const((1, D_MODEL))],
        out_specs=pl.BlockSpec((tm, D_MODEL), lambda i, e: (i, 0)),
        out_shape=jax.ShapeDtypeStruct((n, D_MODEL), F32),
        scratch_shapes=[pltpu.VMEM((tm, D_MODEL), MXU_DTYPE), pltpu.VMEM((tm, ROUTER_LANES), F32),
                        pltpu.VMEM((tm, D_MODEL), F32)],
        compiler_params=_cparams("arbitrary", "arbitrary"),
        name="moe_ln",
    )(x2, wr_hi, wr_lo, br, w1, w3, w2, g, b)


def _gelu_tanh(x):
    return 0.5 * x * (1.0 + jnp.tanh(math.sqrt(2.0 / math.pi) * (x + 0.044715 * (x * x * x))))


def _compress_kernel(pt_ref, *refs, n_ops, op_rows):
    del pt_ref
    pages = (refs[:n_ops], refs[n_ops:2 * n_ops])
    pos_ref, wlo_ref, whi_ref, w2_ref, out_ref, carry_ref = refs[2 * n_ops:]
    cpo = op_rows // CMP_STRIDE
    m = n_ops * cpo
    hw = C_GROUPS * CMP_HIDDEN

    @pl.when(pl.program_id(1) == 0)
    def _():
        carry_ref[...] = jnp.zeros_like(carry_ref)

    u = [jnp.zeros((m, hw), F32), jnp.zeros((m, hw), F32)]
    v = [jnp.zeros((m, hw), F32), jnp.zeros((m, hw), F32)]
    for pair in range(CMP_STRIDE // 2):
        for kv in range(2):
            sl = slice(kv * C_KV_W, (kv + 1) * C_KV_W)
            xlo, xhi = [], []
            for t in (2 * pair, 2 * pair + 1):
                xt = jnp.concatenate([p[0, 0, pl.ds(t, cpo, stride=CMP_STRIDE), :] for p in pages[kv]], axis=0)
                xlo.append((xt + pos_ref[t:t + 1, sl]).astype(MXU_DTYPE))
                xhi.append((xt + pos_ref[CMP_STRIDE + t:CMP_STRIDE + t + 1, sl]).astype(MXU_DTYPE))
            u[kv] = u[kv] + _dot(jnp.concatenate(xlo, axis=1), wlo_ref[kv, pair])
            v[kv] = v[kv] + _dot(jnp.concatenate(xhi, axis=1), whi_ref[kv, pair])
    row0 = lax.broadcasted_iota(jnp.int32, (m, 1), 0) == 0
    for kv in range(2):
        prev = jnp.where(row0, carry_ref[0:1, kv * hw:(kv + 1) * hw], pltpu.roll(u[kv], 1, axis=0))
        carry_ref[0:1, kv * hw:(kv + 1) * hw] = u[kv][m - 1:m, :]
        hid = _gelu_tanh(prev + v[kv])
        out_ref[0, :, kv * C_KV_W:(kv + 1) * C_KV_W] = _dot(hid.astype(MXU_DTYPE), w2_ref[kv])


def _compress(cache, layer, page_table, col_block, cw, n_ops):
    pos2, wlo, whi, w2b = cw
    nb, n_pages = page_table.shape
    op_rows = cache.shape[2]
    assert n_pages % n_ops == 0
    steps = n_pages // n_ops
    cpo = op_rows // CMP_STRIDE
    kv_w = 2 * C_KV_W

    def page_map(b, s, pt, *, k, kv):
        return (layer, pt[b, s * n_ops + k], 0, col_block + kv)

    const = lambda shape: pl.BlockSpec(shape, lambda b, s, pt: (0,) * len(shape))
    grid_spec = pltpu.PrefetchScalarGridSpec(
        num_scalar_prefetch=1,
        grid=(nb, steps),
        in_specs=[pl.BlockSpec((1, 1, op_rows, C_KV_W), functools.partial(page_map, k=k, kv=kv))
                  for kv in range(2) for k in range(n_ops)]
        + [const(pos2.shape), const(wlo.shape), const(whi.shape), const(w2b.shape)],
        out_specs=pl.BlockSpec((1, n_ops * cpo, kv_w), lambda b, s, pt: (b, s, 0)),
        scratch_shapes=[pltpu.VMEM((SUBLANES, 2 * C_GROUPS * CMP_HIDDEN), F32)],
    )
    return pl.pallas_call(
        functools.partial(_compress_kernel, n_ops=n_ops, op_rows=op_rows),
        grid_spec=grid_spec,
        out_shape=jax.ShapeDtypeStruct((nb, n_pages * cpo, kv_w), F32),
        compiler_params=_cparams("arbitrary", "arbitrary"),
        name="nsa_compress",
    )(page_table, *([cache] * (2 * n_ops)), pos2, wlo, whi, w2b)


def _attend_tiles(qs, k_ref, v_ref, acc_ref, lo, hi, logits_fn):
    acc_ref[...] = jnp.zeros_like(acc_ref)

    def body(n, carry):
        m, l = carry
        off = pl.multiple_of(n * T, T)
        s = logits_fn(n, _dot_nt(qs, k_ref[pl.ds(off, T), :]))
        p, alpha, m, l = _online_update(s, m, l)
        acc_ref[...] = acc_ref[...] * alpha + _dot(p.astype(MXU_DTYPE), v_ref[pl.ds(off, T), :])
        return m, l

    rows = qs.shape[0]
    _, l = lax.fori_loop(lo, hi, body, (jnp.full((rows, 1), NEG, F32), jnp.zeros((rows, 1), F32)))
    return acc_ref[...] / l


def _rel_tile(qi, n):
    i = lax.broadcasted_iota(jnp.int32, (T, T), 0)
    j = lax.broadcasted_iota(jnp.int32, (T, T), 1)
    return (i - j) + (qi - n) * T


def _cast_tiles(src_ref, col, width, dst_ref, seq):
    for n in range(seq // T):
        dst_ref[n * T:(n + 1) * T, :] = src_ref[0, n * T:(n + 1) * T, col:col + width].astype(dst_ref.dtype)


def _moba_prompt_kernel(q_ref, kv_ref, bias_ref, o_ref, kb_ref, vb_ref, mean_ref, acc_ref, *, seq):
    qi = pl.program_id(1)
    nb = seq // MOBA_BLOCK

    @pl.when(qi == 0)
    def _():
        _cast_tiles(kv_ref, 0, A_W, kb_ref, seq)
        _cast_tiles(kv_ref, A_W, A_W, vb_ref, seq)
        mean_ref[...] = jnp.zeros_like(mean_ref)
        for n in range(nb):
            mean_ref[n:n + 1, :] = jnp.mean(kv_ref[0, n * T:(n + 1) * T, 0:A_W], axis=0, keepdims=True)

    q = q_ref[0]
    lane_q = _lane_iota(A_W)
    lane_b = _lane_iota(LANES)
    means = mean_ref[...].astype(MXU_DTYPE)
    scale = HEAD_DIM ** -0.5
    out = jnp.zeros((T, A_W), F32)
    for h in range(A_HEADS):
        hm = _in_range(lane_q, h * HEAD_DIM, (h + 1) * HEAD_DIM)
        qh = jnp.where(hm, q, jnp.zeros_like(q))
        gate = _dot_nt(qh, means)
        rank = jnp.zeros((T, LANES), F32)
        for mb in range(nb):
            col = gate[:, mb:mb + 1]
            beats = (col > gate) | ((col == gate) & (lane_b > mb))
            rank = rank + jnp.where(beats, jnp.where(qi > mb, 1.0, 0.0), 0.0)
        sel = jnp.where((lane_b < qi) & (rank < MOBA_TOPK), 1.0, 0.0)

        def logits_fn(n, s, h=h, sel=sel):
            s = s * scale + bias_ref[h, jnp.minimum(qi - n, 2)]
            seln = jnp.max(jnp.where(lane_b == n, sel, 0.0), axis=-1, keepdims=True) + jnp.where(qi == n, 1.0, 0.0)
            ok = (_rel_tile(qi, n) >= 0) & (seln > 0.5)
            return jnp.where(ok, s, NEG)

        o = _attend_tiles(qh, kb_ref, vb_ref, acc_ref, 0, qi + 1, logits_fn)
        out = jnp.where(hm, o, out)
    o_ref[0] = out.astype(o_ref.dtype)


def _moba_prompt(qa, row, bias_a):
    b, seq, _ = row.shape
    assert seq % T == 0 and seq // MOBA_BLOCK >= MOBA_TOPK and seq // MOBA_BLOCK <= LANES
    return pl.pallas_call(
        functools.partial(_moba_prompt_kernel, seq=seq),
        grid=(b, seq // T),
        in_specs=[pl.BlockSpec((1, T, A_W), lambda i, j: (i, j, 0)),
                  pl.BlockSpec((1, seq, 2 * A_W), lambda i, j: (i, 0, KA_OFF // (2 * A_W))),
                  pl.BlockSpec(bias_a.shape, lambda i, j: (0, 0, 0, 0))],
        out_specs=pl.BlockSpec((1, T, A_W), lambda i, j: (i, j, 0)),
        out_shape=jax.ShapeDtypeStruct((b, seq, A_W), MXU_DTYPE),
        scratch_shapes=[pltpu.VMEM((seq, A_W), MXU_DTYPE), pltpu.VMEM((seq, A_W), MXU_DTYPE),
                        pltpu.VMEM((LANES, A_W), F32), pltpu.VMEM((T, A_W), F32)],
        compiler_params=_cparams("arbitrary", "arbitrary"),
        name="moba_prompt",
    )(qa, row, bias_a)


def _diff_finish(o0, o1, lam, lane_v):
    o = o0 - lam * o1
    out = jnp.zeros_like(o)
    for h in range(B_HEADS):
        hm = _in_range(lane_v, h * B_V_DIM, (h + 1) * B_V_DIM)
        ms = jnp.sum(jnp.where(hm, o * o, 0.0), axis=-1, keepdims=True) * (1.0 / B_V_DIM)
        out = jnp.where(hm, o * lax.rsqrt(ms + LN_EPS), out)
    return out


def _diff_prompt_kernel(dp_ref, q_ref, kv_ref, bias_ref, subw_ref, o_ref, kb_ref, vb_ref, acc_ref, *, seq):
    qi = pl.program_id(1)

    @pl.when(qi == 0)
    def _():
        _cast_tiles(kv_ref, 0, B_QK_W, kb_ref, seq)
        _cast_tiles(kv_ref, B_QK_W, B_W, vb_ref, seq)

    q = q_ref[0]
    lane_q = _lane_iota(B_QK_W)
    lane_v = _lane_iota(B_W)
    scale = B_QK_DIM ** -0.5
    maps = [jnp.zeros((T, B_W), F32), jnp.zeros((T, B_W), F32)]
    for h in range(B_HEADS):
        hm = _in_range(lane_v, h * B_V_DIM, (h + 1) * B_V_DIM)

        def logits_fn(n, s, h=h):
            s = s * scale + bias_ref[h, jnp.minimum(qi - n, 2)]
            return jnp.where(_rel_tile(qi, n) >= 0, s, NEG)

        for mp in range(2):
            c0 = (2 * h + mp) * B_QK_DIM
            qhm = jnp.where(_in_range(lane_q, c0, c0 + B_QK_DIM), q, jnp.zeros_like(q))
            o = _attend_tiles(qhm, kb_ref, vb_ref, acc_ref, 0, qi + 1, logits_fn)
            maps[mp] = jnp.where(hm, o, maps[mp])
    out = _diff_finish(maps[0], maps[1], dp_ref[:, 0:1], lane_v) * subw_ref[...] * dp_ref[:, 1:2]
    o_ref[0] = out.astype(o_ref.dtype)


def _diff_prompt(dp, qb, row, bias_b, subw):
    b, seq, _ = row.shape
    kvw = B_QK_W + B_W
    return pl.pallas_call(
        functools.partial(_diff_prompt_kernel, seq=seq),
        grid=(b, seq // T),
        in_specs=[pl.BlockSpec((1, LANES), lambda i, j: (0, 0)),
                  pl.BlockSpec((1, T, B_QK_W), lambda i, j: (i, j, 0)),
                  pl.BlockSpec((1, seq, kvw), lambda i, j: (i, 0, KB_OFF // kvw)),
                  pl.BlockSpec(bias_b.shape, lambda i, j: (0, 0, 0, 0)),
                  pl.BlockSpec((1, B_W), lambda i, j: (0, 0))],
        out_specs=pl.BlockSpec((1, T, B_W), lambda i, j: (i, j, 0)),
        out_shape=jax.ShapeDtypeStruct((b, seq, B_W), MXU_DTYPE),
        scratch_shapes=[pltpu.VMEM((seq, B_QK_W), MXU_DTYPE), pltpu.VMEM((seq, B_W), MXU_DTYPE),
                        pltpu.VMEM((T, B_W), F32)],
        compiler_params=_cparams("arbitrary", "arbitrary"),
        name="diff_prompt",
    )(dp, qb, row, bias_b, subw)


def _stack_group_queries(q, g):
    gm = _in_range(_lane_iota(C_KV_W), g * HEAD_DIM, (g + 1) * HEAD_DIM)
    q = q.astype(F32)
    return jnp.concatenate(
        [jnp.where(gm, q[:, hg * C_KV_W:(hg + 1) * C_KV_W], 0.0) for hg in range(C_HPG)], axis=0).astype(MXU_DTYPE)


def _stack_gate(gt, g, branch):
    cols = [gt[:, (g * C_HPG + hg) * N_BRANCH + branch:(g * C_HPG + hg) * N_BRANCH + branch + 1] for hg in range(C_HPG)]
    x = jnp.concatenate(cols, axis=0)
    return 1.0 / (1.0 + jnp.exp(-x))


def _cmp_attention(qs, kc, vc, q_pos_rows, n_cmp):
    nch = kc.shape[0]
    s = _dot_nt(qs, kc) * (HEAD_DIM ** -0.5)
    r = _lane_iota(nch)
    ok = (r >= 1) & (r <= n_cmp) & ((r - 1) * CMP_STRIDE + (CMP_LEN - 1) <= q_pos_rows)
    s = jnp.where(ok, s, NEG)
    p = jnp.where(ok, jnp.exp(s - jnp.max(s, axis=-1, keepdims=True)), 0.0)
    l = jnp.sum(p, axis=-1, keepdims=True)
    p = p / jnp.where(l > 0.0, l, 1.0)
    return p, _dot(p.astype(MXU_DTYPE), vc)


def _nsa_prompt_kernel(q_ref, gt_ref, cmp_ref, kv_ref, win_ref, bias_ref, wsel_ref, exp_ref, o_ref,
                       ks_ref, vs_ref, kw_ref, vw_ref, kc_ref, vc_ref, acc_ref, *, seq, n_cmp):
    qi = pl.program_id(1)
    nsel = seq // SEL_BLOCK

    @pl.when(qi == 0)
    def _():
        _cast_tiles(kv_ref, 0, C_KV_W, ks_ref, seq)
        _cast_tiles(kv_ref, C_KV_W, C_KV_W, vs_ref, seq)
        _cast_tiles(win_ref, 0, C_KV_W, kw_ref, seq)
        _cast_tiles(win_ref, C_KV_W, C_KV_W, vw_ref, seq)
        kc_ref[...] = cmp_ref[0, :, 0:C_KV_W].astype(kc_ref.dtype)
        vc_ref[...] = cmp_ref[0, :, C_KV_W:2 * C_KV_W].astype(vc_ref.dtype)

    q = q_ref[0]
    gt = gt_ref[0]
    lane = _lane_iota(LANES)
    q_pos = qi * T + lax.broadcasted_iota(jnp.int32, (T, 1), 0)
    q_pos4 = jnp.concatenate([q_pos] * C_HPG, axis=0)
    scale = HEAD_DIM ** -0.5
    outs = []
    for g in range(C_GROUPS):
        qs = _stack_group_queries(q, g)
        p, o_cmp = _cmp_attention(qs, kc_ref[...], vc_ref[...], q_pos4, n_cmp)
        pg = p[0:T] + p[T:2 * T] + p[2 * T:3 * T] + p[3 * T:4 * T]
        ph, plo = _split_hi_lo(pg)
        p_slc = _dot(ph, wsel_ref[...]) + _dot(plo, wsel_ref[...])
        own = jnp.right_shift(q_pos, int(math.log2(SEL_BLOCK)))
        forced = (lane == 0) | (lane == own) | (lane == own - 1)
        score = jnp.where(lane <= own, jnp.where(forced, jnp.inf, p_slc), -jnp.inf)
        rank = jnp.zeros((T, LANES), F32)
        for mb in range(nsel):
            col = score[:, mb:mb + 1]
            rank = rank + jnp.where((col > score) | ((col == score) & (lane > mb)), 1.0, 0.0)
        sel = jnp.where((lane <= own) & (rank < SEL_TOPN), 1.0, 0.0).astype(MXU_DTYPE)

        def slc_logits(n, s, g=g, sel=sel):
            bias = bias_ref[jnp.minimum(qi - n, 2), g * C_HPG:(g + 1) * C_HPG].reshape(C_HPG * T, T)
            ok = (_dot(sel, exp_ref[n]) > 0.5) & (_rel_tile(qi, n) >= 0)
            return jnp.where(jnp.concatenate([ok] * C_HPG, axis=0), s * scale + bias, NEG)

        o_slc = _attend_tiles(qs, ks_ref, vs_ref, acc_ref, 0, qi + 1, slc_logits)

        def win_logits(n, s, g=g):
            bias = bias_ref[jnp.minimum(qi - n, 2), g * C_HPG:(g + 1) * C_HPG].reshape(C_HPG * T, T)
            rel = _rel_tile(qi, n)
            ok = (rel >= 0) & (rel < WINDOW)
            return jnp.where(jnp.concatenate([ok] * C_HPG, axis=0), s * scale + bias, NEG)

        o_win = _attend_tiles(qs, kw_ref, vw_ref, acc_ref, jnp.maximum(qi - WINDOW // T, 0), qi + 1, win_logits)
        outs.append(_stack_gate(gt, g, 0) * o_cmp + _stack_gate(gt, g, 1) * o_slc + _stack_gate(gt, g, 2) * o_win)
    g0 = lane < HEAD_DIM
    for hg in range(C_HPG):
        chunk = jnp.where(g0, outs[0][hg * T:(hg + 1) * T], outs[1][hg * T:(hg + 1) * T])
        o_ref[0, :, hg * C_KV_W:(hg + 1) * C_KV_W] = chunk.astype(o_ref.dtype)


def _nsa_prompt(qc, gt, cmp, row, win, bias_c, wsel, expand):
    b, seq, _ = row.shape
    nch = cmp.shape[1]
    n_cmp = (seq - CMP_LEN) // CMP_STRIDE + 1
    assert seq // SEL_BLOCK <= LANES and seq // SEL_BLOCK >= SEL_TOPN and WINDOW % T == 0
    kvw = 2 * C_KV_W
    full = lambda a: pl.BlockSpec(a.shape, lambda i, j: (0,) * a.ndim)
    return pl.pallas_call(
        functools.partial(_nsa_prompt_kernel, seq=seq, n_cmp=n_cmp),
        grid=(b, seq // T),
        in_specs=[pl.BlockSpec((1, T, C_W), lambda i, j: (i, j, 0)),
                  pl.BlockSpec((1, T, GT_PAD), lambda i, j: (i, j, 0)),
                  pl.BlockSpec((1, nch, kvw), lambda i, j: (i, 0, 0)),
                  pl.BlockSpec((1, seq, kvw), lambda i, j: (i, 0, KS_OFF // kvw)),
                  pl.BlockSpec((1, seq, kvw), lambda i, j: (i, 0, 0)),
                  full(bias_c), full(wsel), full(expand)],
        out_specs=pl.BlockSpec((1, T, C_W), lambda i, j: (i, j, 0)),
        out_shape=jax.ShapeDtypeStruct((b, seq, C_W), MXU_DTYPE),
        scratch_shapes=[pltpu.VMEM((seq, C_KV_W), MXU_DTYPE)] * 4 + [pltpu.VMEM((nch, C_KV_W), MXU_DTYPE)] * 2
        + [pltpu.VMEM((C_HPG * T, C_KV_W), F32)],
        compiler_params=_cparams("arbitrary", "arbitrary"),
        name="nsa_prompt",
    )(qc, gt, cmp, row, win, bias_c, wsel, expand)


PG = 16
PG_CMP = 16
NEW_PAD = LANES


def _rows_iota(n_rep, n):
    return jnp.concatenate([lax.broadcasted_iota(jnp.int32, (n, 1), 0)] * n_rep, axis=0)


def _pad_rows(x, rows):
    return jnp.concatenate([x, jnp.zeros((rows - x.shape[0], x.shape[1]), x.dtype)], axis=0)


def _page_specs(layer, width, col_block, n_ops):
    def page_map(b, s, pt, *, k):
        return (layer, pt[b, s * n_ops + k], 0, col_block)
    return [pl.BlockSpec((1, 1, PAGE_SIZE, width), functools.partial(page_map, k=k)) for k in range(n_ops)]


def _step_bias_spec(shape, steps):
    return pl.BlockSpec((1,) + shape[1:], lambda b, s, pt: (jnp.where(s == steps - 1, 1, 0),) + (0,) * (len(shape) - 1))


def _new_token_logits(qs, k_new, bias_new, scale, nq):
    s = _dot_nt(qs, _pad_rows(k_new, NEW_PAD).astype(MXU_DTYPE)) * scale + bias_new
    j = _lane_iota(NEW_PAD)
    i = _rows_iota(qs.shape[0] // nq, nq)
    return jnp.where((j < nq) & (j <= i), s, NEG)


def _nsa_sample_pre_kernel(q_ref, cmp_ref, state_ref, winnew_ref, bias_ref, wsel_ref, sel_ref, ocw_ref,
                           *, past, n_cmp, nq):
    q = q_ref[0]
    wb = state_ref.shape[2]
    wk = bias_ref.shape[-1]
    rows = C_HPG * nq
    kc = cmp_ref[0, :, 0:C_KV_W].astype(MXU_DTYPE)
    vc = cmp_ref[0, :, C_KV_W:2 * C_KV_W].astype(MXU_DTYPE)
    pad = jnp.zeros((wk - wb - nq, C_KV_W), F32)
    kw = jnp.concatenate([state_ref[0, 0, :, 0:C_KV_W], winnew_ref[0, :, 0:C_KV_W], pad], axis=0).astype(MXU_DTYPE)
    vw = jnp.concatenate([state_ref[0, 0, :, C_KV_W:], winnew_ref[0, :, C_KV_W:], pad], axis=0).astype(MXU_DTYPE)
    nsl = wsel_ref.shape[1]
    lane = _lane_iota(nsl)
    big = jnp.int32(nsl)
    i_q = lax.broadcasted_iota(jnp.int32, (nq, 1), 0)
    q_pos = past + i_q
    q_pos4 = past + _rows_iota(C_HPG, nq)
    scale = HEAD_DIM ** -0.5
    for g in range(C_GROUPS):
        qs = _stack_group_queries(q, g)
        p, o_cmp = _cmp_attention(qs, kc, vc, q_pos4, n_cmp)
        pg = p[0:nq] + p[nq:2 * nq] + p[2 * nq:3 * nq] + p[3 * nq:4 * nq]
        ph, plo = _split_hi_lo(pg)
        p_slc = _dot(ph, wsel_ref[...]) + _dot(plo, wsel_ref[...])
        own = jnp.right_shift(q_pos, int(math.log2(SEL_BLOCK)))
        forced = (lane == 0) | (lane == own) | (lane == own - 1)
        score = jnp.where(lane <= own, jnp.where(forced, jnp.inf, p_slc), -jnp.inf)
        sel = jnp.zeros((nq, nsl), F32)
        for _ in range(SEL_TOPN):
            v = jnp.max(score, axis=-1, keepdims=True)
            idx = jnp.min(jnp.where(score == v, lane, big), axis=-1, keepdims=True)
            hit = lane == idx
            sel = jnp.where(hit & (v > -jnp.inf), 1.0, sel)
            score = jnp.where(hit, -jnp.inf, score)
        sel_ref[0, g * rows:(g + 1) * rows, :] = jnp.concatenate([sel] * C_HPG, axis=0)
        s = _dot_nt(qs, kw) * scale + bias_ref[g * C_HPG:(g + 1) * C_HPG].reshape(rows, wk)
        j = _lane_iota(wk)
        rel = wb + _rows_iota(C_HPG, nq) - j
        ok = (j < wb + nq) & (rel >= 0) & (rel < WINDOW)
        s = jnp.where(ok, s, NEG)
        pw = jnp.exp(s - jnp.max(s, axis=-1, keepdims=True))
        o_win = _dot(pw.astype(MXU_DTYPE), vw) / jnp.sum(pw, axis=-1, keepdims=True)
        ocw_ref[0, g * rows:(g + 1) * rows, 0:C_KV_W] = o_cmp
        ocw_ref[0, g * rows:(g + 1) * rows, C_KV_W:] = o_win


def _nsa_sample_pre(qc, cmp, state_win, layer, win_new, bias_w, wsel, past):
    bs, nq, _ = qc.shape
    nch = cmp.shape[1]
    wb = state_win.shape[2]
    n_cmp = (past + nq - CMP_LEN) // CMP_STRIDE + 1
    rows = C_HEADS * nq
    full = lambda a: pl.BlockSpec(a.shape, lambda b: (0,) * a.ndim)
    return pl.pallas_call(
        functools.partial(_nsa_sample_pre_kernel, past=past, n_cmp=n_cmp, nq=nq),
        grid=(bs,),
        in_specs=[pl.BlockSpec((1, nq, C_W), lambda b: (b, 0, 0)),
                  pl.BlockSpec((1, nch, 2 * C_KV_W), lambda b: (b, 0, 0)),
                  pl.BlockSpec((1, 1, wb, WIN_ROW), lambda b: (layer, b, 0, 0)),
                  pl.BlockSpec((1, nq, WIN_ROW), lambda b: (b, 0, 0)),
                  full(bias_w), full(wsel)],
        out_specs=[pl.BlockSpec((1, rows, wsel.shape[1]), lambda b: (b, 0, 0)),
                   pl.BlockSpec((1, rows, 2 * C_KV_W), lambda b: (b, 0, 0))],
        out_shape=[jax.ShapeDtypeStruct((bs, rows, wsel.shape[1]), F32),
                   jax.ShapeDtypeStruct((bs, rows, 2 * C_KV_W), F32)],
        compiler_params=_cparams("arbitrary"),
        name="nsa_sample_pre",
    )(qc, cmp, state_win, win_new, bias_w, wsel)


def _nsa_sample_slc_kernel(pt_ref, *refs, nq):
    del pt_ref
    pages = refs[:PG]
    (q_ref, gt_ref, selw_ref, exp_ref, bias_ref, rownew_ref, biasnew_ref, ocw_ref,
     o_ref, m_ref, l_ref, acc_ref) = refs[PG:]
    step = pl.program_id(1)
    rows = C_HPG * nq
    scale = HEAD_DIM ** -0.5

    @pl.when(step == 0)
    def _():
        m_ref[...] = jnp.full_like(m_ref, NEG)
        l_ref[...] = jnp.zeros_like(l_ref)
        acc_ref[...] = jnp.zeros_like(acc_ref)

    k = jnp.concatenate([p[0, 0, :, 0:C_KV_W] for p in pages], axis=0).astype(MXU_DTYPE)
    v = jnp.concatenate([p[0, 0, :, C_KV_W:] for p in pages], axis=0).astype(MXU_DTYPE)
    q = q_ref[0]
    for g in range(C_GROUPS):
        rs = slice(g * rows, (g + 1) * rows)
        qs = _stack_group_queries(q, g)
        s = _dot_nt(qs, k) * scale + bias_ref[0, g * C_HPG:(g + 1) * C_HPG].reshape(rows, PG * PAGE_SIZE)
        ok = _dot(selw_ref[0, 0, rs, :].astype(MXU_DTYPE), exp_ref[...]) > 0.5
        p, alpha, m, l = _online_update(jnp.where(ok, s, NEG), m_ref[rs], l_ref[rs])
        m_ref[rs] = m
        l_ref[rs] = l
        acc_ref[rs] = acc_ref[rs] * alpha + _dot(p.astype(MXU_DTYPE), v)

    @pl.when(step == pl.num_programs(1) - 1)
    def _():
        gt = gt_ref[0]
        k_new = rownew_ref[0, :, 0:C_KV_W]
        v_new = _pad_rows(rownew_ref[0, :, C_KV_W:], NEW_PAD).astype(MXU_DTYPE)
        outs = []
        for g in range(C_GROUPS):
            rs = slice(g * rows, (g + 1) * rows)
            qs = _stack_group_queries(q, g)
            s = _new_token_logits(qs, k_new, biasnew_ref[g * C_HPG:(g + 1) * C_HPG].reshape(rows, NEW_PAD), scale, nq)
            p, alpha, _, l = _online_update(s, m_ref[rs], l_ref[rs])
            o_slc = (acc_ref[rs] * alpha + _dot(p.astype(MXU_DTYPE), v_new)) / l
            ocw = ocw_ref[0, rs, :]
            outs.append(_stack_gate(gt, g, 0) * ocw[:, 0:C_KV_W] + _stack_gate(gt, g, 1) * o_slc
                        + _stack_gate(gt, g, 2) * ocw[:, C_KV_W:])
        g0 = _lane_iota(C_KV_W) < HEAD_DIM
        for hg in range(C_HPG):
            chunk = jnp.where(g0, outs[0][hg * nq:(hg + 1) * nq], outs[1][hg * nq:(hg + 1) * nq])
            o_ref[0, :, hg * C_KV_W:(hg + 1) * C_KV_W] = chunk.astype(o_ref.dtype)


def _nsa_sample_slc(cache, layer, page_table, qc, gt, selw, expand, bias_steps, row_new, bias_new, ocw):
    bs, nq, _ = qc.shape
    steps = page_table.shape[1] // PG
    rows = C_HEADS * nq
    kvw = 2 * C_KV_W
    per_seq = lambda shape: pl.BlockSpec((1,) + shape, lambda b, s, pt: (b,) + (0,) * len(shape))
    full = lambda a: pl.BlockSpec(a.shape, lambda b, s, pt: (0,) * a.ndim)
    grid_spec = pltpu.PrefetchScalarGridSpec(
        num_scalar_prefetch=1,
        grid=(bs, steps),
        in_specs=_page_specs(layer, kvw, KS_OFF // kvw, PG)
        + [per_seq((nq, C_W)), per_seq((nq, GT_PAD)),
           pl.BlockSpec((1, 1, rows, LANES), lambda b, s, pt: (b, s, 0, 0)),
           full(expand), _step_bias_spec(bias_steps.shape, steps),
           pl.BlockSpec((1, nq, kvw), lambda b, s, pt: (b, 0, KS_OFF // kvw)),
           full(bias_new), per_seq((rows, kvw))],
        out_specs=per_seq((nq, C_W)),
        scratch_shapes=[pltpu.VMEM((rows, 1), F32), pltpu.VMEM((rows, 1), F32), pltpu.VMEM((rows, C_KV_W), F32)],
    )
    return pl.pallas_call(
        functools.partial(_nsa_sample_slc_kernel, nq=nq),
        grid_spec=grid_spec,
        out_shape=jax.ShapeDtypeStruct((bs, nq, C_W), F32),
        compiler_params=_cparams("arbitrary", "arbitrary"),
        name="nsa_sample_slc",
    )(page_table, *([cache] * PG), qc, gt, selw, expand, bias_steps, row_new, bias_new, ocw)


def _diff_sample_kernel(pt_ref, *refs, nq):
    del pt_ref
    pages = refs[:PG]
    dp_ref, q_ref, bias_ref, rownew_ref, biasnew_ref, subw_ref, o_ref, m_ref, l_ref, acc_ref = refs[PG:]
    step = pl.program_id(1)
    scale = B_QK_DIM ** -0.5

    @pl.when(step == 0)
    def _():
        m_ref[...] = jnp.full_like(m_ref, NEG)
        l_ref[...] = jnp.zeros_like(l_ref)
        acc_ref[...] = jnp.zeros_like(acc_ref)

    q = q_ref[0]
    lane_q = _lane_iota(B_QK_W)
    qs = jnp.concatenate([jnp.where(_in_range(lane_q, c * B_QK_DIM, (c + 1) * B_QK_DIM), q, 0.0)
                          for c in range(2 * B_HEADS)], axis=0).astype(MXU_DTYPE)

    def head_rows(b):
        return jnp.concatenate([b[c // 2] for c in range(2 * B_HEADS)], axis=0)

    k = jnp.concatenate([p[0, 0, :, 0:B_QK_W] for p in pages], axis=0).astype(MXU_DTYPE)
    v = jnp.concatenate([p[0, 0, :, B_QK_W:] for p in pages], axis=0).astype(MXU_DTYPE)
    s = _dot_nt(qs, k) * scale + head_rows(bias_ref[0])
    p, alpha, m, l = _online_update(s, m_ref[...], l_ref[...])
    m_ref[...] = m
    l_ref[...] = l
    acc_ref[...] = acc_ref[...] * alpha + _dot(p.astype(MXU_DTYPE), v)

    @pl.when(step == pl.num_programs(1) - 1)
    def _():
        k_new = rownew_ref[0, :, 0:B_QK_W]
        v_new = _pad_rows(rownew_ref[0, :, B_QK_W:], NEW_PAD).astype(MXU_DTYPE)
        s = _new_token_logits(qs, k_new, head_rows(biasnew_ref[...]), scale, nq)
        p, alpha, _, l = _online_update(s, m_ref[...], l_ref[...])
        o = (acc_ref[...] * alpha + _dot(p.astype(MXU_DTYPE), v_new)) / l
        lane_v = _lane_iota(B_W)
        maps = [jnp.zeros((nq, B_W), F32), jnp.zeros((nq, B_W), F32)]
        for c in range(2 * B_HEADS):
            hm = _in_range(lane_v, (c // 2) * B_V_DIM, (c // 2 + 1) * B_V_DIM)
            maps[c % 2] = jnp.where(hm, o[c * nq:(c + 1) * nq], maps[c % 2])
        out = _diff_finish(maps[0], maps[1], dp_ref[:, 0:1], lane_v) * subw_ref[...] * dp_ref[:, 1:2]
        o_ref[0] = out.astype(o_ref.dtype)


def _diff_sample(cache, layer, page_table, dp, qb, bias_steps, row_new, bias_new, subw):
    bs, nq, _ = qb.shape
    steps = page_table.shape[1] // PG
    rows = 2 * B_HEADS * nq
    kvw = B_QK_W + B_W
    per_seq = lambda shape: pl.BlockSpec((1,) + shape, lambda b, s, pt: (b,) + (0,) * len(shape))
    full = lambda a: pl.BlockSpec(a.shape, lambda b, s, pt: (0,) * a.ndim)
    grid_spec = pltpu.PrefetchScalarGridSpec(
        num_scalar_prefetch=1,
        grid=(bs, steps),
        in_specs=_page_specs(layer, kvw, KB_OFF // kvw, PG)
        + [full(dp), per_seq((nq, B_QK_W)), _step_bias_spec(bias_steps.shape, steps),
           pl.BlockSpec((1, nq, kvw), lambda b, s, pt: (b, 0, KB_OFF // kvw)), full(bias_new), full(subw)],
        out_specs=per_seq((nq, B_W)),
        scratch_shapes=[pltpu.VMEM((rows, 1), F32), pltpu.VMEM((rows, 1), F32), pltpu.VMEM((rows, B_W), F32)],
    )
    return pl.pallas_call(
        functools.partial(_diff_sample_kernel, nq=nq),
        grid_spec=grid_spec,
        out_shape=jax.ShapeDtypeStruct((bs, nq, B_W), F32),
        compiler_params=_cparams("arbitrary", "arbitrary"),
        name="diff_sample",
    )(page_table, *([cache] * PG), dp, qb, bias_steps, row_new, bias_new, subw)


def _moba_sample_kernel(pt_ref, *refs, nq, n_past_blocks):
    del pt_ref
    pages = refs[:PG]
    q_ref, bias_ref, rownew_ref, biasnew_ref, o_ref, oblk_ref, gate_ref, mst_ref, lst_ref = refs[PG:]
    step = pl.program_id(1)
    rows = A_HEADS * nq
    bps = PG * PAGE_SIZE // MOBA_BLOCK
    scale = HEAD_DIM ** -0.5
    nbl = gate_ref.shape[1]
    lane_b = _lane_iota(nbl)
    lane_q = _lane_iota(A_W)
    q = q_ref[0]
    qf = jnp.concatenate([jnp.where(_in_range(lane_q, h * HEAD_DIM, (h + 1) * HEAD_DIM), q, 0.0)
                          for h in range(A_HEADS)], axis=0)
    qs = qf.astype(MXU_DTYPE)

    def head_rows(b):
        return jnp.concatenate([b[h] for h in range(A_HEADS)], axis=0)

    @pl.when(step == 0)
    def _():
        gate_ref[...] = jnp.zeros_like(gate_ref)
        mst_ref[...] = jnp.zeros_like(mst_ref)
        lst_ref[...] = jnp.zeros_like(lst_ref)

    kf = jnp.concatenate([p[0, 0, :, 0:A_W] for p in pages], axis=0)
    k = kf.astype(MXU_DTYPE)
    v = jnp.concatenate([p[0, 0, :, A_W:] for p in pages], axis=0).astype(MXU_DTYPE)
    s = _dot_nt(qs, k) * scale + head_rows(bias_ref[0])
    gate, mst, lst = gate_ref[...], mst_ref[...], lst_ref[...]
    for j in range(bps):
        cs = slice(j * MOBA_BLOCK, (j + 1) * MOBA_BLOCK)
        sj = s[:, cs]
        mj = jnp.max(sj, axis=-1, keepdims=True)
        pj = jnp.exp(sj - mj)
        n = step * bps + j
        oblk_ref[n] = _dot(pj.astype(MXU_DTYPE), v[cs, :])
        mean = jnp.mean(kf[cs, :], axis=0, keepdims=True)
        hit = lane_b == n
        gate = jnp.where(hit, jnp.sum(qf * mean, axis=-1, keepdims=True), gate)
        mst = jnp.where(hit, mj, mst)
        lst = jnp.where(hit, jnp.sum(pj, axis=-1, keepdims=True), lst)
    gate_ref[...] = gate
    mst_ref[...] = mst
    lst_ref[...] = lst

    @pl.when(step == pl.num_programs(1) - 1)
    def _():
        big = jnp.int32(nbl)
        score = jnp.where(lane_b < n_past_blocks, gate, -jnp.inf)
        sel = jnp.zeros((rows, nbl), F32)
        for _ in range(MOBA_TOPK):
            vmax = jnp.max(score, axis=-1, keepdims=True)
            idx = jnp.min(jnp.where(score == vmax, lane_b, big), axis=-1, keepdims=True)
            hit = lane_b == idx
            sel = jnp.where(hit & (vmax > -jnp.inf), 1.0, sel)
            score = jnp.where(hit, -jnp.inf, score)
        k_new = rownew_ref[0, :, 0:A_W]
        v_new = _pad_rows(rownew_ref[0, :, A_W:], NEW_PAD).astype(MXU_DTYPE)
        s_new = _new_token_logits(qs, k_new, head_rows(biasnew_ref[...]), scale, nq)
        m_new = jnp.max(s_new, axis=-1, keepdims=True)
        p_new = jnp.exp(s_new - m_new)
        chosen = sel > 0.5
        m_all = jnp.maximum(m_new, jnp.max(jnp.where(chosen, mst, NEG), axis=-1, keepdims=True))
        w = jnp.where(chosen, jnp.exp(mst - m_all), 0.0)
        w_new = jnp.exp(m_new - m_all)
        l_all = jnp.sum(w * lst, axis=-1, keepdims=True) + w_new * jnp.sum(p_new, axis=-1, keepdims=True)
        o = w_new * _dot(p_new.astype(MXU_DTYPE), v_new)
        for n in range(n_past_blocks):
            o = o + w[:, n:n + 1] * oblk_ref[n]
        o = o / l_all
        out = jnp.zeros((nq, A_W), F32)
        for h in range(A_HEADS):
            out = jnp.where(_in_range(lane_q, h * HEAD_DIM, (h + 1) * HEAD_DIM), o[h * nq:(h + 1) * nq], out)
        o_ref[0] = out.astype(o_ref.dtype)


def _moba_sample(cache, layer, page_table, qa, bias_steps, row_new, bias_new, past):
    bs, nq, _ = qa.shape
    steps = page_table.shape[1] // PG
    rows = A_HEADS * nq
    kvw = 2 * A_W
    n_past_blocks = past // MOBA_BLOCK
    nbl = -(-n_past_blocks // LANES) * LANES
    per_seq = lambda shape: pl.BlockSpec((1,) + shape, lambda b, s, pt: (b,) + (0,) * len(shape))
    full = lambda a: pl.BlockSpec(a.shape, lambda b, s, pt: (0,) * a.ndim)
    grid_spec = pltpu.PrefetchScalarGridSpec(
        num_scalar_prefetch=1,
        grid=(bs, steps),
        in_specs=_page_specs(layer, kvw, KA_OFF // kvw, PG)
        + [per_seq((nq, A_W)), _step_bias_spec(bias_steps.shape, steps),
           pl.BlockSpec((1, nq, kvw), lambda b, s, pt: (b, 0, KA_OFF // kvw)), full(bias_new)],
        out_specs=per_seq((nq, A_W)),
        scratch_shapes=[pltpu.VMEM((n_past_blocks, rows, A_W), F32)] + [pltpu.VMEM((rows, nbl), F32)] * 3,
    )
    return pl.pallas_call(
        functools.partial(_moba_sample_kernel, nq=nq, n_past_blocks=n_past_blocks),
        grid_spec=grid_spec,
        out_shape=jax.ShapeDtypeStruct((bs, nq, A_W), F32),
        compiler_params=_cparams("arbitrary", "arbitrary"),
        name="moba_sample",
    )(page_table, *([cache] * PG), qa, bias_steps, row_new, bias_new)


def _t5_bucket(rel):
    n = jnp.maximum(rel, 0)
    nf = jnp.maximum(n, 1).astype(F32)
    large = MAX_EXACT + (jnp.log(nf / MAX_EXACT) / math.log(T5_MAX_DIST / MAX_EXACT)
                         * (N_BUCKETS - MAX_EXACT)).astype(jnp.int32)
    return jnp.where(n < MAX_EXACT, n, jnp.minimum(large, N_BUCKETS - 1))


def _nsa_perm():
    return np.array([g * C_HPG * HEAD_DIM + hg * HEAD_DIM + d
                     for hg in range(C_HPG) for g in range(C_GROUPS) for d in range(HEAD_DIM)], np.int32)


def _sel_weights(nch, n_cmp, nsl):
    r = np.arange(nch)[:, None]
    s = (r - 1) * CMP_STRIDE
    b0 = np.arange(nsl)[None, :] * SEL_BLOCK
    ov = np.clip(np.minimum(s + CMP_LEN, b0 + SEL_BLOCK) - np.maximum(s, b0), 0, None)
    ov = np.where((r >= 1) & (r <= n_cmp), ov, 0)
    return jnp.asarray(ov.astype(np.float32) / np.float32(CMP_STRIDE), MXU_DTYPE)


def _block_expand(n_tiles, lanes, keys_per_tile):
    key = np.arange(keys_per_tile)[None, None, :] + np.arange(n_tiles)[:, None, None] * keys_per_tile
    blk = np.arange(lanes)[None, :, None]
    return jnp.asarray((key // SEL_BLOCK == blk).astype(np.float32), MXU_DTYPE)


def _compress_weights(pos_k, pos_v, k_w1, k_w2, v_w1, v_w2):
    def first_layer(w1, half):
        w = w1[half * CMP_STRIDE * HEAD_DIM:(half + 1) * CMP_STRIDE * HEAD_DIM].reshape(CMP_STRIDE, HEAD_DIM, CMP_HIDDEN)
        z = jnp.zeros_like(w)
        blk = jnp.concatenate([jnp.concatenate([w, z], axis=2), jnp.concatenate([z, w], axis=2)], axis=1)
        return blk.reshape(CMP_STRIDE // 2, 2 * C_KV_W, C_GROUPS * CMP_HIDDEN)

    def second_layer(w2):
        z = jnp.zeros_like(w2)
        return jnp.concatenate([jnp.concatenate([w2, z], axis=1), jnp.concatenate([z, w2], axis=1)], axis=0)

    pos2 = jnp.concatenate([pos_k, pos_k, pos_v, pos_v], axis=1).astype(F32)
    wlo = jnp.stack([first_layer(k_w1, 0), first_layer(v_w1, 0)]).astype(MXU_DTYPE)
    whi = jnp.stack([first_layer(k_w1, 1), first_layer(v_w1, 1)]).astype(MXU_DTYPE)
    w2b = jnp.stack([second_layer(k_w2), second_layer(v_w2)]).astype(MXU_DTYPE)
    return pos2, wlo, whi, w2b


def kernel(x_prompt, x_sample, cache_kv, state_win, page_table, rel_bias_table, w_in, w_out, lam_q1, lam_k1, lam_q2, lam_k2, diff_subln_w, cmp_pos_k, cmp_pos_v, cmp_k_w1, cmp_k_w2, cmp_v_w1, cmp_v_w2, ln_mix_g, ln_mix_b, ln_ffn_g, ln_ffn_b, router_w1, router_b1, router_w2, router_b2, expert_w1, expert_w3, expert_w2):
    depth = w_in.shape[0]
    bp, seq, _ = x_prompt.shape
    bs, nq, _ = x_sample.shape
    n_pages = page_table.shape[1]
    past = n_pages * PAGE_SIZE
    wb = state_win.shape[2]
    alpha = (2.0 * depth) ** 0.25
    assert cache_kv.shape[2] == PAGE_SIZE and nq < CMP_STRIDE and nq % SUBLANES == 0
    assert past % MOBA_BLOCK == 0 and n_pages % PG == 0 and PG * PAGE_SIZE // SEL_BLOCK <= LANES
    assert wb == WINDOW and wb <= past

    perm = _nsa_perm()
    tab = rel_bias_table.astype(F32)
    bvec = tab[:, _t5_bucket(jnp.arange(BIAS_RANGE, dtype=jnp.int32))]

    def toeplitz(rel):
        return bvec[:, np.clip(rel, 0, BIAS_RANGE - 1)]

    ti = np.arange(T)[:, None] - np.arange(T)[None, :]
    bias_tiles = jnp.stack([toeplitz(ti + d * T) for d in range(3)], axis=1)
    ha, hb = A_HEADS, A_HEADS + B_HEADS
    bias_a_p, bias_b_p = bias_tiles[:ha], bias_tiles[ha:hb]
    bias_c_p = jnp.swapaxes(bias_tiles[hb:], 0, 1)
    qi = np.arange(nq)[:, None]
    far = jnp.broadcast_to(bvec[:, BIAS_RANGE - 1][:, None, None], (bvec.shape[0], nq, (PG - 1) * PAGE_SIZE))
    last_page = toeplitz(PAGE_SIZE + qi - np.arange(PAGE_SIZE)[None, :])
    bias_steps = jnp.stack([jnp.broadcast_to(bvec[:, BIAS_RANGE - 1][:, None, None], (bvec.shape[0], nq, PG * PAGE_SIZE)),
                            jnp.concatenate([far, last_page], axis=2)])
    bias_new = toeplitz(qi - np.arange(NEW_PAD)[None, :])
    wk = -(-(wb + nq) // LANES) * LANES
    bias_win = toeplitz(wb + qi - np.arange(wk)[None, :])[hb:]

    nch_p = seq // CMP_STRIDE
    wsel_p = _sel_weights(nch_p, (seq - CMP_LEN) // CMP_STRIDE + 1, LANES)
    expand_p = _block_expand(seq // T, LANES, T)
    nch_s = past // CMP_STRIDE
    nsel_s = past // SEL_BLOCK + 1
    steps = n_pages // PG
    bps = PG * PAGE_SIZE // SEL_BLOCK
    nsl = -(-max(nsel_s, steps * bps) // LANES) * LANES
    wsel_s = _sel_weights(nch_s, (past + nq - CMP_LEN) // CMP_STRIDE + 1, nsl)
    expand_s = _block_expand(1, LANES, PG * PAGE_SIZE)[0]
    pt_prompt = jnp.arange(bp, dtype=jnp.int32)[:, None]
    pg_cmp = math.gcd(PG_CMP, n_pages)

    xp = x_prompt.reshape(bp * seq, D_MODEL)
    xs = x_sample.reshape(bs * nq, D_MODEL)
    kv_p, win_p, kv_s, win_s = [], [], [], []
    for l in range(depth):
        w = w_in[l]
        w_packed = jnp.concatenate(
            [w[:, :A_W + B_QK_W], w[:, A_W + B_QK_W:A_W + B_QK_W + C_W][:, perm],
             jnp.pad(w[:, Q_W - GATE_W:Q_W], ((0, 0), (0, GT_PAD - GATE_W))), w[:, Q_W:]], axis=1).astype(MXU_DTYPE)
        wo = w_out[l]
        w_out_p = jnp.concatenate([wo[:A_W + B_W], wo[A_W + B_W:][perm]], axis=0).astype(MXU_DTYPE)
        lam_init = 0.8 - 0.6 * math.exp(-0.3 * l)
        lam = (jnp.exp(jnp.sum(lam_q1[l].astype(F32) * lam_k1[l].astype(F32)))
               - jnp.exp(jnp.sum(lam_q2[l].astype(F32) * lam_k2[l].astype(F32))) + lam_init)
        dp = jnp.zeros((1, LANES), F32).at[0, 0].set(lam).at[0, 1].set(1.0 - lam_init)
        subw = jnp.tile(diff_subln_w[l].astype(F32), B_HEADS)[None, :]
        cw = _compress_weights(cmp_pos_k[l], cmp_pos_v[l], cmp_k_w1[l], cmp_k_w2[l], cmp_v_w1[l], cmp_v_w2[l])
        wr = jnp.concatenate([router_w1[l], jnp.moveaxis(router_w2[l], 0, 1).reshape(D_MODEL, N_EXPERTS)], axis=1)
        wr = jnp.pad(wr.astype(F32), ((0, 0), (0, ROUTER_LANES - wr.shape[1])))
        wr_hi = wr.astype(MXU_DTYPE)
        wr_lo = (wr - wr_hi.astype(F32)).astype(MXU_DTYPE)
        br = jnp.concatenate([router_b1[l], router_b2[l].reshape(-1)]).astype(F32)
        br = jnp.pad(br, (0, ROUTER_LANES - br.shape[0]))[None, :]
        ew1, ew3, ew2 = expert_w1[l].astype(MXU_DTYPE), expert_w3[l].astype(MXU_DTYPE), expert_w2[l].astype(MXU_DTYPE)
        g_mix, b_mix = ln_mix_g[l].astype(F32)[None, :], ln_mix_b[l].astype(F32)[None, :]
        g_ffn, b_ffn = ln_ffn_g[l].astype(F32)[None, :], ln_ffn_b[l].astype(F32)[None, :]

        qa, qb, qc, gt, row, win = _inproj(xp, w_packed)
        r3 = lambda a: a.reshape(bp, seq, a.shape[-1])
        row3, win3 = r3(row), r3(win)
        cmp_p = _compress(row3[None], 0, pt_prompt, KC_OFF // C_KV_W, cw, 1)
        oa = _moba_prompt(r3(qa), row3, bias_a_p)
        ob = _diff_prompt(dp, r3(qb), row3, bias_b_p, subw)
        oc = _nsa_prompt(r3(qc), r3(gt), cmp_p, row3, win3, bias_c_p, wsel_p, expand_p)
        f2 = lambda a: a.reshape(bp * seq, a.shape[-1])
        xp = _outproj_ln(f2(oa), f2(ob), f2(oc), xp, w_out_p, g_mix, b_mix, alpha)
        xp = _moe_ln(xp, wr_hi, wr_lo, br, ew1, ew3, ew2, g_ffn, b_ffn, alpha)
        kv_p.append(row3)
        win_p.append(win3[:, seq - min(WINDOW, seq):])

        qa, qb, qc, gt, row, win = _inproj(xs, w_packed)
        s3 = lambda a: a.reshape(bs, nq, a.shape[-1]).astype(F32)
        row3, win3 = s3(row), s3(win)
        cmp_s = _compress(cache_kv, l, page_table, KC_OFF // C_KV_W, cw, pg_cmp)
        sel, ocw = _nsa_sample_pre(s3(qc), cmp_s, state_win, l, win3, bias_win, wsel_s, past)
        selw = sel[:, :, :steps * bps].reshape(bs, C_HEADS * nq, steps, bps)
        selw = jnp.pad(jnp.moveaxis(selw, 2, 1), ((0, 0), (0, 0), (0, 0), (0, LANES - bps)))
        oc = _nsa_sample_slc(cache_kv, l, page_table, s3(qc), s3(gt), selw, expand_s, bias_steps[:, hb:], row3,
                             bias_new[hb:], ocw)
        ob = _diff_sample(cache_kv, l, page_table, dp, s3(qb), bias_steps[:, ha:hb], row3, bias_new[ha:hb], subw)
        oa = _moba_sample(cache_kv, l, page_table, s3(qa), bias_steps[:, :ha], row3, bias_new[:ha], past)
        f2 = lambda a: a.reshape(bs * nq, a.shape[-1]).astype(MXU_DTYPE)
        xs = _outproj_ln(f2(oa), f2(ob), f2(oc), xs, w_out_p, g_mix, b_mix, alpha)
        xs = _moe_ln(xs, wr_hi, wr_lo, br, ew1, ew3, ew2, g_ffn, b_ffn, alpha)
        kv_s.append(row3)
        win_s.append(jnp.concatenate([state_win[l], win3.astype(state_win.dtype)], axis=1)[:, nq:])

    return (xp.reshape(bp, seq, D_MODEL), xs.reshape(bs, nq, D_MODEL),
            jnp.stack(kv_p), jnp.stack(win_p), jnp.stack(kv_s), jnp.stack(win_s))
```

```python
import functools
import math

import numpy as np
import jax
import jax.numpy as jnp
from jax import lax
from jax.experimental import pallas as pl
from jax.experimental.pallas import tpu as pltpu

D_MODEL = 1024
PAGE_SIZE = 128
HEAD_DIM = 64
A_HEADS = 4
MOBA_BLOCK = 256
MOBA_TOPK = 3
B_HEADS = 4
B_QK_DIM = 32
B_V_DIM = 2 * B_QK_DIM
C_HEADS = 8
C_GROUPS = 2
C_HPG = C_HEADS // C_GROUPS
CMP_LEN = 32
CMP_STRIDE = 16
CMP_HIDDEN = 256
SEL_BLOCK = 64
SEL_TOPN = 16
WINDOW = 512
N_BRANCH = 3
N_BUCKETS = 32
MAX_EXACT = N_BUCKETS // 2
T5_MAX_DIST = 128
A_W = A_HEADS * HEAD_DIM
B_QK_W = B_HEADS * 2 * B_QK_DIM
B_W = B_HEADS * B_V_DIM
C_W = C_HEADS * HEAD_DIM
C_KV_W = C_GROUPS * HEAD_DIM
MIX_W = A_W + B_W + C_W
GATE_W = C_HEADS * N_BRANCH
Q_W = A_W + B_QK_W + C_W + GATE_W
KV_ROW = 2 * A_W + B_QK_W + B_W + 4 * C_KV_W
WIN_ROW = 2 * C_KV_W
N_GROUPS = 4
EXPERTS_PER_GROUP = 8
N_EXPERTS = N_GROUPS * EXPERTS_PER_GROUP
EXPERT_FF = 256
LN_EPS = 1e-5
NEG = -1e30
F32 = jnp.float32
MXU_DTYPE = jnp.bfloat16

LANES = 128
SUBLANES = 8
VMEM_LIMIT_BYTES = 56 * 1024 * 1024

T = 256
GT_PAD = LANES
QA_OFF, QB_OFF, QC_OFF, GT_OFF = 0, A_W, A_W + B_QK_W, A_W + B_QK_W + C_W
ROW_OFF = GT_OFF + GT_PAD
WIN_OFF = ROW_OFF + KV_ROW
IN_W_PACKED = WIN_OFF + WIN_ROW
KA_OFF, VA_OFF, KB_OFF, VB_OFF = 0, A_W, 2 * A_W, 2 * A_W + B_QK_W
KC_OFF = 2 * A_W + B_QK_W + B_W
KS_OFF = KC_OFF + 2 * C_KV_W
BIAS_RANGE = 1024


def _cparams(*sem):
    return pltpu.CompilerParams(dimension_semantics=sem, vmem_limit_bytes=VMEM_LIMIT_BYTES)


def _dot(a, b):
    return jnp.dot(a, b, preferred_element_type=F32)


def _dot_nt(a, b):
    return lax.dot_general(a, b, (((1,), (1,)), ((), ())), preferred_element_type=F32)


def _lane_iota(n):
    return lax.broadcasted_iota(jnp.int32, (1, n), 1)


def _in_range(x, lo, hi):
    return (x >= lo) & (x < hi)


def _online_update(s, m, l):
    m_new = jnp.maximum(m, jnp.max(s, axis=-1, keepdims=True))
    alpha = jnp.exp(m - m_new)
    p = jnp.exp(s - m_new)
    return p, alpha, m_new, alpha * l + jnp.sum(p, axis=-1, keepdims=True)


def _layer_norm(z, g, b):
    mu = jnp.mean(z, axis=-1, keepdims=True)
    zc = z - mu
    var = jnp.mean(zc * zc, axis=-1, keepdims=True)
    return zc * lax.rsqrt(var + LN_EPS) * g + b


def _split_hi_lo(x):
    hi = x.astype(MXU_DTYPE)
    lo = (x - hi.astype(F32)).astype(MXU_DTYPE)
    return hi, lo


def _inproj_kernel(x_ref, w_ref, qa_ref, qb_ref, qc_ref, gt_ref, row_ref, win_ref):
    x = x_ref[...].astype(MXU_DTYPE)
    qa_ref[...] = _dot(x, w_ref[:, QA_OFF:QB_OFF]).astype(qa_ref.dtype)
    qb_ref[...] = _dot(x, w_ref[:, QB_OFF:QC_OFF]).astype(qb_ref.dtype)
    qc_ref[...] = _dot(x, w_ref[:, QC_OFF:GT_OFF]).astype(qc_ref.dtype)
    gt_ref[...] = _dot(x, w_ref[:, GT_OFF:ROW_OFF])
    row_ref[...] = _dot(x, w_ref[:, ROW_OFF:WIN_OFF])
    win_ref[...] = _dot(x, w_ref[:, WIN_OFF:IN_W_PACKED])


def _inproj(x2, w_packed):
    n = x2.shape[0]
    tm = min(512, n)
    assert n % tm == 0
    widths = (A_W, B_QK_W, C_W, GT_PAD, KV_ROW, WIN_ROW)
    dtypes = (MXU_DTYPE, MXU_DTYPE, MXU_DTYPE, F32, F32, F32)
    return pl.pallas_call(
        _inproj_kernel,
        grid=(n // tm,),
        in_specs=[pl.BlockSpec((tm, D_MODEL), lambda i: (i, 0)),
                  pl.BlockSpec((D_MODEL, IN_W_PACKED), lambda i: (0, 0))],
        out_specs=[pl.BlockSpec((tm, w), lambda i: (i, 0)) for w in widths],
        out_shape=[jax.ShapeDtypeStruct((n, w), d) for w, d in zip(widths, dtypes)],
        compiler_params=_cparams("arbitrary"),
        name="inproj",
    )(x2, w_packed)


def _outproj_ln_kernel(oa_ref, ob_ref, oc_ref, x_ref, w_ref, g_ref, b_ref, y_ref, *, alpha):
    acc = _dot(oa_ref[...], w_ref[0:A_W, :])
    acc = acc + _dot(ob_ref[...], w_ref[A_W:A_W + B_W, :])
    acc = acc + _dot(oc_ref[...], w_ref[A_W + B_W:MIX_W, :])
    y_ref[...] = _layer_norm(alpha * x_ref[...] + acc, g_ref[...], b_ref[...])


def _outproj_ln(oa, ob, oc, x2, w_out_p, g, b, alpha):
    n = x2.shape[0]
    tm = min(512, n)
    assert n % tm == 0
    row = lambda w: pl.BlockSpec((tm, w), lambda i: (i, 0))
    const = lambda shape: pl.BlockSpec(shape, lambda i: (0, 0))
    return pl.pallas_call(
        functools.partial(_outproj_ln_kernel, alpha=alpha),
        grid=(n // tm,),
        in_specs=[row(A_W), row(B_W), row(C_W), row(D_MODEL), const((MIX_W, D_MODEL)),
                  const((1, D_MODEL)), const((1, D_MODEL))],
        out_specs=row(D_MODEL),
        out_shape=jax.ShapeDtypeStruct((n, D_MODEL), F32),
        compiler_params=_cparams("arbitrary"),
        name="outproj_ln",
    )(oa, ob, oc, x2, w_out_p, g, b)


ROUTER_LANES = LANES


def _route(lg):
    lane = _lane_iota(ROUTER_LANES)
    big = jnp.int32(ROUTER_LANES)
    is_g = lane < N_GROUPS
    lg1 = jnp.where(is_g, lg, -jnp.inf)
    m1 = jnp.max(lg1, axis=-1, keepdims=True)
    grp = jnp.min(jnp.where(lg1 == m1, lane, big), axis=-1, keepdims=True)
    pg = 1.0 / jnp.sum(jnp.where(is_g, jnp.exp(lg1 - m1), 0.0), axis=-1, keepdims=True)
    lo = N_GROUPS + grp * EXPERTS_PER_GROUP
    lg2 = jnp.where((lane >= lo) & (lane < lo + EXPERTS_PER_GROUP), lg, -jnp.inf)
    v1 = jnp.max(lg2, axis=-1, keepdims=True)
    i1 = jnp.min(jnp.where(lg2 == v1, lane, big), axis=-1, keepdims=True)
    lg2b = jnp.where(lane == i1, -jnp.inf, lg2)
    v2 = jnp.max(lg2b, axis=-1, keepdims=True)
    i2 = jnp.min(jnp.where(lg2b == v2, lane, big), axis=-1, keepdims=True)
    e2 = jnp.exp(v2 - v1)
    w1 = pg / (1.0 + e2)
    w2 = pg * e2 / (1.0 + e2)
    return jnp.where(lane == i1, w1, jnp.where(lane == i2, w2, 0.0))


def _moe_kernel(x_ref, wrh_ref, wrl_ref, br_ref, w1_ref, w3_ref, w2_ref, g_ref, b_ref, y_ref,
                xb_ref, comb_ref, acc_ref, *, alpha):
    e = pl.program_id(1)

    @pl.when(e == 0)
    def _():
        xh, xl = _split_hi_lo(x_ref[...])
        xb_ref[...] = xh
        lg = _dot(xh, wrh_ref[...]) + _dot(xl, wrh_ref[...]) + _dot(xh, wrl_ref[...]) + br_ref[...]
        comb_ref[...] = _route(lg)
        acc_ref[...] = jnp.zeros_like(acc_ref)

    xb = xb_ref[...]
    h1 = _dot(xb, w1_ref[0])
    h3 = _dot(xb, w3_ref[0])
    lane = _lane_iota(ROUTER_LANES)
    c = jnp.sum(jnp.where(lane == e + N_GROUPS, comb_ref[...], 0.0), axis=-1, keepdims=True)
    hd = h1 * (1.0 / (1.0 + jnp.exp(-h1))) * h3 * c
    acc_ref[...] += _dot(hd.astype(MXU_DTYPE), w2_ref[0])

    @pl.when(e == pl.num_programs(1) - 1)
    def _():
        y_ref[...] = _layer_norm(alpha * x_ref[...] + acc_ref[...], g_ref[...], b_ref[...])


def _moe_ln(x2, wr_hi, wr_lo, br, w1, w3, w2, g, b, alpha):
    n = x2.shape[0]
    tm = min(1024, n)
    assert n % tm == 0
    const = lambda shape: pl.BlockSpec(shape, lambda i, e: (0,) * len(shape))
    return pl.pallas_call(
        functools.partial(_moe_kernel, alpha=alpha),
        grid=(n // tm, N_EXPERTS),
        in_specs=[pl.BlockSpec((tm, D_MODEL), lambda i, e: (i, 0)),
                  const((D_MODEL, ROUTER_LANES)), const((D_MODEL, ROUTER_LANES)), const((1, ROUTER_LANES)),
                  pl.BlockSpec((1, D_MODEL, EXPERT_FF), lambda i, e: (e, 0, 0)),
                  pl.BlockSpec((1, D_MODEL, EXPERT_FF), lambda i, e: (e, 0, 0)),
                  pl.BlockSpec((1, EXPERT_FF, D_MODEL), lambda i, e: (e, 0, 0)),
                  const((1, D_MODEL)), const((1, D_MODEL))],
        out_specs=pl.BlockSpec((tm, D_MODEL), lambda i, e: (i, 0)),
        out_shape=jax.ShapeDtypeStruct((n, D_MODEL), F32),
        scratch_shapes=[pltpu.VMEM((tm, D_MODEL), MXU_DTYPE), pltpu.VMEM((tm, ROUTER_LANES), F32),
                        pltpu.VMEM((tm, D_MODEL), F32)],
        compiler_params=_cparams("arbitrary", "arbitrary"),
        name="moe_ln",
    )(x2, wr_hi, wr_lo, br, w1, w3, w2, g, b)


def _gelu_tanh(x):
    return 0.5 * x * (1.0 + jnp.tanh(math.sqrt(2.0 / math.pi) * (x + 0.044715 * (x * x * x))))


def _compress_kernel(pt_ref, *refs, n_ops, op_rows):
    del pt_ref
    pages = (refs[:n_ops], refs[n_ops:2 * n_ops])
    pos_ref, wlo_ref, whi_ref, w2_ref, out_ref, carry_ref = refs[2 * n_ops:]
    cpo = op_rows // CMP_STRIDE
    m = n_ops * cpo
    hw = C_GROUPS * CMP_HIDDEN

    @pl.when(pl.program_id(1) == 0)
    def _():
        carry_ref[...] = jnp.zeros_like(carry_ref)

    u = [jnp.zeros((m, hw), F32), jnp.zeros((m, hw), F32)]
    v = [jnp.zeros((m, hw), F32), jnp.zeros((m, hw), F32)]
    for pair in range(CMP_STRIDE // 2):
        for kv in range(2):
            sl = slice(kv * C_KV_W, (kv + 1) * C_KV_W)
            xlo, xhi = [], []
            for t in (2 * pair, 2 * pair + 1):
                xt = jnp.concatenate([p[0, 0, pl.ds(t, cpo, stride=CMP_STRIDE), :] for p in pages[kv]], axis=0)
                xlo.append((xt + pos_ref[t:t + 1, sl]).astype(MXU_DTYPE))
                xhi.append((xt + pos_ref[CMP_STRIDE + t:CMP_STRIDE + t + 1, sl]).astype(MXU_DTYPE))
            u[kv] = u[kv] + _dot(jnp.concatenate(xlo, axis=1), wlo_ref[kv, pair])
            v[kv] = v[kv] + _dot(jnp.concatenate(xhi, axis=1), whi_ref[kv, pair])
    row0 = lax.broadcasted_iota(jnp.int32, (m, 1), 0) == 0
    for kv in range(2):
        prev = jnp.where(row0, carry_ref[0:1, kv * hw:(kv + 1) * hw], pltpu.roll(u[kv], 1, axis=0))
        carry_ref[0:1, kv * hw:(kv + 1) * hw] = u[kv][m - 1:m, :]
        hid = _gelu_tanh(prev + v[kv])
        out_ref[0, :, kv * C_KV_W:(kv + 1) * C_KV_W] = _dot(hid.astype(MXU_DTYPE), w2_ref[kv])


def _compress(cache, layer, page_table, col_block, cw, n_ops):
    pos2, wlo, whi, w2b = cw
    nb, n_pages = page_table.shape
    op_rows = cache.shape[2]
    assert n_pages % n_ops == 0
    steps = n_pages // n_ops
    cpo = op_rows // CMP_STRIDE
    kv_w = 2 * C_KV_W

    def page_map(b, s, pt, *, k, kv):
        return (layer, pt[b, s * n_ops + k], 0, col_block + kv)

    const = lambda shape: pl.BlockSpec(shape, lambda b, s, pt: (0,) * len(shape))
    grid_spec = pltpu.PrefetchScalarGridSpec(
        num_scalar_prefetch=1,
        grid=(nb, steps),
        in_specs=[pl.BlockSpec((1, 1, op_rows, C_KV_W), functools.partial(page_map, k=k, kv=kv))
                  for kv in range(2) for k in range(n_ops)]
        + [const(pos2.shape), const(wlo.shape), const(whi.shape), const(w2b.shape)],
        out_specs=pl.BlockSpec((1, n_ops * cpo, kv_w), lambda b, s, pt: (b, s, 0)),
        scratch_shapes=[pltpu.VMEM((SUBLANES, 2 * C_GROUPS * CMP_HIDDEN), F32)],
    )
    return pl.pallas_call(
        functools.partial(_compress_kernel, n_ops=n_ops, op_rows=op_rows),
        grid_spec=grid_spec,
        out_shape=jax.ShapeDtypeStruct((nb, n_pages * cpo, kv_w), F32),
        compiler_params=_cparams("arbitrary", "arbitrary"),
        name="nsa_compress",
    )(page_table, *([cache] * (2 * n_ops)), pos2, wlo, whi, w2b)


def _attend_t(qs, k_ref, vt_ref, acc_ref, v_rows, lo, qi, far_fn, own_fn):
    ns = qs.shape[0] // T
    w = (ns // len(v_rows)) * T
    acc_ref[...] = jnp.zeros_like(acc_ref)

    def tile(n, m, l, fn):
        off = pl.multiple_of(n * T, T)
        s_all = _dot_nt(k_ref[pl.ds(off, T), :], qs)
        s = jnp.concatenate([fn(c, s_all[:, c * T:(c + 1) * T]) for c in range(ns)], axis=1)
        m_new = jnp.maximum(m, jnp.max(s, axis=0, keepdims=True))
        alpha = jnp.exp(m - m_new)
        p = jnp.exp(s - m_new)
        l = alpha * l + jnp.sum(p, axis=0, keepdims=True)
        pb = p.astype(MXU_DTYPE)
        for gi, r0 in enumerate(v_rows):
            cs = slice(gi * w, (gi + 1) * w)
            acc_ref[gi] = acc_ref[gi] * alpha[:, cs] + _dot(vt_ref[n, r0:r0 + HEAD_DIM, :], pb[:, cs])
        return m_new, l

    init = (jnp.full((1, ns * T), NEG, F32), jnp.zeros((1, ns * T), F32))
    m, l = lax.fori_loop(lo, qi, lambda n, c: tile(n, c[0], c[1], far_fn(n)), init)
    _, l = tile(qi, m, l, own_fn)
    return [acc_ref[gi] / l[:, gi * w:(gi + 1) * w] for gi in range(len(v_rows))]


def _causal_penalty():
    j = lax.broadcasted_iota(jnp.int32, (T, T), 0)
    i = lax.broadcasted_iota(jnp.int32, (T, T), 1)
    return jnp.where(j <= i, 0.0, NEG)


def _cast_tiles(src_ref, col, width, dst_ref, seq):
    for n in range(seq // T):
        dst_ref[n * T:(n + 1) * T, :] = src_ref[0, n * T:(n + 1) * T, col:col + width].astype(dst_ref.dtype)


def _transpose_tiles(src_ref, col, width, dst_ref, seq):
    for n in range(seq // T):
        dst_ref[n] = src_ref[0, n * T:(n + 1) * T, col:col + width].T.astype(dst_ref.dtype)


def _rank_rows(score, n_rows, valid_fn=None):
    blk = lax.broadcasted_iota(jnp.int32, (score.shape[0], 1), 0)
    rank = jnp.zeros(score.shape, F32)
    for mb in range(n_rows):
        row = score[mb:mb + 1, :]
        beats = (row > score) | ((row == score) & (blk > mb))
        rank = rank + jnp.where(beats, 1.0 if valid_fn is None else valid_fn(mb), 0.0)
    return rank


def _moba_prompt_kernel(q_ref, kv_ref, bias_ref, o_ref, kb_ref, vt_ref, mean_ref, pen_ref, acc_ref, *, seq):
    qi = pl.program_id(1)
    nb = seq // MOBA_BLOCK

    @pl.when(qi == 0)
    def _():
        _cast_tiles(kv_ref, 0, A_W, kb_ref, seq)
        _transpose_tiles(kv_ref, A_W, A_W, vt_ref, seq)
        mean_ref[...] = jnp.zeros_like(mean_ref)
        for n in range(nb):
            mean_ref[n:n + 1, :] = jnp.mean(kv_ref[0, n * T:(n + 1) * T, 0:A_W], axis=0, keepdims=True)

    q = q_ref[0]
    lane_q = _lane_iota(A_W)
    qs = jnp.concatenate([jnp.where(_in_range(lane_q, h * HEAD_DIM, (h + 1) * HEAD_DIM), q, jnp.zeros_like(q))
                          for h in range(A_HEADS)], axis=0)
    gate = _dot_nt(mean_ref[...].astype(MXU_DTYPE), qs)
    blk = lax.broadcasted_iota(jnp.int32, (gate.shape[0], 1), 0)
    rank = _rank_rows(gate, nb, lambda mb: jnp.where(qi > mb, 1.0, 0.0))
    pen_ref[...] = jnp.where((blk < qi) & (rank < MOBA_TOPK), 0.0, NEG)
    causal = _causal_penalty()

    def far_fn(n):
        idx = jnp.minimum(qi - n, 2)
        return lambda c, s: s + bias_ref[c, idx] + pen_ref[pl.ds(n, 1), c * T:(c + 1) * T]

    outs = _attend_t(qs, kb_ref, vt_ref, acc_ref, [h * HEAD_DIM for h in range(A_HEADS)], 0, qi, far_fn,
                     lambda c, s: s + bias_ref[c, 0] + causal)
    o_ref[0] = jnp.concatenate(outs, axis=0).T.astype(o_ref.dtype)


def _moba_prompt(qa, row, bias_a):
    b, seq, _ = row.shape
    nb = seq // MOBA_BLOCK
    nbp = -(-nb // SUBLANES) * SUBLANES
    assert seq % T == 0 and T == MOBA_BLOCK and nb >= MOBA_TOPK
    return pl.pallas_call(
        functools.partial(_moba_prompt_kernel, seq=seq),
        grid=(b, seq // T),
        in_specs=[pl.BlockSpec((1, T, A_W), lambda i, j: (i, j, 0)),
                  pl.BlockSpec((1, seq, 2 * A_W), lambda i, j: (i, 0, KA_OFF // (2 * A_W))),
                  pl.BlockSpec(bias_a.shape, lambda i, j: (0, 0, 0, 0))],
        out_specs=pl.BlockSpec((1, T, A_W), lambda i, j: (i, j, 0)),
        out_shape=jax.ShapeDtypeStruct((b, seq, A_W), MXU_DTYPE),
        scratch_shapes=[pltpu.VMEM((seq, A_W), MXU_DTYPE), pltpu.VMEM((seq // T, A_W, T), MXU_DTYPE),
                        pltpu.VMEM((nbp, A_W), F32), pltpu.VMEM((nbp, A_HEADS * T), F32),
                        pltpu.VMEM((A_HEADS, HEAD_DIM, T), F32)],
        compiler_params=_cparams("arbitrary", "arbitrary"),
        name="moba_prompt",
    )(qa, row, bias_a)


def _diff_finish(o0, o1, lam, lane_v):
    o = o0 - lam * o1
    out = jnp.zeros_like(o)
    for h in range(B_HEADS):
        hm = _in_range(lane_v, h * B_V_DIM, (h + 1) * B_V_DIM)
        ms = jnp.sum(jnp.where(hm, o * o, 0.0), axis=-1, keepdims=True) * (1.0 / B_V_DIM)
        out = jnp.where(hm, o * lax.rsqrt(ms + LN_EPS), out)
    return out


def _diff_prompt_kernel(dp_ref, q_ref, kv_ref, bias_ref, subw_ref, o_ref, kb_ref, vt_ref, acc_ref, *, seq):
    qi = pl.program_id(1)

    @pl.when(qi == 0)
    def _():
        _cast_tiles(kv_ref, 0, B_QK_W, kb_ref, seq)
        _transpose_tiles(kv_ref, B_QK_W, B_W, vt_ref, seq)

    q = q_ref[0]
    lane_q = _lane_iota(B_QK_W)
    qs = jnp.concatenate([jnp.where(_in_range(lane_q, c * B_QK_DIM, (c + 1) * B_QK_DIM), q, jnp.zeros_like(q))
                          for c in range(2 * B_HEADS)], axis=0)
    causal = _causal_penalty()

    def far_fn(n):
        idx = jnp.minimum(qi - n, 2)
        return lambda c, s: s + bias_ref[c // 2, idx]

    outs = _attend_t(qs, kb_ref, vt_ref, acc_ref, [h * B_V_DIM for h in range(B_HEADS)], 0, qi, far_fn,
                     lambda c, s: s + bias_ref[c // 2, 0] + causal)
    lam = dp_ref[:, 0:1]
    heads = []
    for h in range(B_HEADS):
        o = outs[h][:, 0:T] - lam * outs[h][:, T:2 * T]
        ms = jnp.mean(o * o, axis=0, keepdims=True)
        heads.append(o * lax.rsqrt(ms + LN_EPS))
    out = jnp.concatenate(heads, axis=0) * subw_ref[...] * dp_ref[:, 1:2]
    o_ref[0] = out.T.astype(o_ref.dtype)


def _diff_prompt(dp, qb, row, bias_b, subw_t):
    b, seq, _ = row.shape
    kvw = B_QK_W + B_W
    assert B_V_DIM == HEAD_DIM
    return pl.pallas_call(
        functools.partial(_diff_prompt_kernel, seq=seq),
        grid=(b, seq // T),
        in_specs=[pl.BlockSpec((1, LANES), lambda i, j: (0, 0)),
                  pl.BlockSpec((1, T, B_QK_W), lambda i, j: (i, j, 0)),
                  pl.BlockSpec((1, seq, kvw), lambda i, j: (i, 0, KB_OFF // kvw)),
                  pl.BlockSpec(bias_b.shape, lambda i, j: (0, 0, 0, 0)),
                  pl.BlockSpec((B_W, T), lambda i, j: (0, 0))],
        out_specs=pl.BlockSpec((1, T, B_W), lambda i, j: (i, j, 0)),
        out_shape=jax.ShapeDtypeStruct((b, seq, B_W), MXU_DTYPE),
        scratch_shapes=[pltpu.VMEM((seq, B_QK_W), MXU_DTYPE), pltpu.VMEM((seq // T, B_W, T), MXU_DTYPE),
                        pltpu.VMEM((B_HEADS, B_V_DIM, 2 * T), F32)],
        compiler_params=_cparams("arbitrary", "arbitrary"),
        name="diff_prompt",
    )(dp, qb, row, bias_b, subw_t)


def _stack_group_queries(q, g):
    gm = _in_range(_lane_iota(C_KV_W), g * HEAD_DIM, (g + 1) * HEAD_DIM)
    q = q.astype(F32)
    return jnp.concatenate(
        [jnp.where(gm, q[:, hg * C_KV_W:(hg + 1) * C_KV_W], 0.0) for hg in range(C_HPG)], axis=0).astype(MXU_DTYPE)


def _stack_gate(gt, g, branch):
    cols = [gt[:, (g * C_HPG + hg) * N_BRANCH + branch:(g * C_HPG + hg) * N_BRANCH + branch + 1] for hg in range(C_HPG)]
    x = jnp.concatenate(cols, axis=0)
    return 1.0 / (1.0 + jnp.exp(-x))


def _cmp_attention(qs, kc, vc, q_pos_rows, n_cmp):
    nch = kc.shape[0]
    s = _dot_nt(qs, kc)
    r = _lane_iota(nch)
    ok = (r >= 1) & (r <= n_cmp) & ((r - 1) * CMP_STRIDE + (CMP_LEN - 1) <= q_pos_rows)
    s = jnp.where(ok, s, NEG)
    p = jnp.where(ok, jnp.exp(s - jnp.max(s, axis=-1, keepdims=True)), 0.0)
    l = jnp.sum(p, axis=-1, keepdims=True)
    p = p / jnp.where(l > 0.0, l, 1.0)
    return p, _dot(p.astype(MXU_DTYPE), vc)


def _nsa_prompt_kernel(q_ref, gt_ref, cmp_ref, kv_ref, win_ref, bias_ref, wsel_ref, o_ref,
                       ks_ref, vst_ref, kw_ref, vwt_ref, kc_ref, vct_ref, pen_ref, acc_ref, *, seq, n_cmp):
    qi = pl.program_id(1)
    nsel = seq // SEL_BLOCK
    bpt = T // SEL_BLOCK

    @pl.when(qi == 0)
    def _():
        _cast_tiles(kv_ref, 0, C_KV_W, ks_ref, seq)
        _transpose_tiles(kv_ref, C_KV_W, C_KV_W, vst_ref, seq)
        _cast_tiles(win_ref, 0, C_KV_W, kw_ref, seq)
        _transpose_tiles(win_ref, C_KV_W, C_KV_W, vwt_ref, seq)
        kc_ref[...] = cmp_ref[0, :, 0:C_KV_W].astype(kc_ref.dtype)
        vct_ref[...] = cmp_ref[0, :, C_KV_W:2 * C_KV_W].T.astype(vct_ref.dtype)

    q = q_ref[0]
    gates = 1.0 / (1.0 + jnp.exp(-gt_ref[0].T))
    q_pos = qi * T + _lane_iota(T)
    q_pos4 = jnp.concatenate([q_pos] * C_HPG, axis=1)
    nch = kc_ref.shape[0]
    r = lax.broadcasted_iota(jnp.int32, (nch, 1), 0)
    cmp_ok = (r >= 1) & (r <= n_cmp) & ((r - 1) * CMP_STRIDE + (CMP_LEN - 1) <= q_pos4)
    blk = lax.broadcasted_iota(jnp.int32, (pen_ref.shape[0], 1), 0)
    own = jnp.right_shift(q_pos, int(math.log2(SEL_BLOCK)))
    causal = _causal_penalty()
    jj = lax.broadcasted_iota(jnp.int32, (T, T), 0)
    ii = lax.broadcasted_iota(jnp.int32, (T, T), 1)
    outs = []
    for g in range(C_GROUPS):
        qs = _stack_group_queries(q, g)
        vrow = [g * HEAD_DIM]
        s = jnp.where(cmp_ok, _dot_nt(kc_ref[...], qs), NEG)
        p = jnp.where(cmp_ok, jnp.exp(s - jnp.max(s, axis=0, keepdims=True)), 0.0)
        l = jnp.sum(p, axis=0, keepdims=True)
        p = p / jnp.where(l > 0.0, l, 1.0)
        o_cmp = _dot(vct_ref[g * HEAD_DIM:(g + 1) * HEAD_DIM, :], p.astype(MXU_DTYPE))
        pg = p[:, 0:T] + p[:, T:2 * T] + p[:, 2 * T:3 * T] + p[:, 3 * T:4 * T]
        ph, plo = _split_hi_lo(pg)
        p_slc = _dot(wsel_ref[...], ph) + _dot(wsel_ref[...], plo)
        forced = (blk == 0) | (blk == own) | (blk == own - 1)
        score = jnp.where(blk <= own, jnp.where(forced, jnp.inf, p_slc), -jnp.inf)
        rank = _rank_rows(score, nsel)
        pen_ref[...] = jnp.where((blk <= own) & (rank < SEL_TOPN), 0.0, NEG)

        def pen_tile(n):
            return jnp.concatenate([jnp.broadcast_to(pen_ref[pl.ds(n * bpt + b, 1), :], (SEL_BLOCK, T))
                                    for b in range(bpt)], axis=0)

        def slc_far(n, g=g):
            idx = jnp.minimum(qi - n, 2)
            pen = pen_tile(n)
            return lambda c, s: s + bias_ref[idx, g * C_HPG + c] + pen

        own_pen = pen_tile(qi) + causal
        o_slc = _attend_t(qs, ks_ref, vst_ref, acc_ref, vrow, 0, qi, slc_far,
                          lambda c, s, g=g: s + bias_ref[0, g * C_HPG + c] + own_pen)[0]

        def win_far(n, g=g):
            d = qi - n
            idx = jnp.minimum(d, 2)
            pen = jnp.where(ii >= jj, jnp.where(d * T >= WINDOW, NEG, 0.0), 0.0)
            return lambda c, s: s + bias_ref[idx, g * C_HPG + c] + pen

        o_win = _attend_t(qs, kw_ref, vwt_ref, acc_ref, vrow, jnp.maximum(qi - WINDOW // T, 0), qi, win_far,
                          lambda c, s, g=g: s + bias_ref[0, g * C_HPG + c] + causal)[0]

        def gate_row(branch):
            return jnp.concatenate([gates[(g * C_HPG + hg) * N_BRANCH + branch:(g * C_HPG + hg) * N_BRANCH + branch + 1, :]
                                    for hg in range(C_HPG)], axis=1)

        outs.append(gate_row(0) * o_cmp + gate_row(1) * o_slc + gate_row(2) * o_win)
    out_t = jnp.concatenate([outs[g][:, hg * T:(hg + 1) * T] for hg in range(C_HPG) for g in range(C_GROUPS)], axis=0)
    o_ref[0] = out_t.T.astype(o_ref.dtype)


def _nsa_prompt(qc, gt, cmp, row, win, bias_c, wsel_t):
    b, seq, _ = row.shape
    nch = cmp.shape[1]
    n_cmp = (seq - CMP_LEN) // CMP_STRIDE + 1
    nselp = wsel_t.shape[0]
    assert seq // SEL_BLOCK >= SEL_TOPN and WINDOW == 2 * T and nselp >= seq // SEL_BLOCK
    kvw = 2 * C_KV_W
    nk = seq // T
    full = lambda a: pl.BlockSpec(a.shape, lambda i, j: (0,) * a.ndim)
    return pl.pallas_call(
        functools.partial(_nsa_prompt_kernel, seq=seq, n_cmp=n_cmp),
        grid=(b, nk),
        in_specs=[pl.BlockSpec((1, T, C_W), lambda i, j: (i, j, 0)),
                  pl.BlockSpec((1, T, GT_PAD), lambda i, j: (i, j, 0)),
                  pl.BlockSpec((1, nch, kvw), lambda i, j: (i, 0, 0)),
                  pl.BlockSpec((1, seq, kvw), lambda i, j: (i, 0, KS_OFF // kvw)),
                  pl.BlockSpec((1, seq, kvw), lambda i, j: (i, 0, 0)),
                  full(bias_c), full(wsel_t)],
        out_specs=pl.BlockSpec((1, T, C_W), lambda i, j: (i, j, 0)),
        out_shape=jax.ShapeDtypeStruct((b, seq, C_W), MXU_DTYPE),
        scratch_shapes=[pltpu.VMEM((seq, C_KV_W), MXU_DTYPE), pltpu.VMEM((nk, C_KV_W, T), MXU_DTYPE),
                        pltpu.VMEM((seq, C_KV_W), MXU_DTYPE), pltpu.VMEM((nk, C_KV_W, T), MXU_DTYPE),
                        pltpu.VMEM((nch, C_KV_W), MXU_DTYPE), pltpu.VMEM((C_KV_W, nch), MXU_DTYPE),
                        pltpu.VMEM((nselp, T), F32), pltpu.VMEM((1, HEAD_DIM, C_HPG * T), F32)],
        compiler_params=_cparams("arbitrary", "arbitrary"),
        name="nsa_prompt",
    )(qc, gt, cmp, row, win, bias_c, wsel_t)


PG = 16
PG_CMP = 16
NEW_PAD = LANES


def _rows_iota(n_rep, n):
    return jnp.concatenate([lax.broadcasted_iota(jnp.int32, (n, 1), 0)] * n_rep, axis=0)


def _pad_rows(x, rows):
    return jnp.concatenate([x, jnp.zeros((rows - x.shape[0], x.shape[1]), x.dtype)], axis=0)


def _page_specs(layer, width, col_block, n_ops):
    def page_map(b, s, pt, *, k):
        return (layer, pt[b, s * n_ops + k], 0, col_block)
    return [pl.BlockSpec((1, 1, PAGE_SIZE, width), functools.partial(page_map, k=k)) for k in range(n_ops)]


def _step_bias_spec(shape, steps):
    return pl.BlockSpec((1,) + shape[1:], lambda b, s, pt: (jnp.where(s == steps - 1, 1, 0),) + (0,) * (len(shape) - 1))


def _new_token_logits(qs, k_new, bias_new, nq):
    s = _dot_nt(qs, _pad_rows(k_new, NEW_PAD).astype(MXU_DTYPE)) + bias_new
    j = _lane_iota(NEW_PAD)
    i = _rows_iota(qs.shape[0] // nq, nq)
    return jnp.where((j < nq) & (j <= i), s, NEG)


def _nsa_sample_pre_kernel(q_ref, cmp_ref, state_ref, winnew_ref, bias_ref, wsel_ref, sel_ref, ocw_ref,
                           *, past, n_cmp, nq):
    q = q_ref[0]
    wb = state_ref.shape[2]
    wk = bias_ref.shape[-1]
    rows = C_HPG * nq
    kc = cmp_ref[0, :, 0:C_KV_W].astype(MXU_DTYPE)
    vc = cmp_ref[0, :, C_KV_W:2 * C_KV_W].astype(MXU_DTYPE)
    pad = jnp.zeros((wk - wb - nq, C_KV_W), F32)
    kw = jnp.concatenate([state_ref[0, 0, :, 0:C_KV_W], winnew_ref[0, :, 0:C_KV_W], pad], axis=0).astype(MXU_DTYPE)
    vw = jnp.concatenate([state_ref[0, 0, :, C_KV_W:], winnew_ref[0, :, C_KV_W:], pad], axis=0).astype(MXU_DTYPE)
    nsl = wsel_ref.shape[1]
    lane = _lane_iota(nsl)
    big = jnp.int32(nsl)
    i_q = lax.broadcasted_iota(jnp.int32, (nq, 1), 0)
    q_pos = past + i_q
    q_pos4 = past + _rows_iota(C_HPG, nq)
    for g in range(C_GROUPS):
        qs = _stack_group_queries(q, g)
        p, o_cmp = _cmp_attention(qs, kc, vc, q_pos4, n_cmp)
        pg = p[0:nq] + p[nq:2 * nq] + p[2 * nq:3 * nq] + p[3 * nq:4 * nq]
        ph, plo = _split_hi_lo(pg)
        p_slc = _dot(ph, wsel_ref[...]) + _dot(plo, wsel_ref[...])
        own = jnp.right_shift(q_pos, int(math.log2(SEL_BLOCK)))
        forced = (lane == 0) | (lane == own) | (lane == own - 1)
        score = jnp.where(lane <= own, jnp.where(forced, jnp.inf, p_slc), -jnp.inf)
        sel = jnp.zeros((nq, nsl), F32)
        for _ in range(SEL_TOPN):
            v = jnp.max(score, axis=-1, keepdims=True)
            idx = jnp.min(jnp.where(score == v, lane, big), axis=-1, keepdims=True)
            hit = lane == idx
            sel = jnp.where(hit & (v > -jnp.inf), 1.0, sel)
            score = jnp.where(hit, -jnp.inf, score)
        sel_ref[0, g * rows:(g + 1) * rows, :] = jnp.concatenate([sel] * C_HPG, axis=0)
        s = _dot_nt(qs, kw) +bias_ref[g * C_HPG:(g + 1) * C_HPG].reshape(rows, wk)
        j = _lane_iota(wk)
        rel = wb + _rows_iota(C_HPG, nq) - j
        ok = (j < wb + nq) & (rel >= 0) & (rel < WINDOW)
        s = jnp.where(ok, s, NEG)
        pw = jnp.exp(s - jnp.max(s, axis=-1, keepdims=True))
        o_win = _dot(pw.astype(MXU_DTYPE), vw) / jnp.sum(pw, axis=-1, keepdims=True)
        ocw_ref[0, g * rows:(g + 1) * rows, 0:C_KV_W] = o_cmp
        ocw_ref[0, g * rows:(g + 1) * rows, C_KV_W:] = o_win


def _nsa_sample_pre(qc, cmp, state_win, layer, win_new, bias_w, wsel, past):
    bs, nq, _ = qc.shape
    nch = cmp.shape[1]
    wb = state_win.shape[2]
    n_cmp = (past + nq - CMP_LEN) // CMP_STRIDE + 1
    rows = C_HEADS * nq
    full = lambda a: pl.BlockSpec(a.shape, lambda b: (0,) * a.ndim)
    return pl.pallas_call(
        functools.partial(_nsa_sample_pre_kernel, past=past, n_cmp=n_cmp, nq=nq),
        grid=(bs,),
        in_specs=[pl.BlockSpec((1, nq, C_W), lambda b: (b, 0, 0)),
                  pl.BlockSpec((1, nch, 2 * C_KV_W), lambda b: (b, 0, 0)),
                  pl.BlockSpec((1, 1, wb, WIN_ROW), lambda b: (layer, b, 0, 0)),
                  pl.BlockSpec((1, nq, WIN_ROW), lambda b: (b, 0, 0)),
                  full(bias_w), full(wsel)],
        out_specs=[pl.BlockSpec((1, rows, wsel.shape[1]), lambda b: (b, 0, 0)),
                   pl.BlockSpec((1, rows, 2 * C_KV_W), lambda b: (b, 0, 0))],
        out_shape=[jax.ShapeDtypeStruct((bs, rows, wsel.shape[1]), F32),
                   jax.ShapeDtypeStruct((bs, rows, 2 * C_KV_W), F32)],
        compiler_params=_cparams("arbitrary"),
        name="nsa_sample_pre",
    )(qc, cmp, state_win, win_new, bias_w, wsel)


def _nsa_sample_slc_kernel(pt_ref, *refs, nq):
    del pt_ref
    pages = refs[:PG]
    (q_ref, gt_ref, selw_ref, exp_ref, bias_ref, rownew_ref, biasnew_ref, ocw_ref,
     o_ref, m_ref, l_ref, acc_ref) = refs[PG:]
    step = pl.program_id(1)
    rows = C_HPG * nq

    @pl.when(step == 0)
    def _():
        m_ref[...] = jnp.full_like(m_ref, NEG)
        l_ref[...] = jnp.zeros_like(l_ref)
        acc_ref[...] = jnp.zeros_like(acc_ref)

    k = jnp.concatenate([p[0, 0, :, 0:C_KV_W] for p in pages], axis=0).astype(MXU_DTYPE)
    v = jnp.concatenate([p[0, 0, :, C_KV_W:] for p in pages], axis=0).astype(MXU_DTYPE)
    q = q_ref[0]
    for g in range(C_GROUPS):
        rs = slice(g * rows, (g + 1) * rows)
        qs = _stack_group_queries(q, g)
        s = _dot_nt(qs, k) +bias_ref[0, g * C_HPG:(g + 1) * C_HPG].reshape(rows, PG * PAGE_SIZE)
        ok = _dot(selw_ref[0, 0, rs, :].astype(MXU_DTYPE), exp_ref[...]) > 0.5
        p, alpha, m, l = _online_update(jnp.where(ok, s, NEG), m_ref[rs], l_ref[rs])
        m_ref[rs] = m
        l_ref[rs] = l
        acc_ref[rs] = acc_ref[rs] * alpha + _dot(p.astype(MXU_DTYPE), v)

    @pl.when(step == pl.num_programs(1) - 1)
    def _():
        gt = gt_ref[0]
        k_new = rownew_ref[0, :, 0:C_KV_W]
        v_new = _pad_rows(rownew_ref[0, :, C_KV_W:], NEW_PAD).astype(MXU_DTYPE)
        outs = []
        for g in range(C_GROUPS):
            rs = slice(g * rows, (g + 1) * rows)
            qs = _stack_group_queries(q, g)
            s = _new_token_logits(qs, k_new, biasnew_ref[g * C_HPG:(g + 1) * C_HPG].reshape(rows, NEW_PAD), nq)
            p, alpha, _, l = _online_update(s, m_ref[rs], l_ref[rs])
            o_slc = (acc_ref[rs] * alpha + _dot(p.astype(MXU_DTYPE), v_new)) / l
            ocw = ocw_ref[0, rs, :]
            outs.append(_stack_gate(gt, g, 0) * ocw[:, 0:C_KV_W] + _stack_gate(gt, g, 1) * o_slc
                        + _stack_gate(gt, g, 2) * ocw[:, C_KV_W:])
        g0 = _lane_iota(C_KV_W) < HEAD_DIM
        for hg in range(C_HPG):
            chunk = jnp.where(g0, outs[0][hg * nq:(hg + 1) * nq], outs[1][hg * nq:(hg + 1) * nq])
            o_ref[0, :, hg * C_KV_W:(hg + 1) * C_KV_W] = chunk.astype(o_ref.dtype)


def _nsa_sample_slc(cache, layer, page_table, qc, gt, selw, expand, bias_steps, row_new, bias_new, ocw):
    bs, nq, _ = qc.shape
    steps = page_table.shape[1] // PG
    rows = C_HEADS * nq
    kvw = 2 * C_KV_W
    per_seq = lambda shape: pl.BlockSpec((1,) + shape, lambda b, s, pt: (b,) + (0,) * len(shape))
    full = lambda a: pl.BlockSpec(a.shape, lambda b, s, pt: (0,) * a.ndim)
    grid_spec = pltpu.PrefetchScalarGridSpec(
        num_scalar_prefetch=1,
        grid=(bs, steps),
        in_specs=_page_specs(layer, kvw, KS_OFF // kvw, PG)
        + [per_seq((nq, C_W)), per_seq((nq, GT_PAD)),
           pl.BlockSpec((1, 1, rows, LANES), lambda b, s, pt: (b, s, 0, 0)),
           full(expand), _step_bias_spec(bias_steps.shape, steps),
           pl.BlockSpec((1, nq, kvw), lambda b, s, pt: (b, 0, KS_OFF // kvw)),
           full(bias_new), per_seq((rows, kvw))],
        out_specs=per_seq((nq, C_W)),
        scratch_shapes=[pltpu.VMEM((rows, 1), F32), pltpu.VMEM((rows, 1), F32), pltpu.VMEM((rows, C_KV_W), F32)],
    )
    return pl.pallas_call(
        functools.partial(_nsa_sample_slc_kernel, nq=nq),
        grid_spec=grid_spec,
        out_shape=jax.ShapeDtypeStruct((bs, nq, C_W), F32),
        compiler_params=_cparams("arbitrary", "arbitrary"),
        name="nsa_sample_slc",
    )(page_table, *([cache] * PG), qc, gt, selw, expand, bias_steps, row_new, bias_new, ocw)


def _diff_sample_kernel(pt_ref, *refs, nq):
    del pt_ref
    pages = refs[:PG]
    dp_ref, q_ref, bias_ref, rownew_ref, biasnew_ref, subw_ref, o_ref, m_ref, l_ref, acc_ref = refs[PG:]
    step = pl.program_id(1)

    @pl.when(step == 0)
    def _():
        m_ref[...] = jnp.full_like(m_ref, NEG)
        l_ref[...] = jnp.zeros_like(l_ref)
        acc_ref[...] = jnp.zeros_like(acc_ref)

    q = q_ref[0]
    lane_q = _lane_iota(B_QK_W)
    qs = jnp.concatenate([jnp.where(_in_range(lane_q, c * B_QK_DIM, (c + 1) * B_QK_DIM), q, 0.0)
                          for c in range(2 * B_HEADS)], axis=0).astype(MXU_DTYPE)

    def head_rows(b):
        return jnp.concatenate([b[c // 2] for c in range(2 * B_HEADS)], axis=0)

    k = jnp.concatenate([p[0, 0, :, 0:B_QK_W] for p in pages], axis=0).astype(MXU_DTYPE)
    v = jnp.concatenate([p[0, 0, :, B_QK_W:] for p in pages], axis=0).astype(MXU_DTYPE)
    s = _dot_nt(qs, k) +head_rows(bias_ref[0])
    p, alpha, m, l = _online_update(s, m_ref[...], l_ref[...])
    m_ref[...] = m
    l_ref[...] = l
    acc_ref[...] = acc_ref[...] * alpha + _dot(p.astype(MXU_DTYPE), v)

    @pl.when(step == pl.num_programs(1) - 1)
    def _():
        k_new = rownew_ref[0, :, 0:B_QK_W]
        v_new = _pad_rows(rownew_ref[0, :, B_QK_W:], NEW_PAD).astype(MXU_DTYPE)
        s = _new_token_logits(qs, k_new, head_rows(biasnew_ref[...]), nq)
        p, alpha, _, l = _online_update(s, m_ref[...], l_ref[...])
        o = (acc_ref[...] * alpha + _dot(p.astype(MXU_DTYPE), v_new)) / l
        lane_v = _lane_iota(B_W)
        maps = [jnp.zeros((nq, B_W), F32), jnp.zeros((nq, B_W), F32)]
        for c in range(2 * B_HEADS):
            hm = _in_range(lane_v, (c // 2) * B_V_DIM, (c // 2 + 1) * B_V_DIM)
            maps[c % 2] = jnp.where(hm, o[c * nq:(c + 1) * nq], maps[c % 2])
        out = _diff_finish(maps[0], maps[1], dp_ref[:, 0:1], lane_v) * subw_ref[...] * dp_ref[:, 1:2]
        o_ref[0] = out.astype(o_ref.dtype)


def _diff_sample(cache, layer, page_table, dp, qb, bias_steps, row_new, bias_new, subw):
    bs, nq, _ = qb.shape
    steps = page_table.shape[1] // PG
    rows = 2 * B_HEADS * nq
    kvw = B_QK_W + B_W
    per_seq = lambda shape: pl.BlockSpec((1,) + shape, lambda b, s, pt: (b,) + (0,) * len(shape))
    full = lambda a: pl.BlockSpec(a.shape, lambda b, s, pt: (0,) * a.ndim)
    grid_spec = pltpu.PrefetchScalarGridSpec(
        num_scalar_prefetch=1,
        grid=(bs, steps),
        in_specs=_page_specs(layer, kvw, KB_OFF // kvw, PG)
        + [full(dp), per_seq((nq, B_QK_W)), _step_bias_spec(bias_steps.shape, steps),
           pl.BlockSpec((1, nq, kvw), lambda b, s, pt: (b, 0, KB_OFF // kvw)), full(bias_new), full(subw)],
        out_specs=per_seq((nq, B_W)),
        scratch_shapes=[pltpu.VMEM((rows, 1), F32), pltpu.VMEM((rows, 1), F32), pltpu.VMEM((rows, B_W), F32)],
    )
    return pl.pallas_call(
        functools.partial(_diff_sample_kernel, nq=nq),
        grid_spec=grid_spec,
        out_shape=jax.ShapeDtypeStruct((bs, nq, B_W), F32),
        compiler_params=_cparams("arbitrary", "arbitrary"),
        name="diff_sample",
    )(page_table, *([cache] * PG), dp, qb, bias_steps, row_new, bias_new, subw)


def _moba_sample_kernel(pt_ref, *refs, nq, n_past_blocks):
    del pt_ref
    pages = refs[:PG]
    q_ref, bias_ref, rownew_ref, biasnew_ref, o_ref, oblk_ref, gate_ref, mst_ref, lst_ref = refs[PG:]
    step = pl.program_id(1)
    rows = A_HEADS * nq
    bps = PG * PAGE_SIZE // MOBA_BLOCK
    nbl = gate_ref.shape[1]
    lane_b = _lane_iota(nbl)
    lane_q = _lane_iota(A_W)
    q = q_ref[0]
    qf = jnp.concatenate([jnp.where(_in_range(lane_q, h * HEAD_DIM, (h + 1) * HEAD_DIM), q, 0.0)
                          for h in range(A_HEADS)], axis=0)
    qs = qf.astype(MXU_DTYPE)

    def head_rows(b):
        return jnp.concatenate([b[h] for h in range(A_HEADS)], axis=0)

    @pl.when(step == 0)
    def _():
        gate_ref[...] = jnp.zeros_like(gate_ref)
        mst_ref[...] = jnp.zeros_like(mst_ref)
        lst_ref[...] = jnp.zeros_like(lst_ref)

    kf = jnp.concatenate([p[0, 0, :, 0:A_W] for p in pages], axis=0)
    k = kf.astype(MXU_DTYPE)
    v = jnp.concatenate([p[0, 0, :, A_W:] for p in pages], axis=0).astype(MXU_DTYPE)
    s = _dot_nt(qs, k) +head_rows(bias_ref[0])
    gate, mst, lst = gate_ref[...], mst_ref[...], lst_ref[...]
    for j in range(bps):
        cs = slice(j * MOBA_BLOCK, (j + 1) * MOBA_BLOCK)
        sj = s[:, cs]
        mj = jnp.max(sj, axis=-1, keepdims=True)
        pj = jnp.exp(sj - mj)
        n = step * bps + j
        oblk_ref[n] = _dot(pj.astype(MXU_DTYPE), v[cs, :])
        mean = jnp.mean(kf[cs, :], axis=0, keepdims=True)
        hit = lane_b == n
        gate = jnp.where(hit, jnp.sum(qf * mean, axis=-1, keepdims=True), gate)
        mst = jnp.where(hit, mj, mst)
        lst = jnp.where(hit, jnp.sum(pj, axis=-1, keepdims=True), lst)
    gate_ref[...] = gate
    mst_ref[...] = mst
    lst_ref[...] = lst

    @pl.when(step == pl.num_programs(1) - 1)
    def _():
        big = jnp.int32(nbl)
        score = jnp.where(lane_b < n_past_blocks, gate, -jnp.inf)
        sel = jnp.zeros((rows, nbl), F32)
        for _ in range(MOBA_TOPK):
            vmax = jnp.max(score, axis=-1, keepdims=True)
            idx = jnp.min(jnp.where(score == vmax, lane_b, big), axis=-1, keepdims=True)
            hit = lane_b == idx
            sel = jnp.where(hit & (vmax > -jnp.inf), 1.0, sel)
            score = jnp.where(hit, -jnp.inf, score)
        k_new = rownew_ref[0, :, 0:A_W]
        v_new = _pad_rows(rownew_ref[0, :, A_W:], NEW_PAD).astype(MXU_DTYPE)
        s_new = _new_token_logits(qs, k_new, head_rows(biasnew_ref[...]), nq)
        m_new = jnp.max(s_new, axis=-1, keepdims=True)
        p_new = jnp.exp(s_new - m_new)
        chosen = sel > 0.5
        m_all = jnp.maximum(m_new, jnp.max(jnp.where(chosen, mst, NEG), axis=-1, keepdims=True))
        w = jnp.where(chosen, jnp.exp(mst - m_all), 0.0)
        w_new = jnp.exp(m_new - m_all)
        l_all = jnp.sum(w * lst, axis=-1, keepdims=True) + w_new * jnp.sum(p_new, axis=-1, keepdims=True)
        o = w_new * _dot(p_new.astype(MXU_DTYPE), v_new)
        for n in range(n_past_blocks):
            o = o + w[:, n:n + 1] * oblk_ref[n]
        o = o / l_all
        out = jnp.zeros((nq, A_W), F32)
        for h in range(A_HEADS):
            out = jnp.where(_in_range(lane_q, h * HEAD_DIM, (h + 1) * HEAD_DIM), o[h * nq:(h + 1) * nq], out)
        o_ref[0] = out.astype(o_ref.dtype)


def _moba_sample(cache, layer, page_table, qa, bias_steps, row_new, bias_new, past):
    bs, nq, _ = qa.shape
    steps = page_table.shape[1] // PG
    rows = A_HEADS * nq
    kvw = 2 * A_W
    n_past_blocks = past // MOBA_BLOCK
    nbl = -(-n_past_blocks // LANES) * LANES
    per_seq = lambda shape: pl.BlockSpec((1,) + shape, lambda b, s, pt: (b,) + (0,) * len(shape))
    full = lambda a: pl.BlockSpec(a.shape, lambda b, s, pt: (0,) * a.ndim)
    grid_spec = pltpu.PrefetchScalarGridSpec(
        num_scalar_prefetch=1,
        grid=(bs, steps),
        in_specs=_page_specs(layer, kvw, KA_OFF // kvw, PG)
        + [per_seq((nq, A_W)), _step_bias_spec(bias_steps.shape, steps),
           pl.BlockSpec((1, nq, kvw), lambda b, s, pt: (b, 0, KA_OFF // kvw)), full(bias_new)],
        out_specs=per_seq((nq, A_W)),
        scratch_shapes=[pltpu.VMEM((n_past_blocks, rows, A_W), F32)] + [pltpu.VMEM((rows, nbl), F32)] * 3,
    )
    return pl.pallas_call(
        functools.partial(_moba_sample_kernel, nq=nq, n_past_blocks=n_past_blocks),
        grid_spec=grid_spec,
        out_shape=jax.ShapeDtypeStruct((bs, nq, A_W), F32),
        compiler_params=_cparams("arbitrary", "arbitrary"),
        name="moba_sample",
    )(page_table, *([cache] * PG), qa, bias_steps, row_new, bias_new)


def _t5_bucket(rel):
    n = jnp.maximum(rel, 0)
    nf = jnp.maximum(n, 1).astype(F32)
    large = MAX_EXACT + (jnp.log(nf / MAX_EXACT) / math.log(T5_MAX_DIST / MAX_EXACT)
                         * (N_BUCKETS - MAX_EXACT)).astype(jnp.int32)
    return jnp.where(n < MAX_EXACT, n, jnp.minimum(large, N_BUCKETS - 1))


def _nsa_perm():
    return np.array([g * C_HPG * HEAD_DIM + hg * HEAD_DIM + d
                     for hg in range(C_HPG) for g in range(C_GROUPS) for d in range(HEAD_DIM)], np.int32)


def _sel_weights(nch, n_cmp, nsl):
    r = np.arange(nch)[:, None]
    s = (r - 1) * CMP_STRIDE
    b0 = np.arange(nsl)[None, :] * SEL_BLOCK
    ov = np.clip(np.minimum(s + CMP_LEN, b0 + SEL_BLOCK) - np.maximum(s, b0), 0, None)
    ov = np.where((r >= 1) & (r <= n_cmp), ov, 0)
    return jnp.asarray(ov.astype(np.float32) / np.float32(CMP_STRIDE), MXU_DTYPE)


def _block_expand(n_tiles, lanes, keys_per_tile):
    key = np.arange(keys_per_tile)[None, None, :] + np.arange(n_tiles)[:, None, None] * keys_per_tile
    blk = np.arange(lanes)[None, :, None]
    return jnp.asarray((key // SEL_BLOCK == blk).astype(np.float32), MXU_DTYPE)


def _compress_weights(pos_k, pos_v, k_w1, k_w2, v_w1, v_w2):
    def first_layer(w1, half):
        w = w1[half * CMP_STRIDE * HEAD_DIM:(half + 1) * CMP_STRIDE * HEAD_DIM].reshape(CMP_STRIDE, HEAD_DIM, CMP_HIDDEN)
        z = jnp.zeros_like(w)
        blk = jnp.concatenate([jnp.concatenate([w, z], axis=2), jnp.concatenate([z, w], axis=2)], axis=1)
        return blk.reshape(CMP_STRIDE // 2, 2 * C_KV_W, C_GROUPS * CMP_HIDDEN)

    def second_layer(w2):
        z = jnp.zeros_like(w2)
        return jnp.concatenate([jnp.concatenate([w2, z], axis=1), jnp.concatenate([z, w2], axis=1)], axis=0)

    pos2 = jnp.concatenate([pos_k, pos_k, pos_v, pos_v], axis=1).astype(F32)
    wlo = jnp.stack([first_layer(k_w1, 0), first_layer(v_w1, 0)]).astype(MXU_DTYPE)
    whi = jnp.stack([first_layer(k_w1, 1), first_layer(v_w1, 1)]).astype(MXU_DTYPE)
    w2b = jnp.stack([second_layer(k_w2), second_layer(v_w2)]).astype(MXU_DTYPE)
    return pos2, wlo, whi, w2b


def kernel(x_prompt, x_sample, cache_kv, state_win, page_table, rel_bias_table, w_in, w_out, lam_q1, lam_k1, lam_q2, lam_k2, diff_subln_w, cmp_pos_k, cmp_pos_v, cmp_k_w1, cmp_k_w2, cmp_v_w1, cmp_v_w2, ln_mix_g, ln_mix_b, ln_ffn_g, ln_ffn_b, router_w1, router_b1, router_w2, router_b2, expert_w1, expert_w3, expert_w2):
    depth = w_in.shape[0]
    bp, seq, _ = x_prompt.shape
    bs, nq, _ = x_sample.shape
    n_pages = page_table.shape[1]
    past = n_pages * PAGE_SIZE
    wb = state_win.shape[2]
    alpha = (2.0 * depth) ** 0.25
    assert cache_kv.shape[2] == PAGE_SIZE and nq < CMP_STRIDE and nq % SUBLANES == 0
    assert past % MOBA_BLOCK == 0 and n_pages % PG == 0 and PG * PAGE_SIZE // SEL_BLOCK <= LANES
    assert wb == WINDOW and wb <= past

    perm = _nsa_perm()
    tab = rel_bias_table.astype(F32)
    bvec = tab[:, _t5_bucket(jnp.arange(BIAS_RANGE, dtype=jnp.int32))]

    def toeplitz(rel):
        return bvec[:, np.clip(rel, 0, BIAS_RANGE - 1)]

    ti = np.arange(T)[None, :] - np.arange(T)[:, None]
    bias_tiles = jnp.stack([toeplitz(ti + d * T) for d in range(3)], axis=1)
    ha, hb = A_HEADS, A_HEADS + B_HEADS
    bias_a_p, bias_b_p = bias_tiles[:ha], bias_tiles[ha:hb]
    bias_c_p = jnp.swapaxes(bias_tiles[hb:], 0, 1)
    qi = np.arange(nq)[:, None]
    far = jnp.broadcast_to(bvec[:, BIAS_RANGE - 1][:, None, None], (bvec.shape[0], nq, (PG - 1) * PAGE_SIZE))
    last_page = toeplitz(PAGE_SIZE + qi - np.arange(PAGE_SIZE)[None, :])
    bias_steps = jnp.stack([jnp.broadcast_to(bvec[:, BIAS_RANGE - 1][:, None, None], (bvec.shape[0], nq, PG * PAGE_SIZE)),
                            jnp.concatenate([far, last_page], axis=2)])
    bias_new = toeplitz(qi - np.arange(NEW_PAD)[None, :])
    wk = -(-(wb + nq) // LANES) * LANES
    bias_win = toeplitz(wb + qi - np.arange(wk)[None, :])[hb:]

    nch_p = seq // CMP_STRIDE
    nselp = -(-(seq // SEL_BLOCK) // SUBLANES) * SUBLANES
    wsel_p = _sel_weights(nch_p, (seq - CMP_LEN) // CMP_STRIDE + 1, nselp).T
    nch_s = past // CMP_STRIDE
    nsel_s = past // SEL_BLOCK + 1
    steps = n_pages // PG
    bps = PG * PAGE_SIZE // SEL_BLOCK
    nsl = -(-max(nsel_s, steps * bps) // LANES) * LANES
    wsel_s = _sel_weights(nch_s, (past + nq - CMP_LEN) // CMP_STRIDE + 1, nsl)
    expand_s = _block_expand(1, LANES, PG * PAGE_SIZE)[0]
    pt_prompt = jnp.arange(bp, dtype=jnp.int32)[:, None]
    pg_cmp = math.gcd(PG_CMP, n_pages)

    xp = x_prompt.reshape(bp * seq, D_MODEL)
    xs = x_sample.reshape(bs * nq, D_MODEL)
    kv_p, win_p, kv_s, win_s = [], [], [], []
    for l in range(depth):
        w = w_in[l]
        w_packed = jnp.concatenate(
            [w[:, :A_W] * HEAD_DIM ** -0.5, w[:, A_W:A_W + B_QK_W] * B_QK_DIM ** -0.5,
             w[:, A_W + B_QK_W:A_W + B_QK_W + C_W][:, perm] * HEAD_DIM ** -0.5,
             jnp.pad(w[:, Q_W - GATE_W:Q_W], ((0, 0), (0, GT_PAD - GATE_W))), w[:, Q_W:]], axis=1).astype(MXU_DTYPE)
        wo = w_out[l]
        w_out_p = jnp.concatenate([wo[:A_W + B_W], wo[A_W + B_W:][perm]], axis=0).astype(MXU_DTYPE)
        lam_init = 0.8 - 0.6 * math.exp(-0.3 * l)
        lam = (jnp.exp(jnp.sum(lam_q1[l].astype(F32) * lam_k1[l].astype(F32)))
               - jnp.exp(jnp.sum(lam_q2[l].astype(F32) * lam_k2[l].astype(F32))) + lam_init)
        dp = jnp.zeros((1, LANES), F32).at[0, 0].set(lam).at[0, 1].set(1.0 - lam_init)
        subw = jnp.tile(diff_subln_w[l].astype(F32), B_HEADS)[None, :]
        cw = _compress_weights(cmp_pos_k[l], cmp_pos_v[l], cmp_k_w1[l], cmp_k_w2[l], cmp_v_w1[l], cmp_v_w2[l])
        wr = jnp.concatenate([router_w1[l], jnp.moveaxis(router_w2[l], 0, 1).reshape(D_MODEL, N_EXPERTS)], axis=1)
        wr = jnp.pad(wr.astype(F32), ((0, 0), (0, ROUTER_LANES - wr.shape[1])))
        wr_hi = wr.astype(MXU_DTYPE)
        wr_lo = (wr - wr_hi.astype(F32)).astype(MXU_DTYPE)
        br = jnp.concatenate([router_b1[l], router_b2[l].reshape(-1)]).astype(F32)
        br = jnp.pad(br, (0, ROUTER_LANES - br.shape[0]))[None, :]
        ew1, ew3, ew2 = expert_w1[l].astype(MXU_DTYPE), expert_w3[l].astype(MXU_DTYPE), expert_w2[l].astype(MXU_DTYPE)
        g_mix, b_mix = ln_mix_g[l].astype(F32)[None, :], ln_mix_b[l].astype(F32)[None, :]
        g_ffn, b_ffn = ln_ffn_g[l].astype(F32)[None, :], ln_ffn_b[l].astype(F32)[None, :]

        qa, qb, qc, gt, row, win = _inproj(xp, w_packed)
        r3 = lambda a: a.reshape(bp, seq, a.shape[-1])
        row3, win3 = r3(row), r3(win)
        cmp_p = _compress(row3[None], 0, pt_prompt, KC_OFF // C_KV_W, cw, 1)
        oa = _moba_prompt(r3(qa), row3, bias_a_p)
        ob = _diff_prompt(dp, r3(qb), row3, bias_b_p, jnp.broadcast_to(subw.T, (B_W, T)))
        oc = _nsa_prompt(r3(qc), r3(gt), cmp_p, row3, win3, bias_c_p, wsel_p)
        f2 = lambda a: a.reshape(bp * seq, a.shape[-1])
        xp = _outproj_ln(f2(oa), f2(ob), f2(oc), xp, w_out_p, g_mix, b_mix, alpha)
        xp = _moe_ln(xp, wr_hi, wr_lo, br, ew1, ew3, ew2, g_ffn, b_ffn, alpha)
        kv_p.append(row3)
        win_p.append(win3[:, seq - min(WINDOW, seq):])

        qa, qb, qc, gt, row, win = _inproj(xs, w_packed)
        s3 = lambda a: a.reshape(bs, nq, a.shape[-1]).astype(F32)
        row3, win3 = s3(row), s3(win)
        cmp_s = _compress(cache_kv, l, page_table, KC_OFF // C_KV_W, cw, pg_cmp)
        sel, ocw = _nsa_sample_pre(s3(qc), cmp_s, state_win, l, win3, bias_win, wsel_s, past)
        selw = sel[:, :, :steps * bps].reshape(bs, C_HEADS * nq, steps, bps)
        selw = jnp.pad(jnp.moveaxis(selw, 2, 1), ((0, 0), (0, 0), (0, 0), (0, LANES - bps)))
        oc = _nsa_sample_slc(cache_kv, l, page_table, s3(qc), s3(gt), selw, expand_s, bias_steps[:, hb:], row3,
                             bias_new[hb:], ocw)
        ob = _diff_sample(cache_kv, l, page_table, dp, s3(qb), bias_steps[:, ha:hb], row3, bias_new[ha:hb], subw)
        oa = _moba_sample(cache_kv, l, page_table, s3(qa), bias_steps[:, :ha], row3, bias_new[:ha], past)
        f2 = lambda a: a.reshape(bs * nq, a.shape[-1]).astype(MXU_DTYPE)
        xs = _outproj_ln(f2(oa), f2(ob), f2(oc), xs, w_out_p, g_mix, b_mix, alpha)
        xs = _moe_ln(xs, wr_hi, wr_lo, br, ew1, ew3, ew2, g_ffn, b_ffn, alpha)
        kv_s.append(row3)
        win_s.append(jnp.concatenate([state_win[l], win3.astype(state_win.dtype)], axis=1)[:, nq:])

    return (xp.reshape(bp, seq, D_MODEL), xs.reshape(bs, nq, D_MODEL),
            jnp.stack(kv_p), jnp.stack(win_p), jnp.stack(kv_s), jnp.stack(win_s))
```

```python
import functools
import math

import numpy as np
import jax
import jax.numpy as jnp
from jax import lax
from jax.experimental import pallas as pl
from jax.experimental.pallas import tpu as pltpu

D_MODEL = 1024
PAGE_SIZE = 128
HEAD_DIM = 64
A_HEADS = 4
MOBA_BLOCK = 256
MOBA_TOPK = 3
B_HEADS = 4
B_QK_DIM = 32
B_V_DIM = 2 * B_QK_DIM
C_HEADS = 8
C_GROUPS = 2
C_HPG = C_HEADS // C_GROUPS
CMP_LEN = 32
CMP_STRIDE = 16
CMP_HIDDEN = 256
SEL_BLOCK = 64
SEL_TOPN = 16
WINDOW = 512
N_BRANCH = 3
N_BUCKETS = 32
MAX_EXACT = N_BUCKETS // 2
T5_MAX_DIST = 128
A_W = A_HEADS * HEAD_DIM
B_QK_W = B_HEADS * 2 * B_QK_DIM
B_W = B_HEADS * B_V_DIM
C_W = C_HEADS * HEAD_DIM
C_KV_W = C_GROUPS * HEAD_DIM
MIX_W = A_W + B_W + C_W
GATE_W = C_HEADS * N_BRANCH
Q_W = A_W + B_QK_W + C_W + GATE_W
KV_ROW = 2 * A_W + B_QK_W + B_W + 4 * C_KV_W
WIN_ROW = 2 * C_KV_W
N_GROUPS = 4
EXPERTS_PER_GROUP = 8
N_EXPERTS = N_GROUPS * EXPERTS_PER_GROUP
EXPERT_FF = 256
LN_EPS = 1e-5
NEG = -1e30
F32 = jnp.float32
MXU_DTYPE = jnp.bfloat16

LANES = 128
SUBLANES = 8
VMEM_LIMIT_BYTES = 56 * 1024 * 1024

T = 256
GT_PAD = LANES
QA_OFF, QB_OFF, QC_OFF, GT_OFF = 0, A_W, A_W + B_QK_W, A_W + B_QK_W + C_W
ROW_OFF = GT_OFF + GT_PAD
WIN_OFF = ROW_OFF + KV_ROW
IN_W_PACKED = WIN_OFF + WIN_ROW
KA_OFF, VA_OFF, KB_OFF, VB_OFF = 0, A_W, 2 * A_W, 2 * A_W + B_QK_W
KC_OFF = 2 * A_W + B_QK_W + B_W
KS_OFF = KC_OFF + 2 * C_KV_W


def _cparams(*sem):
    return pltpu.CompilerParams(dimension_semantics=sem, vmem_limit_bytes=VMEM_LIMIT_BYTES)


def _dot(a, b):
    return jnp.dot(a, b, preferred_element_type=F32)


def _dot_nt(a, b):
    return lax.dot_general(a, b, (((1,), (1,)), ((), ())), preferred_element_type=F32)


def _lane_iota(n):
    return lax.broadcasted_iota(jnp.int32, (1, n), 1)


def _in_range(x, lo, hi):
    return (x >= lo) & (x < hi)


def _online_update(s, m, l):
    m_new = jnp.maximum(m, jnp.max(s, axis=-1, keepdims=True))
    alpha = jnp.exp(m - m_new)
    p = jnp.exp(s - m_new)
    return p, alpha, m_new, alpha * l + jnp.sum(p, axis=-1, keepdims=True)


def _layer_norm(z, g, b):
    mu = jnp.mean(z, axis=-1, keepdims=True)
    zc = z - mu
    var = jnp.mean(zc * zc, axis=-1, keepdims=True)
    return zc * lax.rsqrt(var + LN_EPS) * g + b


def _split_hi_lo(x):
    hi = x.astype(MXU_DTYPE)
    lo = (x - hi.astype(F32)).astype(MXU_DTYPE)
    return hi, lo


def _inproj_kernel(x_ref, w_ref, qa_ref, qb_ref, qc_ref, gt_ref, row_ref, win_ref):
    x = x_ref[...].astype(MXU_DTYPE)
    qa_ref[...] = _dot(x, w_ref[:, QA_OFF:QB_OFF]).astype(qa_ref.dtype)
    qb_ref[...] = _dot(x, w_ref[:, QB_OFF:QC_OFF]).astype(qb_ref.dtype)
    qc_ref[...] = _dot(x, w_ref[:, QC_OFF:GT_OFF]).astype(qc_ref.dtype)
    gt_ref[...] = _dot(x, w_ref[:, GT_OFF:ROW_OFF])
    row_ref[...] = _dot(x, w_ref[:, ROW_OFF:WIN_OFF])
    win_ref[...] = _dot(x, w_ref[:, WIN_OFF:IN_W_PACKED])


def _inproj(x2, w_packed):
    n = x2.shape[0]
    tm = min(512, n)
    assert n % tm == 0
    widths = (A_W, B_QK_W, C_W, GT_PAD, KV_ROW, WIN_ROW)
    dtypes = (MXU_DTYPE, MXU_DTYPE, MXU_DTYPE, F32, F32, F32)
    return pl.pallas_call(
        _inproj_kernel,
        grid=(n // tm,),
        in_specs=[pl.BlockSpec((tm, D_MODEL), lambda i: (i, 0)),
                  pl.BlockSpec((D_MODEL, IN_W_PACKED), lambda i: (0, 0))],
        out_specs=[pl.BlockSpec((tm, w), lambda i: (i, 0)) for w in widths],
        out_shape=[jax.ShapeDtypeStruct((n, w), d) for w, d in zip(widths, dtypes)],
        compiler_params=_cparams("arbitrary"),
        name="inproj",
    )(x2, w_packed)


def _outproj_ln_kernel(oa_ref, ob_ref, oc_ref, x_ref, w_ref, g_ref, b_ref, y_ref, *, alpha):
    acc = _dot(oa_ref[...], w_ref[0:A_W, :])
    acc = acc + _dot(ob_ref[...], w_ref[A_W:A_W + B_W, :])
    acc = acc + _dot(oc_ref[...], w_ref[A_W + B_W:MIX_W, :])
    y_ref[...] = _layer_norm(alpha * x_ref[...] + acc, g_ref[...], b_ref[...])


def _outproj_ln(oa, ob, oc, x2, w_out_p, g, b, alpha):
    n = x2.shape[0]
    tm = min(512, n)
    assert n % tm == 0
    row = lambda w: pl.BlockSpec((tm, w), lambda i: (i, 0))
    const = lambda shape: pl.BlockSpec(shape, lambda i: (0, 0))
    return pl.pallas_call(
        functools.partial(_outproj_ln_kernel, alpha=alpha),
        grid=(n // tm,),
        in_specs=[row(A_W), row(B_W), row(C_W), row(D_MODEL), const((MIX_W, D_MODEL)),
                  const((1, D_MODEL)), const((1, D_MODEL))],
        out_specs=row(D_MODEL),
        out_shape=jax.ShapeDtypeStruct((n, D_MODEL), F32),
        compiler_params=_cparams("arbitrary"),
        name="outproj_ln",
    )(oa, ob, oc, x2, w_out_p, g, b)


ROUTER_LANES = LANES


def _route(lg):
    lane = _lane_iota(ROUTER_LANES)
    big = jnp.int32(ROUTER_LANES)
    is_g = lane < N_GROUPS
    lg1 = jnp.where(is_g, lg, -jnp.inf)
    m1 = jnp.max(lg1, axis=-1, keepdims=True)
    grp = jnp.min(jnp.where(lg1 == m1, lane, big), axis=-1, keepdims=True)
    pg = 1.0 / jnp.sum(jnp.where(is_g, jnp.exp(lg1 - m1), 0.0), axis=-1, keepdims=True)
    lo = N_GROUPS + grp * EXPERTS_PER_GROUP
    lg2 = jnp.where((lane >= lo) & (lane < lo + EXPERTS_PER_GROUP), lg, -jnp.inf)
    v1 = jnp.max(lg2, axis=-1, keepdims=True)
    i1 = jnp.min(jnp.where(lg2 == v1, lane, big), axis=-1, keepdims=True)
    lg2b = jnp.where(lane == i1, -jnp.inf, lg2)
    v2 = jnp.max(lg2b, axis=-1, keepdims=True)
    i2 = jnp.min(jnp.where(lg2b == v2, lane, big), axis=-1, keepdims=True)
    e2 = jnp.exp(v2 - v1)
    w1 = pg / (1.0 + e2)
    w2 = pg * e2 / (1.0 + e2)
    return jnp.where(lane == i1, w1, jnp.where(lane == i2, w2, 0.0))


def _moe_kernel(x_ref, wrh_ref, wrl_ref, br_ref, w1_ref, w3_ref, w2_ref, g_ref, b_ref, y_ref,
                xb_ref, comb_ref, acc_ref, *, alpha):
    e = pl.program_id(1)

    @pl.when(e == 0)
    def _():
        xh, xl = _split_hi_lo(x_ref[...])
        xb_ref[...] = xh
        lg = _dot(xh, wrh_ref[...]) + _dot(xl, wrh_ref[...]) + _dot(xh, wrl_ref[...]) + br_ref[...]
        comb_ref[...] = _route(lg)
        acc_ref[...] = jnp.zeros_like(acc_ref)

    xb = xb_ref[...]
    h1 = _dot(xb, w1_ref[0])
    h3 = _dot(xb, w3_ref[0])
    lane = _lane_iota(ROUTER_LANES)
    c = jnp.sum(jnp.where(lane == e + N_GROUPS, comb_ref[...], 0.0), axis=-1, keepdims=True)
    hd = h1 * (1.0 / (1.0 + jnp.exp(-h1))) * h3 * c
    acc_ref[...] += _dot(hd.astype(MXU_DTYPE), w2_ref[0])

    @pl.when(e == pl.num_programs(1) - 1)
    def _():
        y_ref[...] = _layer_norm(alpha * x_ref[...] + acc_ref[...], g_ref[...], b_ref[...])


def _moe_ln(x2, wr_hi, wr_lo, br, w1, w3, w2, g, b, alpha):
    n = x2.shape[0]
    tm = min(1024, n)
    assert n % tm == 0
    const = lambda shape: pl.BlockSpec(shape, lambda i, e: (0,) * len(shape))
    return pl.pallas_call(
        functools.partial(_moe_kernel, alpha=alpha),
        grid=(n // tm, N_EXPERTS),
        in_specs=[pl.BlockSpec((tm, D_MODEL), lambda i, e: (i, 0)),
                  const((D_MODEL, ROUTER_LANES)), const((D_MODEL, ROUTER_LANES)), const((1, ROUTER_LANES)),
                  pl.BlockSpec((1, D_MODEL, EXPERT_FF), lambda i, e: (e, 0, 0)),
                  pl.BlockSpec((1, D_MODEL, EXPERT_FF), lambda i, e: (e, 0, 0)),
                  pl.BlockSpec((1, EXPERT_FF, D_MODEL), lambda i, e: (e, 0, 0)),
                  const((1, D_MODEL)), const((1, D_MODEL))],
        out_specs=pl.BlockSpec((tm, D_MODEL), lambda i, e: (i, 0)),
        out_shape=jax.ShapeDtypeStruct((n, D_MODEL), F32),
        scratch_shapes=[pltpu.VMEM((tm, D_MODEL), MXU_DTYPE), pltpu.VMEM((tm, ROUTER_LANES), F32),
                        pltpu.VMEM((tm, D_MODEL), F32)],
        compiler_params=_cparams("arbitrary", "arbitrary"),
        name="moe_ln",
    )(x2, wr_hi, wr_lo, br, w1, w3, w2, g, b)


def _gelu_tanh(x):
    return 0.5 * x * (1.0 + jnp.tanh(math.sqrt(2.0 / math.pi) * (x + 0.044715 * (x * x * x))))


def _compress_kernel(pt_ref, *refs, n_ops, op_rows):
    del pt_ref
    pages = (refs[:n_ops], refs[n_ops:2 * n_ops])
    pos_ref, wlo_ref, whi_ref, w2_ref, out_ref, carry_ref = refs[2 * n_ops:]
    cpo = op_rows // CMP_STRIDE
    m = n_ops * cpo

    @pl.when(pl.program_id(1) == 0)
    def _():
        carry_ref[...] = jnp.zeros_like(carry_ref)

    first = _lane_iota(C_KV_W) < HEAD_DIM
    row0 = lax.broadcasted_iota(jnp.int32, (m, 1), 0) == 0
    for kv in range(2):
        u = jnp.zeros((C_GROUPS * m, CMP_HIDDEN), F32)
        v = jnp.zeros((C_GROUPS * m, CMP_HIDDEN), F32)
        for quad in range(CMP_STRIDE // 4):
            lo, hi = [], []
            for pair in (2 * quad, 2 * quad + 1):
                x0, x1 = [jnp.concatenate([p[0, 0, pl.ds(t, cpo, stride=CMP_STRIDE), :] for p in pages[kv]], axis=0)
                          for t in (2 * pair, 2 * pair + 1)]
                r0, r1 = pltpu.roll(x0, HEAD_DIM, axis=1), pltpu.roll(x1, HEAD_DIM, axis=1)
                ab = jnp.concatenate([jnp.where(first, x0, r1), jnp.where(first, r0, x1)], axis=0)
                lo.append((ab + pos_ref[kv, 0, pair:pair + 1, :]).astype(MXU_DTYPE))
                hi.append((ab + pos_ref[kv, 1, pair:pair + 1, :]).astype(MXU_DTYPE))
            u = u + _dot(jnp.concatenate(lo, axis=1), wlo_ref[kv, quad])
            v = v + _dot(jnp.concatenate(hi, axis=1), whi_ref[kv, quad])
        out = jnp.zeros((m, C_KV_W), F32)
        for g in range(C_GROUPS):
            ug, vg = u[g * m:(g + 1) * m], v[g * m:(g + 1) * m]
            cs = slice((kv * C_GROUPS + g) * CMP_HIDDEN, (kv * C_GROUPS + g + 1) * CMP_HIDDEN)
            prev = jnp.where(row0, carry_ref[0:1, cs], pltpu.roll(ug, 1, axis=0))
            carry_ref[0:1, cs] = ug[m - 1:m, :]
            out = out + _dot(_gelu_tanh(prev + vg).astype(MXU_DTYPE), w2_ref[kv, g])
        out_ref[0, :, kv * C_KV_W:(kv + 1) * C_KV_W] = out


def _compress(cache, layer, page_table, col_block, cw, n_ops):
    pos2, wlo, whi, w2b = cw
    nb, n_pages = page_table.shape
    op_rows = cache.shape[2]
    assert n_pages % n_ops == 0
    steps = n_pages // n_ops
    cpo = op_rows // CMP_STRIDE
    kv_w = 2 * C_KV_W

    def page_map(b, s, pt, *, k, kv):
        return (layer, pt[b, s * n_ops + k], 0, col_block + kv)

    const = lambda shape: pl.BlockSpec(shape, lambda b, s, pt: (0,) * len(shape))
    grid_spec = pltpu.PrefetchScalarGridSpec(
        num_scalar_prefetch=1,
        grid=(nb, steps),
        in_specs=[pl.BlockSpec((1, 1, op_rows, C_KV_W), functools.partial(page_map, k=k, kv=kv))
                  for kv in range(2) for k in range(n_ops)]
        + [const(pos2.shape), const(wlo.shape), const(whi.shape), const(w2b.shape)],
        out_specs=pl.BlockSpec((1, n_ops * cpo, kv_w), lambda b, s, pt: (b, s, 0)),
        scratch_shapes=[pltpu.VMEM((SUBLANES, 2 * C_GROUPS * CMP_HIDDEN), F32)],
    )
    return pl.pallas_call(
        functools.partial(_compress_kernel, n_ops=n_ops, op_rows=op_rows),
        grid_spec=grid_spec,
        out_shape=jax.ShapeDtypeStruct((nb, n_pages * cpo, kv_w), F32),
        compiler_params=_cparams("arbitrary", "arbitrary"),
        name="nsa_compress",
    )(page_table, *([cache] * (2 * n_ops)), pos2, wlo, whi, w2b)


def _attend_t(qs, k_ref, vt_ref, acc_ref, v_rows, lo, qi, far_fn, own_fn):
    ns = qs.shape[0] // T
    w = (ns // len(v_rows)) * T
    acc_ref[...] = jnp.zeros_like(acc_ref)

    def tile(n, m, l, fn):
        off = pl.multiple_of(n * T, T)
        s_all = _dot_nt(k_ref[pl.ds(off, T), :], qs)
        s = jnp.concatenate([fn(c, s_all[:, c * T:(c + 1) * T]) for c in range(ns)], axis=1)
        m_new = jnp.maximum(m, jnp.max(s, axis=0, keepdims=True))
        alpha = jnp.exp(m - m_new)
        p = jnp.exp(s - m_new)
        l = alpha * l + jnp.sum(p, axis=0, keepdims=True)
        pb = p.astype(MXU_DTYPE)
        for gi, r0 in enumerate(v_rows):
            cs = slice(gi * w, (gi + 1) * w)
            acc_ref[gi] = acc_ref[gi] * alpha[:, cs] + _dot(vt_ref[n, r0:r0 + HEAD_DIM, :], pb[:, cs])
        return m_new, l

    init = (jnp.full((1, ns * T), NEG, F32), jnp.zeros((1, ns * T), F32))
    m, l = lax.fori_loop(lo, qi, lambda n, c: tile(n, c[0], c[1], far_fn(n)), init)
    _, l = tile(qi, m, l, own_fn)
    return [acc_ref[gi] / l[:, gi * w:(gi + 1) * w] for gi in range(len(v_rows))]


def _causal_penalty():
    j = lax.broadcasted_iota(jnp.int32, (T, T), 0)
    i = lax.broadcasted_iota(jnp.int32, (T, T), 1)
    return jnp.where(j <= i, 0.0, NEG)


def _cast_tiles(src_ref, col, width, dst_ref, seq):
    for n in range(seq // T):
        dst_ref[n * T:(n + 1) * T, :] = src_ref[0, n * T:(n + 1) * T, col:col + width].astype(dst_ref.dtype)


def _transpose_tiles(src_ref, col, width, dst_ref, seq):
    for n in range(seq // T):
        dst_ref[n] = src_ref[0, n * T:(n + 1) * T, col:col + width].T.astype(dst_ref.dtype)


def _rank_rows(score, n_rows, valid_fn=None):
    blk = lax.broadcasted_iota(jnp.int32, (score.shape[0], 1), 0)
    rank = jnp.zeros(score.shape, F32)
    for mb in range(n_rows):
        row = score[mb:mb + 1, :]
        beats = (row > score) | ((row == score) & (blk > mb))
        rank = rank + jnp.where(beats, 1.0 if valid_fn is None else valid_fn(mb), 0.0)
    return rank


def _moba_prompt_kernel(q_ref, kv_ref, bias_ref, o_ref, kb_ref, vt_ref, mean_ref, pen_ref, acc_ref, *, seq):
    qi = pl.program_id(1)
    nb = seq // MOBA_BLOCK

    @pl.when(qi == 0)
    def _():
        _cast_tiles(kv_ref, 0, A_W, kb_ref, seq)
        _transpose_tiles(kv_ref, A_W, A_W, vt_ref, seq)
        mean_ref[...] = jnp.zeros_like(mean_ref)
        for n in range(nb):
            mean_ref[n:n + 1, :] = jnp.mean(kv_ref[0, n * T:(n + 1) * T, 0:A_W], axis=0, keepdims=True)

    q = q_ref[0]
    lane_q = _lane_iota(A_W)
    qs = jnp.concatenate([jnp.where(_in_range(lane_q, h * HEAD_DIM, (h + 1) * HEAD_DIM), q, jnp.zeros_like(q))
                          for h in range(A_HEADS)], axis=0)
    gate = _dot_nt(mean_ref[...].astype(MXU_DTYPE), qs)
    blk = lax.broadcasted_iota(jnp.int32, (gate.shape[0], 1), 0)
    rank = _rank_rows(gate, nb, lambda mb: jnp.where(qi > mb, 1.0, 0.0))
    pen_ref[...] = jnp.where((blk < qi) & (rank < MOBA_TOPK), 0.0, NEG)
    causal = _causal_penalty()

    def far_fn(n):
        idx = jnp.minimum(qi - n, 2)
        return lambda c, s: s + bias_ref[c, idx] + pen_ref[pl.ds(n, 1), c * T:(c + 1) * T]

    outs = _attend_t(qs, kb_ref, vt_ref, acc_ref, [h * HEAD_DIM for h in range(A_HEADS)], 0, qi, far_fn,
                     lambda c, s: s + bias_ref[c, 0] + causal)
    o_ref[0] = jnp.concatenate(outs, axis=0).T.astype(o_ref.dtype)


def _moba_prompt(qa, row, bias_a):
    b, seq, _ = row.shape
    nb = seq // MOBA_BLOCK
    nbp = -(-nb // SUBLANES) * SUBLANES
    assert seq % T == 0 and T == MOBA_BLOCK and nb >= MOBA_TOPK
    return pl.pallas_call(
        functools.partial(_moba_prompt_kernel, seq=seq),
        grid=(b, seq // T),
        in_specs=[pl.BlockSpec((1, T, A_W), lambda i, j: (i, j, 0)),
                  pl.BlockSpec((1, seq, 2 * A_W), lambda i, j: (i, 0, KA_OFF // (2 * A_W))),
                  pl.BlockSpec(bias_a.shape, lambda i, j: (0, 0, 0, 0))],
        out_specs=pl.BlockSpec((1, T, A_W), lambda i, j: (i, j, 0)),
        out_shape=jax.ShapeDtypeStruct((b, seq, A_W), MXU_DTYPE),
        scratch_shapes=[pltpu.VMEM((seq, A_W), MXU_DTYPE), pltpu.VMEM((seq // T, A_W, T), MXU_DTYPE),
                        pltpu.VMEM((nbp, A_W), F32), pltpu.VMEM((nbp, A_HEADS * T), F32),
                        pltpu.VMEM((A_HEADS, HEAD_DIM, T), F32)],
        compiler_params=_cparams("arbitrary", "arbitrary"),
        name="moba_prompt",
    )(qa, row, bias_a)


def _diff_finish(o0, o1, lam, lane_v):
    o = o0 - lam * o1
    out = jnp.zeros_like(o)
    for h in range(B_HEADS):
        hm = _in_range(lane_v, h * B_V_DIM, (h + 1) * B_V_DIM)
        ms = jnp.sum(jnp.where(hm, o * o, 0.0), axis=-1, keepdims=True) * (1.0 / B_V_DIM)
        out = jnp.where(hm, o * lax.rsqrt(ms + LN_EPS), out)
    return out


def _diff_prompt_kernel(dp_ref, q_ref, kv_ref, bias_ref, subw_ref, o_ref, kb_ref, vt_ref, acc_ref, *, seq):
    qi = pl.program_id(1)

    @pl.when(qi == 0)
    def _():
        _cast_tiles(kv_ref, 0, B_QK_W, kb_ref, seq)
        _transpose_tiles(kv_ref, B_QK_W, B_W, vt_ref, seq)

    q = q_ref[0]
    lane_q = _lane_iota(B_QK_W)
    qs = jnp.concatenate([jnp.where(_in_range(lane_q, c * B_QK_DIM, (c + 1) * B_QK_DIM), q, jnp.zeros_like(q))
                          for c in range(2 * B_HEADS)], axis=0)
    causal = _causal_penalty()

    def far_fn(n):
        idx = jnp.minimum(qi - n, 2)
        return lambda c, s: s + bias_ref[c // 2, idx]

    outs = _attend_t(qs, kb_ref, vt_ref, acc_ref, [h * B_V_DIM for h in range(B_HEADS)], 0, qi, far_fn,
                     lambda c, s: s + bias_ref[c // 2, 0] + causal)
    lam = dp_ref[:, 0:1]
    heads = []
    for h in range(B_HEADS):
        o = outs[h][:, 0:T] - lam * outs[h][:, T:2 * T]
        ms = jnp.mean(o * o, axis=0, keepdims=True)
        heads.append(o * lax.rsqrt(ms + LN_EPS))
    out = jnp.concatenate(heads, axis=0) * subw_ref[...] * dp_ref[:, 1:2]
    o_ref[0] = out.T.astype(o_ref.dtype)


def _diff_prompt(dp, qb, row, bias_b, subw_t):
    b, seq, _ = row.shape
    kvw = B_QK_W + B_W
    assert B_V_DIM == HEAD_DIM
    return pl.pallas_call(
        functools.partial(_diff_prompt_kernel, seq=seq),
        grid=(b, seq // T),
        in_specs=[pl.BlockSpec((1, LANES), lambda i, j: (0, 0)),
                  pl.BlockSpec((1, T, B_QK_W), lambda i, j: (i, j, 0)),
                  pl.BlockSpec((1, seq, kvw), lambda i, j: (i, 0, KB_OFF // kvw)),
                  pl.BlockSpec(bias_b.shape, lambda i, j: (0, 0, 0, 0)),
                  pl.BlockSpec((B_W, T), lambda i, j: (0, 0))],
        out_specs=pl.BlockSpec((1, T, B_W), lambda i, j: (i, j, 0)),
        out_shape=jax.ShapeDtypeStruct((b, seq, B_W), MXU_DTYPE),
        scratch_shapes=[pltpu.VMEM((seq, B_QK_W), MXU_DTYPE), pltpu.VMEM((seq // T, B_W, T), MXU_DTYPE),
                        pltpu.VMEM((B_HEADS, B_V_DIM, 2 * T), F32)],
        compiler_params=_cparams("arbitrary", "arbitrary"),
        name="diff_prompt",
    )(dp, qb, row, bias_b, subw_t)


def _stack_group_queries(q, g):
    gm = _in_range(_lane_iota(C_KV_W), g * HEAD_DIM, (g + 1) * HEAD_DIM)
    q = q.astype(F32)
    return jnp.concatenate(
        [jnp.where(gm, q[:, hg * C_KV_W:(hg + 1) * C_KV_W], 0.0) for hg in range(C_HPG)], axis=0).astype(MXU_DTYPE)


def _stack_gate(gt, g, branch):
    cols = [gt[:, (g * C_HPG + hg) * N_BRANCH + branch:(g * C_HPG + hg) * N_BRANCH + branch + 1] for hg in range(C_HPG)]
    x = jnp.concatenate(cols, axis=0)
    return 1.0 / (1.0 + jnp.exp(-x))


def _cmp_attention(qs, kc, vc, q_pos_rows, n_cmp):
    nch = kc.shape[0]
    s = _dot_nt(qs, kc)
    r = _lane_iota(nch)
    ok = (r >= 1) & (r <= n_cmp) & ((r - 1) * CMP_STRIDE + (CMP_LEN - 1) <= q_pos_rows)
    s = jnp.where(ok, s, NEG)
    p = jnp.where(ok, jnp.exp(s - jnp.max(s, axis=-1, keepdims=True)), 0.0)
    l = jnp.sum(p, axis=-1, keepdims=True)
    p = p / jnp.where(l > 0.0, l, 1.0)
    return p, _dot(p.astype(MXU_DTYPE), vc)


def _nsa_prompt_kernel(q_ref, gt_ref, cmp_ref, kv_ref, win_ref, bias_ref, wsel_ref, o_ref,
                       ks_ref, vst_ref, kw_ref, vwt_ref, kc_ref, vct_ref, pen_ref, acc_ref, *, seq, n_cmp):
    qi = pl.program_id(1)
    nsel = seq // SEL_BLOCK
    bpt = T // SEL_BLOCK

    @pl.when(qi == 0)
    def _():
        _cast_tiles(kv_ref, 0, C_KV_W, ks_ref, seq)
        _transpose_tiles(kv_ref, C_KV_W, C_KV_W, vst_ref, seq)
        _cast_tiles(win_ref, 0, C_KV_W, kw_ref, seq)
        _transpose_tiles(win_ref, C_KV_W, C_KV_W, vwt_ref, seq)
        kc_ref[...] = cmp_ref[0, :, 0:C_KV_W].astype(kc_ref.dtype)
        vct_ref[...] = cmp_ref[0, :, C_KV_W:2 * C_KV_W].T.astype(vct_ref.dtype)

    q = q_ref[0]
    gates = 1.0 / (1.0 + jnp.exp(-gt_ref[0].T))
    q_pos = qi * T + _lane_iota(T)
    q_pos4 = jnp.concatenate([q_pos] * C_HPG, axis=1)
    nch = kc_ref.shape[0]
    r = lax.broadcasted_iota(jnp.int32, (nch, 1), 0)
    cmp_ok = (r >= 1) & (r <= n_cmp) & ((r - 1) * CMP_STRIDE + (CMP_LEN - 1) <= q_pos4)
    blk = lax.broadcasted_iota(jnp.int32, (pen_ref.shape[1], 1), 0)
    own = jnp.right_shift(q_pos, int(math.log2(SEL_BLOCK)))
    causal = _causal_penalty()
    jj = lax.broadcasted_iota(jnp.int32, (T, T), 0)
    ii = lax.broadcasted_iota(jnp.int32, (T, T), 1)
    qs = jnp.concatenate([_stack_group_queries(q, g) for g in range(C_GROUPS)], axis=0)
    vrows = [g * HEAD_DIM for g in range(C_GROUPS)]
    o_cmp = []
    for g in range(C_GROUPS):
        s = jnp.where(cmp_ok, _dot_nt(kc_ref[...], qs[g * C_HPG * T:(g + 1) * C_HPG * T]), NEG)
        p = jnp.where(cmp_ok, jnp.exp(s - jnp.max(s, axis=0, keepdims=True)), 0.0)
        l = jnp.sum(p, axis=0, keepdims=True)
        p = p / jnp.where(l > 0.0, l, 1.0)
        o_cmp.append(_dot(vct_ref[g * HEAD_DIM:(g + 1) * HEAD_DIM, :], p.astype(MXU_DTYPE)))
        pg = p[:, 0:T] + p[:, T:2 * T] + p[:, 2 * T:3 * T] + p[:, 3 * T:4 * T]
        ph, plo = _split_hi_lo(pg)
        p_slc = _dot(wsel_ref[...], ph) + _dot(wsel_ref[...], plo)
        forced = (blk == 0) | (blk == own) | (blk == own - 1)
        score = jnp.where(blk <= own, jnp.where(forced, jnp.inf, p_slc), -jnp.inf)
        rank = _rank_rows(score, nsel)
        pen_ref[g] = jnp.where((blk <= own) & (rank < SEL_TOPN), 0.0, NEG)

    def pen_tile(g, n):
        return jnp.concatenate([jnp.broadcast_to(pen_ref[g, pl.ds(n * bpt + b, 1), :], (SEL_BLOCK, T))
                                for b in range(bpt)], axis=0)

    def slc_far(n):
        idx = jnp.minimum(qi - n, 2)
        pens = [pen_tile(g, n) for g in range(C_GROUPS)]
        return lambda c, s: s + bias_ref[idx, c] + pens[c // C_HPG]

    own_pens = [pen_tile(g, qi) + causal for g in range(C_GROUPS)]
    o_slc = _attend_t(qs, ks_ref, vst_ref, acc_ref, vrows, 0, qi, slc_far,
                      lambda c, s: s + bias_ref[0, c] + own_pens[c // C_HPG])

    def win_far(n):
        d = qi - n
        idx = jnp.minimum(d, 2)
        pen = jnp.where(ii >= jj, jnp.where(d * T >= WINDOW, NEG, 0.0), 0.0)
        return lambda c, s: s + bias_ref[idx, c] + pen

    o_win = _attend_t(qs, kw_ref, vwt_ref, acc_ref, vrows, jnp.maximum(qi - WINDOW // T, 0), qi, win_far,
                      lambda c, s: s + bias_ref[0, c] + causal)

    def gate_row(g, branch):
        return jnp.concatenate([gates[(g * C_HPG + hg) * N_BRANCH + branch:(g * C_HPG + hg) * N_BRANCH + branch + 1, :]
                                for hg in range(C_HPG)], axis=1)

    outs = [gate_row(g, 0) * o_cmp[g] + gate_row(g, 1) * o_slc[g] + gate_row(g, 2) * o_win[g]
            for g in range(C_GROUPS)]
    out_t = jnp.concatenate([outs[g][:, hg * T:(hg + 1) * T] for hg in range(C_HPG) for g in range(C_GROUPS)], axis=0)
    o_ref[0] = out_t.T.astype(o_ref.dtype)


def _nsa_prompt(qc, gt, cmp, row, win, bias_c, wsel_t):
    b, seq, _ = row.shape
    nch = cmp.shape[1]
    n_cmp = (seq - CMP_LEN) // CMP_STRIDE + 1
    nselp = wsel_t.shape[0]
    assert seq // SEL_BLOCK >= SEL_TOPN and WINDOW == 2 * T and nselp >= seq // SEL_BLOCK
    kvw = 2 * C_KV_W
    nk = seq // T
    full = lambda a: pl.BlockSpec(a.shape, lambda i, j: (0,) * a.ndim)
    return pl.pallas_call(
        functools.partial(_nsa_prompt_kernel, seq=seq, n_cmp=n_cmp),
        grid=(b, nk),
        in_specs=[pl.BlockSpec((1, T, C_W), lambda i, j: (i, j, 0)),
                  pl.BlockSpec((1, T, GT_PAD), lambda i, j: (i, j, 0)),
                  pl.BlockSpec((1, nch, kvw), lambda i, j: (i, 0, 0)),
                  pl.BlockSpec((1, seq, kvw), lambda i, j: (i, 0, KS_OFF // kvw)),
                  pl.BlockSpec((1, seq, kvw), lambda i, j: (i, 0, 0)),
                  full(bias_c), full(wsel_t)],
        out_specs=pl.BlockSpec((1, T, C_W), lambda i, j: (i, j, 0)),
        out_shape=jax.ShapeDtypeStruct((b, seq, C_W), MXU_DTYPE),
        scratch_shapes=[pltpu.VMEM((seq, C_KV_W), MXU_DTYPE), pltpu.VMEM((nk, C_KV_W, T), MXU_DTYPE),
                        pltpu.VMEM((seq, C_KV_W), MXU_DTYPE), pltpu.VMEM((nk, C_KV_W, T), MXU_DTYPE),
                        pltpu.VMEM((nch, C_KV_W), MXU_DTYPE), pltpu.VMEM((C_KV_W, nch), MXU_DTYPE),
                        pltpu.VMEM((C_GROUPS, nselp, T), F32), pltpu.VMEM((C_GROUPS, HEAD_DIM, C_HPG * T), F32)],
        compiler_params=_cparams("arbitrary", "arbitrary"),
        name="nsa_prompt",
    )(qc, gt, cmp, row, win, bias_c, wsel_t)


PG = 16
PG_CMP = 32
NEW_PAD = LANES


def _rows_iota(n_rep, n):
    return jnp.concatenate([lax.broadcasted_iota(jnp.int32, (n, 1), 0)] * n_rep, axis=0)


def _pad_rows(x, rows):
    return jnp.concatenate([x, jnp.zeros((rows - x.shape[0], x.shape[1]), x.dtype)], axis=0)


def _page_specs(layer, width, col_block, n_ops):
    def page_map(b, s, pt, *, k):
        return (layer, pt[b, s * n_ops + k], 0, col_block)
    return [pl.BlockSpec((1, 1, PAGE_SIZE, width), functools.partial(page_map, k=k)) for k in range(n_ops)]


def _step_bias_spec(shape, steps):
    return pl.BlockSpec((1,) + shape[1:], lambda b, s, pt: (jnp.where(s == steps - 1, 1, 0),) + (0,) * (len(shape) - 1))


def _new_token_logits(qs, k_new, bias_new, nq):
    s = _dot_nt(qs, _pad_rows(k_new, NEW_PAD).astype(MXU_DTYPE)) + bias_new
    j = _lane_iota(NEW_PAD)
    i = _rows_iota(qs.shape[0] // nq, nq)
    return jnp.where((j < nq) & (j <= i), s, NEG)


def _nsa_sample_pre_kernel(q_ref, cmp_ref, state_ref, winnew_ref, bias_ref, wsel_ref, sel_ref, ocw_ref,
                           *, past, n_cmp, nq):
    q = q_ref[0]
    wb = state_ref.shape[2]
    wk = bias_ref.shape[-1]
    rows = C_HPG * nq
    kc = cmp_ref[0, :, 0:C_KV_W].astype(MXU_DTYPE)
    vc = cmp_ref[0, :, C_KV_W:2 * C_KV_W].astype(MXU_DTYPE)
    pad = jnp.zeros((wk - wb - nq, C_KV_W), F32)
    kw = jnp.concatenate([state_ref[0, 0, :, 0:C_KV_W], winnew_ref[0, :, 0:C_KV_W], pad], axis=0).astype(MXU_DTYPE)
    vw = jnp.concatenate([state_ref[0, 0, :, C_KV_W:], winnew_ref[0, :, C_KV_W:], pad], axis=0).astype(MXU_DTYPE)
    nsl = wsel_ref.shape[1]
    lane = _lane_iota(nsl)
    big = jnp.int32(nsl)
    i_q = lax.broadcasted_iota(jnp.int32, (nq, 1), 0)
    q_pos = past + i_q
    q_pos4 = past + _rows_iota(C_HPG, nq)
    for g in range(C_GROUPS):
        qs = _stack_group_queries(q, g)
        p, o_cmp = _cmp_attention(qs, kc, vc, q_pos4, n_cmp)
        pg = p[0:nq] + p[nq:2 * nq] + p[2 * nq:3 * nq] + p[3 * nq:4 * nq]
        ph, plo = _split_hi_lo(pg)
        p_slc = _dot(ph, wsel_ref[...]) + _dot(plo, wsel_ref[...])
        own = jnp.right_shift(q_pos, int(math.log2(SEL_BLOCK)))
        forced = (lane == 0) | (lane == own) | (lane == own - 1)
        score = jnp.where(lane <= own, jnp.where(forced, jnp.inf, p_slc), -jnp.inf)
        sel = jnp.zeros((nq, nsl), F32)
        for _ in range(SEL_TOPN):
            v = jnp.max(score, axis=-1, keepdims=True)
            idx = jnp.min(jnp.where(score == v, lane, big), axis=-1, keepdims=True)
            hit = lane == idx
            sel = jnp.where(hit & (v > -jnp.inf), 1.0, sel)
            score = jnp.where(hit, -jnp.inf, score)
        sel_ref[0, g * rows:(g + 1) * rows, :] = jnp.concatenate([sel] * C_HPG, axis=0)
        s = _dot_nt(qs, kw) +bias_ref[g * C_HPG:(g + 1) * C_HPG].reshape(rows, wk)
        j = _lane_iota(wk)
        rel = wb + _rows_iota(C_HPG, nq) - j
        ok = (j < wb + nq) & (rel >= 0) & (rel < WINDOW)
        s = jnp.where(ok, s, NEG)
        pw = jnp.exp(s - jnp.max(s, axis=-1, keepdims=True))
        o_win = _dot(pw.astype(MXU_DTYPE), vw) / jnp.sum(pw, axis=-1, keepdims=True)
        ocw_ref[0, g * rows:(g + 1) * rows, 0:C_KV_W] = o_cmp
        ocw_ref[0, g * rows:(g + 1) * rows, C_KV_W:] = o_win


def _nsa_sample_pre(qc, cmp, state_win, layer, win_new, bias_w, wsel, past):
    bs, nq, _ = qc.shape
    nch = cmp.shape[1]
    wb = state_win.shape[2]
    n_cmp = (past + nq - CMP_LEN) // CMP_STRIDE + 1
    rows = C_HEADS * nq
    full = lambda a: pl.BlockSpec(a.shape, lambda b: (0,) * a.ndim)
    return pl.pallas_call(
        functools.partial(_nsa_sample_pre_kernel, past=past, n_cmp=n_cmp, nq=nq),
        grid=(bs,),
        in_specs=[pl.BlockSpec((1, nq, C_W), lambda b: (b, 0, 0)),
                  pl.BlockSpec((1, nch, 2 * C_KV_W), lambda b: (b, 0, 0)),
                  pl.BlockSpec((1, 1, wb, WIN_ROW), lambda b: (layer, b, 0, 0)),
                  pl.BlockSpec((1, nq, WIN_ROW), lambda b: (b, 0, 0)),
                  full(bias_w), full(wsel)],
        out_specs=[pl.BlockSpec((1, rows, wsel.shape[1]), lambda b: (b, 0, 0)),
                   pl.BlockSpec((1, rows, 2 * C_KV_W), lambda b: (b, 0, 0))],
        out_shape=[jax.ShapeDtypeStruct((bs, rows, wsel.shape[1]), F32),
                   jax.ShapeDtypeStruct((bs, rows, 2 * C_KV_W), F32)],
        compiler_params=_cparams("arbitrary"),
        name="nsa_sample_pre",
    )(qc, cmp, state_win, win_new, bias_w, wsel)


def _nsa_sample_slc_kernel(pt_ref, *refs, nq):
    del pt_ref
    pages = refs[:PG]
    (q_ref, gt_ref, selw_ref, exp_ref, bias_ref, rownew_ref, biasnew_ref, ocw_ref,
     o_ref, m_ref, l_ref, acc_ref) = refs[PG:]
    rows = C_HPG * nq
    q = q_ref[0]

    def init():
        m_ref[...] = jnp.full_like(m_ref, NEG)
        l_ref[...] = jnp.zeros_like(l_ref)
        acc_ref[...] = jnp.zeros_like(acc_ref)

    def main():
        k = jnp.concatenate([p[0, 0, :, 0:C_KV_W] for p in pages], axis=0).astype(MXU_DTYPE)
        v = jnp.concatenate([p[0, 0, :, C_KV_W:] for p in pages], axis=0).astype(MXU_DTYPE)
        for g in range(C_GROUPS):
            rs = slice(g * rows, (g + 1) * rows)
            qs = _stack_group_queries(q, g)
            s = _dot_nt(qs, k) + bias_ref[0, g * C_HPG:(g + 1) * C_HPG].reshape(rows, PG * PAGE_SIZE)
            ok = _dot(selw_ref[0, 0, rs, :].astype(MXU_DTYPE), exp_ref[...]) > 0.5
            p, alpha, m, l = _online_update(jnp.where(ok, s, NEG), m_ref[rs], l_ref[rs])
            m_ref[rs] = m
            l_ref[rs] = l
            acc_ref[rs] = acc_ref[rs] * alpha + _dot(p.astype(MXU_DTYPE), v)

    def final():
        gt = gt_ref[0]
        k_new = rownew_ref[0, :, 0:C_KV_W]
        v_new = _pad_rows(rownew_ref[0, :, C_KV_W:], NEW_PAD).astype(MXU_DTYPE)
        outs = []
        for g in range(C_GROUPS):
            rs = slice(g * rows, (g + 1) * rows)
            qs = _stack_group_queries(q, g)
            s = _new_token_logits(qs, k_new, biasnew_ref[g * C_HPG:(g + 1) * C_HPG].reshape(rows, NEW_PAD), nq)
            p, alpha, _, l = _online_update(s, m_ref[rs], l_ref[rs])
            o_slc = (acc_ref[rs] * alpha + _dot(p.astype(MXU_DTYPE), v_new)) / l
            ocw = ocw_ref[0, rs, :]
            outs.append(_stack_gate(gt, g, 0) * ocw[:, 0:C_KV_W] + _stack_gate(gt, g, 1) * o_slc
                        + _stack_gate(gt, g, 2) * ocw[:, C_KV_W:])
        g0 = _lane_iota(C_KV_W) < HEAD_DIM
        for hg in range(C_HPG):
            chunk = jnp.where(g0, outs[0][hg * nq:(hg + 1) * nq], outs[1][hg * nq:(hg + 1) * nq])
            o_ref[0, :, hg * C_KV_W:(hg + 1) * C_KV_W] = chunk.astype(o_ref.dtype)

    return init, main, final


def _nsa_sample_slc(cache, layer, page_table, qc, gt, selw, expand, bias_steps, row_new, bias_new, ocw):
    bs, nq, _ = qc.shape
    steps = page_table.shape[1] // PG
    rows = C_HEADS * nq
    kvw = 2 * C_KV_W
    per_seq = lambda shape: pl.BlockSpec((1,) + shape, lambda b, s, pt: (b,) + (0,) * len(shape))
    full = lambda a: pl.BlockSpec(a.shape, lambda b, s, pt: (0,) * a.ndim)
    return dict(
        body=functools.partial(_nsa_sample_slc_kernel, nq=nq),
        in_specs=_page_specs(layer, kvw, KS_OFF // kvw, PG)
        + [per_seq((nq, C_W)), per_seq((nq, GT_PAD)),
           pl.BlockSpec((1, 1, rows, LANES), lambda b, s, pt: (b, s, 0, 0)),
           full(expand), _step_bias_spec(bias_steps.shape, steps),
           pl.BlockSpec((1, nq, kvw), lambda b, s, pt: (b, 0, KS_OFF // kvw)),
           full(bias_new), per_seq((rows, kvw))],
        args=[cache] * PG + [qc, gt, selw, expand, bias_steps, row_new, bias_new, ocw],
        out_spec=per_seq((nq, C_W)),
        out_shape=jax.ShapeDtypeStruct((bs, nq, C_W), F32),
        scratch=[pltpu.VMEM((rows, 1), F32), pltpu.VMEM((rows, 1), F32), pltpu.VMEM((rows, C_KV_W), F32)])


def _diff_sample_kernel(pt_ref, *refs, nq):
    del pt_ref
    pages = refs[:PG]
    dp_ref, q_ref, bias_ref, rownew_ref, biasnew_ref, subw_ref, o_ref, m_ref, l_ref, acc_ref = refs[PG:]

    def init():
        m_ref[...] = jnp.full_like(m_ref, NEG)
        l_ref[...] = jnp.zeros_like(l_ref)
        acc_ref[...] = jnp.zeros_like(acc_ref)

    q = q_ref[0]
    lane_q = _lane_iota(B_QK_W)
    qs = jnp.concatenate([jnp.where(_in_range(lane_q, c * B_QK_DIM, (c + 1) * B_QK_DIM), q, 0.0)
                          for c in range(2 * B_HEADS)], axis=0).astype(MXU_DTYPE)

    def head_rows(b):
        return jnp.concatenate([b[c // 2] for c in range(2 * B_HEADS)], axis=0)

    def main():
        k = jnp.concatenate([p[0, 0, :, 0:B_QK_W] for p in pages], axis=0).astype(MXU_DTYPE)
        v = jnp.concatenate([p[0, 0, :, B_QK_W:] for p in pages], axis=0).astype(MXU_DTYPE)
        s = _dot_nt(qs, k) + head_rows(bias_ref[0])
        p, alpha, m, l = _online_update(s, m_ref[...], l_ref[...])
        m_ref[...] = m
        l_ref[...] = l
        acc_ref[...] = acc_ref[...] * alpha + _dot(p.astype(MXU_DTYPE), v)

    def final():
        k_new = rownew_ref[0, :, 0:B_QK_W]
        v_new = _pad_rows(rownew_ref[0, :, B_QK_W:], NEW_PAD).astype(MXU_DTYPE)
        s = _new_token_logits(qs, k_new, head_rows(biasnew_ref[...]), nq)
        p, alpha, _, l = _online_update(s, m_ref[...], l_ref[...])
        o = (acc_ref[...] * alpha + _dot(p.astype(MXU_DTYPE), v_new)) / l
        lane_v = _lane_iota(B_W)
        maps = [jnp.zeros((nq, B_W), F32), jnp.zeros((nq, B_W), F32)]
        for c in range(2 * B_HEADS):
            hm = _in_range(lane_v, (c // 2) * B_V_DIM, (c // 2 + 1) * B_V_DIM)
            maps[c % 2] = jnp.where(hm, o[c * nq:(c + 1) * nq], maps[c % 2])
        out = _diff_finish(maps[0], maps[1], dp_ref[:, 0:1], lane_v) * subw_ref[...] * dp_ref[:, 1:2]
        o_ref[0] = out.astype(o_ref.dtype)

    return init, main, final


def _diff_sample(cache, layer, page_table, dp, qb, bias_steps, row_new, bias_new, subw):
    bs, nq, _ = qb.shape
    steps = page_table.shape[1] // PG
    rows = 2 * B_HEADS * nq
    kvw = B_QK_W + B_W
    per_seq = lambda shape: pl.BlockSpec((1,) + shape, lambda b, s, pt: (b,) + (0,) * len(shape))
    full = lambda a: pl.BlockSpec(a.shape, lambda b, s, pt: (0,) * a.ndim)
    return dict(
        body=functools.partial(_diff_sample_kernel, nq=nq),
        in_specs=_page_specs(layer, kvw, KB_OFF // kvw, PG)
        + [full(dp), per_seq((nq, B_QK_W)), _step_bias_spec(bias_steps.shape, steps),
           pl.BlockSpec((1, nq, kvw), lambda b, s, pt: (b, 0, KB_OFF // kvw)), full(bias_new), full(subw)],
        args=[cache] * PG + [dp, qb, bias_steps, row_new, bias_new, subw],
        out_spec=per_seq((nq, B_W)),
        out_shape=jax.ShapeDtypeStruct((bs, nq, B_W), F32),
        scratch=[pltpu.VMEM((rows, 1), F32), pltpu.VMEM((rows, 1), F32), pltpu.VMEM((rows, B_W), F32)])


def _moba_sample_kernel(pt_ref, *refs, nq, n_past_blocks):
    del pt_ref
    pages = refs[:PG]
    q_ref, bias_ref, rownew_ref, biasnew_ref, o_ref, oblk_ref, gate_ref, mst_ref, lst_ref = refs[PG:]
    step = pl.program_id(1)
    rows = A_HEADS * nq
    bps = PG * PAGE_SIZE // MOBA_BLOCK
    nbl = gate_ref.shape[1]
    lane_b = _lane_iota(nbl)
    lane_q = _lane_iota(A_W)
    q = q_ref[0]
    qf = jnp.concatenate([jnp.where(_in_range(lane_q, h * HEAD_DIM, (h + 1) * HEAD_DIM), q, 0.0)
                          for h in range(A_HEADS)], axis=0)
    qs = qf.astype(MXU_DTYPE)

    def head_rows(b):
        return jnp.concatenate([b[h] for h in range(A_HEADS)], axis=0)

    def init():
        gate_ref[...] = jnp.zeros_like(gate_ref)
        mst_ref[...] = jnp.zeros_like(mst_ref)
        lst_ref[...] = jnp.zeros_like(lst_ref)

    def main():
        kf = jnp.concatenate([p[0, 0, :, 0:A_W] for p in pages], axis=0)
        k = kf.astype(MXU_DTYPE)
        v = jnp.concatenate([p[0, 0, :, A_W:] for p in pages], axis=0).astype(MXU_DTYPE)
        s = _dot_nt(qs, k) + head_rows(bias_ref[0])
        gate, mst, lst = gate_ref[...], mst_ref[...], lst_ref[...]
        for j in range(bps):
            cs = slice(j * MOBA_BLOCK, (j + 1) * MOBA_BLOCK)
            sj = s[:, cs]
            mj = jnp.max(sj, axis=-1, keepdims=True)
            pj = jnp.exp(sj - mj)
            n = step * bps + j
            oblk_ref[n] = _dot(pj.astype(MXU_DTYPE), v[cs, :])
            mean = jnp.mean(kf[cs, :], axis=0, keepdims=True)
            hit = lane_b == n
            gate = jnp.where(hit, jnp.sum(qf * mean, axis=-1, keepdims=True), gate)
            mst = jnp.where(hit, mj, mst)
            lst = jnp.where(hit, jnp.sum(pj, axis=-1, keepdims=True), lst)
        gate_ref[...] = gate
        mst_ref[...] = mst
        lst_ref[...] = lst

    def final():
        gate, mst, lst = gate_ref[...], mst_ref[...], lst_ref[...]
        big = jnp.int32(nbl)
        score = jnp.where(lane_b < n_past_blocks, gate, -jnp.inf)
        sel = jnp.zeros((rows, nbl), F32)
        for _ in range(MOBA_TOPK):
            vmax = jnp.max(score, axis=-1, keepdims=True)
            idx = jnp.min(jnp.where(score == vmax, lane_b, big), axis=-1, keepdims=True)
            hit = lane_b == idx
            sel = jnp.where(hit & (vmax > -jnp.inf), 1.0, sel)
            score = jnp.where(hit, -jnp.inf, score)
        k_new = rownew_ref[0, :, 0:A_W]
        v_new = _pad_rows(rownew_ref[0, :, A_W:], NEW_PAD).astype(MXU_DTYPE)
        s_new = _new_token_logits(qs, k_new, head_rows(biasnew_ref[...]), nq)
        m_new = jnp.max(s_new, axis=-1, keepdims=True)
        p_new = jnp.exp(s_new - m_new)
        chosen = sel > 0.5
        m_all = jnp.maximum(m_new, jnp.max(jnp.where(chosen, mst, NEG), axis=-1, keepdims=True))
        w = jnp.where(chosen, jnp.exp(mst - m_all), 0.0)
        w_new = jnp.exp(m_new - m_all)
        l_all = jnp.sum(w * lst, axis=-1, keepdims=True) + w_new * jnp.sum(p_new, axis=-1, keepdims=True)
        o = w_new * _dot(p_new.astype(MXU_DTYPE), v_new)
        for n in range(n_past_blocks):
            o = o + w[:, n:n + 1] * oblk_ref[n]
        o = o / l_all
        out = jnp.zeros((nq, A_W), F32)
        for h in range(A_HEADS):
            out = jnp.where(_in_range(lane_q, h * HEAD_DIM, (h + 1) * HEAD_DIM), o[h * nq:(h + 1) * nq], out)
        o_ref[0] = out.astype(o_ref.dtype)

    return init, main, final


def _moba_sample(cache, layer, page_table, qa, bias_steps, row_new, bias_new, past):
    bs, nq, _ = qa.shape
    steps = page_table.shape[1] // PG
    rows = A_HEADS * nq
    kvw = 2 * A_W
    n_past_blocks = past // MOBA_BLOCK
    nbl = -(-n_past_blocks // LANES) * LANES
    per_seq = lambda shape: pl.BlockSpec((1,) + shape, lambda b, s, pt: (b,) + (0,) * len(shape))
    full = lambda a: pl.BlockSpec(a.shape, lambda b, s, pt: (0,) * a.ndim)
    return dict(
        body=functools.partial(_moba_sample_kernel, nq=nq, n_past_blocks=n_past_blocks),
        in_specs=_page_specs(layer, kvw, KA_OFF // kvw, PG)
        + [per_seq((nq, A_W)), _step_bias_spec(bias_steps.shape, steps),
           pl.BlockSpec((1, nq, kvw), lambda b, s, pt: (b, 0, KA_OFF // kvw)), full(bias_new)],
        args=[cache] * PG + [qa, bias_steps, row_new, bias_new],
        out_spec=per_seq((nq, A_W)),
        out_shape=jax.ShapeDtypeStruct((bs, nq, A_W), F32),
        scratch=[pltpu.VMEM((n_past_blocks, rows, A_W), F32)] + [pltpu.VMEM((rows, nbl), F32)] * 3)


def _sample_stream_kernel(pt_ref, *refs, parts):
    n_in = [len(p["in_specs"]) for p in parts]
    n_sc = [len(p["scratch"]) for p in parts]
    outs = refs[sum(n_in):sum(n_in) + len(parts)]
    i0, s0 = 0, sum(n_in) + len(parts)
    phases = []
    for k, p in enumerate(parts):
        phases.append(p["body"](pt_ref, *refs[i0:i0 + n_in[k]], outs[k], *refs[s0:s0 + n_sc[k]]))
        i0 += n_in[k]
        s0 += n_sc[k]
    step = pl.program_id(1)

    @pl.when(step == 0)
    def _():
        for init, _, _ in phases:
            init()

    for _, main, _ in phases:
        main()

    @pl.when(step == pl.num_programs(1) - 1)
    def _():
        for _, _, final in phases:
            final()


def _sample_stream(page_table, parts):
    bs = page_table.shape[0]
    steps = page_table.shape[1] // PG
    grid_spec = pltpu.PrefetchScalarGridSpec(
        num_scalar_prefetch=1,
        grid=(bs, steps),
        in_specs=[s for p in parts for s in p["in_specs"]],
        out_specs=[p["out_spec"] for p in parts],
        scratch_shapes=[s for p in parts for s in p["scratch"]],
    )
    return pl.pallas_call(
        functools.partial(_sample_stream_kernel, parts=parts),
        grid_spec=grid_spec,
        out_shape=[p["out_shape"] for p in parts],
        compiler_params=_cparams("arbitrary", "arbitrary"),
        name="sample_stream",
    )(page_table, *[a for p in parts for a in p["args"]])


def _t5_bucket(rel):
    n = jnp.maximum(rel, 0)
    nf = jnp.maximum(n, 1).astype(F32)
    large = MAX_EXACT + (jnp.log(nf / MAX_EXACT) / math.log(T5_MAX_DIST / MAX_EXACT)
                         * (N_BUCKETS - MAX_EXACT)).astype(jnp.int32)
    return jnp.where(n < MAX_EXACT, n, jnp.minimum(large, N_BUCKETS - 1))


def _nsa_perm():
    return np.array([g * C_HPG * HEAD_DIM + hg * HEAD_DIM + d
                     for hg in range(C_HPG) for g in range(C_GROUPS) for d in range(HEAD_DIM)], np.int32)


def _sel_weights(nch, n_cmp, nsl):
    r = np.arange(nch)[:, None]
    s = (r - 1) * CMP_STRIDE
    b0 = np.arange(nsl)[None, :] * SEL_BLOCK
    ov = np.clip(np.minimum(s + CMP_LEN, b0 + SEL_BLOCK) - np.maximum(s, b0), 0, None)
    ov = np.where((r >= 1) & (r <= n_cmp), ov, 0)
    return jnp.asarray(ov.astype(np.float32) / np.float32(CMP_STRIDE), MXU_DTYPE)


def _block_expand(n_tiles, lanes, keys_per_tile):
    key = np.arange(keys_per_tile)[None, None, :] + np.arange(n_tiles)[:, None, None] * keys_per_tile
    blk = np.arange(lanes)[None, :, None]
    return jnp.asarray((key // SEL_BLOCK == blk).astype(np.float32), MXU_DTYPE)


def _compress_weights(pos_k, pos_v, k_w1, k_w2, v_w1, v_w2):
    half_rows = CMP_STRIDE * HEAD_DIM

    def first_layer(w1, half):
        return w1[half * half_rows:(half + 1) * half_rows].reshape(CMP_STRIDE // 4, 4 * HEAD_DIM, CMP_HIDDEN)

    def second_layer(w2):
        z = jnp.zeros_like(w2)
        return jnp.stack([jnp.concatenate([w2, z], axis=1), jnp.concatenate([z, w2], axis=1)])

    def pair_pos(pos):
        return pos.astype(F32).reshape(2, CMP_STRIDE // 2, 2 * HEAD_DIM)

    pos2 = jnp.stack([pair_pos(pos_k), pair_pos(pos_v)])
    wlo = jnp.stack([first_layer(k_w1, 0), first_layer(v_w1, 0)]).astype(MXU_DTYPE)
    whi = jnp.stack([first_layer(k_w1, 1), first_layer(v_w1, 1)]).astype(MXU_DTYPE)
    w2b = jnp.stack([second_layer(k_w2), second_layer(v_w2)]).astype(MXU_DTYPE)
    return pos2, wlo, whi, w2b


def kernel(x_prompt, x_sample, cache_kv, state_win, page_table, rel_bias_table, w_in, w_out, lam_q1, lam_k1, lam_q2, lam_k2, diff_subln_w, cmp_pos_k, cmp_pos_v, cmp_k_w1, cmp_k_w2, cmp_v_w1, cmp_v_w2, ln_mix_g, ln_mix_b, ln_ffn_g, ln_ffn_b, router_w1, router_b1, router_w2, router_b2, expert_w1, expert_w3, expert_w2):
    depth = w_in.shape[0]
    bp, seq, _ = x_prompt.shape
    bs, nq, _ = x_sample.shape
    n_pages = page_table.shape[1]
    past = n_pages * PAGE_SIZE
    wb = state_win.shape[2]
    alpha = (2.0 * depth) ** 0.25
    assert cache_kv.shape[2] == PAGE_SIZE and nq < CMP_STRIDE and nq % SUBLANES == 0
    assert past % MOBA_BLOCK == 0 and n_pages % PG == 0 and PG * PAGE_SIZE // SEL_BLOCK <= LANES
    assert wb == WINDOW and wb <= past

    perm = _nsa_perm()
    tab = rel_bias_table.astype(F32)

    def toeplitz(rel):
        bucket = _t5_bucket(jnp.asarray(rel.astype(np.int32))).reshape(-1, 1)
        onehot = (bucket == jnp.arange(N_BUCKETS, dtype=jnp.int32)[None, :]).astype(F32)
        out = lax.dot_general(tab, onehot, (((1,), (1,)), ((), ())), precision=lax.Precision.HIGHEST)
        return out.reshape((tab.shape[0],) + rel.shape)

    assert T + 1 >= T5_MAX_DIST and PG * PAGE_SIZE >= T5_MAX_DIST
    ti = np.arange(T)[None, :] - np.arange(T)[:, None]
    bias_tiles = toeplitz(np.stack([ti + d * T for d in range(3)]))
    ha, hb = A_HEADS, A_HEADS + B_HEADS
    bias_a_p, bias_b_p = bias_tiles[:ha], bias_tiles[ha:hb]
    bias_c_p = jnp.swapaxes(bias_tiles[hb:], 0, 1)
    qi = np.arange(nq)[:, None]
    step_keys = np.arange(PG * PAGE_SIZE)[None, :]
    bias_steps = jnp.swapaxes(toeplitz(np.stack([2 * PG * PAGE_SIZE + qi - step_keys,
                                                 PG * PAGE_SIZE + qi - step_keys])), 0, 1)
    bias_new = toeplitz(qi - np.arange(NEW_PAD)[None, :])
    wk = -(-(wb + nq) // LANES) * LANES
    bias_win = toeplitz(wb + qi - np.arange(wk)[None, :])[hb:]

    nch_p = seq // CMP_STRIDE
    nselp = -(-(seq // SEL_BLOCK) // SUBLANES) * SUBLANES
    wsel_p = _sel_weights(nch_p, (seq - CMP_LEN) // CMP_STRIDE + 1, nselp).T
    nch_s = past // CMP_STRIDE
    nsel_s = past // SEL_BLOCK + 1
    steps = n_pages // PG
    bps = PG * PAGE_SIZE // SEL_BLOCK
    nsl = -(-max(nsel_s, steps * bps) // LANES) * LANES
    wsel_s = _sel_weights(nch_s, (past + nq - CMP_LEN) // CMP_STRIDE + 1, nsl)
    expand_s = _block_expand(1, LANES, PG * PAGE_SIZE)[0]
    pt_prompt = jnp.arange(bp, dtype=jnp.int32)[:, None]
    pg_cmp = math.gcd(PG_CMP, n_pages)

    xp = x_prompt.reshape(bp * seq, D_MODEL)
    xs = x_sample.reshape(bs * nq, D_MODEL)
    kv_p, win_p, kv_s, win_s = [], [], [], []
    for l in range(depth):
        w = w_in[l]
        w_packed = jnp.concatenate(
            [w[:, :A_W] * HEAD_DIM ** -0.5, w[:, A_W:A_W + B_QK_W] * B_QK_DIM ** -0.5,
             w[:, A_W + B_QK_W:A_W + B_QK_W + C_W][:, perm] * HEAD_DIM ** -0.5,
             jnp.pad(w[:, Q_W - GATE_W:Q_W], ((0, 0), (0, GT_PAD - GATE_W))), w[:, Q_W:]], axis=1).astype(MXU_DTYPE)
        wo = w_out[l]
        w_out_p = jnp.concatenate([wo[:A_W + B_W], wo[A_W + B_W:][perm]], axis=0).astype(MXU_DTYPE)
        lam_init = 0.8 - 0.6 * math.exp(-0.3 * l)
        lam = (jnp.exp(jnp.sum(lam_q1[l].astype(F32) * lam_k1[l].astype(F32)))
               - jnp.exp(jnp.sum(lam_q2[l].astype(F32) * lam_k2[l].astype(F32))) + lam_init)
        dp = jnp.zeros((1, LANES), F32).at[0, 0].set(lam).at[0, 1].set(1.0 - lam_init)
        subw = jnp.tile(diff_subln_w[l].astype(F32), B_HEADS)[None, :]
        cw = _compress_weights(cmp_pos_k[l], cmp_pos_v[l], cmp_k_w1[l], cmp_k_w2[l], cmp_v_w1[l], cmp_v_w2[l])
        wr = jnp.concatenate([router_w1[l], jnp.moveaxis(router_w2[l], 0, 1).reshape(D_MODEL, N_EXPERTS)], axis=1)
        wr = jnp.pad(wr.astype(F32), ((0, 0), (0, ROUTER_LANES - wr.shape[1])))
        wr_hi = wr.astype(MXU_DTYPE)
        wr_lo = (wr - wr_hi.astype(F32)).astype(MXU_DTYPE)
        br = jnp.concatenate([router_b1[l], router_b2[l].reshape(-1)]).astype(F32)
        br = jnp.pad(br, (0, ROUTER_LANES - br.shape[0]))[None, :]
        ew1, ew3, ew2 = expert_w1[l].astype(MXU_DTYPE), expert_w3[l].astype(MXU_DTYPE), expert_w2[l].astype(MXU_DTYPE)
        g_mix, b_mix = ln_mix_g[l].astype(F32)[None, :], ln_mix_b[l].astype(F32)[None, :]
        g_ffn, b_ffn = ln_ffn_g[l].astype(F32)[None, :], ln_ffn_b[l].astype(F32)[None, :]

        qa, qb, qc, gt, row, win = _inproj(xp, w_packed)
        r3 = lambda a: a.reshape(bp, seq, a.shape[-1])
        row3, win3 = r3(row), r3(win)
        cmp_p = _compress(row3[None], 0, pt_prompt, KC_OFF // C_KV_W, cw, 1)
        oa = _moba_prompt(r3(qa), row3, bias_a_p)
        ob = _diff_prompt(dp, r3(qb), row3, bias_b_p, jnp.broadcast_to(subw.T, (B_W, T)))
        oc = _nsa_prompt(r3(qc), r3(gt), cmp_p, row3, win3, bias_c_p, wsel_p)
        f2 = lambda a: a.reshape(bp * seq, a.shape[-1])
        xp = _outproj_ln(f2(oa), f2(ob), f2(oc), xp, w_out_p, g_mix, b_mix, alpha)
        xp = _moe_ln(xp, wr_hi, wr_lo, br, ew1, ew3, ew2, g_ffn, b_ffn, alpha)
        kv_p.append(row3)
        win_p.append(win3[:, seq - min(WINDOW, seq):])

        qa, qb, qc, gt, row, win = _inproj(xs, w_packed)
        s3 = lambda a: a.reshape(bs, nq, a.shape[-1]).astype(F32)
        row3, win3 = s3(row), s3(win)
        cmp_s = _compress(cache_kv, l, page_table, KC_OFF // C_KV_W, cw, pg_cmp)
        sel, ocw = _nsa_sample_pre(s3(qc), cmp_s, state_win, l, win3, bias_win, wsel_s, past)
        selw = sel[:, :, :steps * bps].reshape(bs, C_HEADS * nq, steps, bps)
        selw = jnp.pad(jnp.moveaxis(selw, 2, 1), ((0, 0), (0, 0), (0, 0), (0, LANES - bps)))
        oc, ob, oa = _sample_stream(page_table, [
            _nsa_sample_slc(cache_kv, l, page_table, s3(qc), s3(gt), selw, expand_s, bias_steps[:, hb:], row3,
                            bias_new[hb:], ocw),
            _diff_sample(cache_kv, l, page_table, dp, s3(qb), bias_steps[:, ha:hb], row3, bias_new[ha:hb], subw),
            _moba_sample(cache_kv, l, page_table, s3(qa), bias_steps[:, :ha], row3, bias_new[:ha], past)])
        f2 = lambda a: a.reshape(bs * nq, a.shape[-1]).astype(MXU_DTYPE)
        xs = _outproj_ln(f2(oa), f2(ob), f2(oc), xs, w_out_p, g_mix, b_mix, alpha)
        xs = _moe_ln(xs, wr_hi, wr_lo, br, ew1, ew3, ew2, g_ffn, b_ffn, alpha)
        kv_s.append(row3)
        win_s.append(jnp.concatenate([state_win[l], win3.astype(state_win.dtype)], axis=1)[:, nq:])

    return (xp.reshape(bp, seq, D_MODEL), xs.reshape(bs, nq, D_MODEL),
            jnp.stack(kv_p), jnp.stack(win_p), jnp.stack(kv_s), jnp.stack(win_s))
```

```python
import functools
import math

import numpy as np
import jax
import jax.numpy as jnp
from jax import lax
from jax.experimental import pallas as pl
from jax.experimental.pallas import tpu as pltpu

D_MODEL = 1024
PAGE_SIZE = 128
HEAD_DIM = 64
A_HEADS = 4
MOBA_BLOCK = 256
MOBA_TOPK = 3
B_HEADS = 4
B_QK_DIM = 32
B_V_DIM = 2 * B_QK_DIM
C_HEADS = 8
C_GROUPS = 2
C_HPG = C_HEADS // C_GROUPS
CMP_LEN = 32
CMP_STRIDE = 16
CMP_HIDDEN = 256
SEL_BLOCK = 64
SEL_TOPN = 16
WINDOW = 512
N_BRANCH = 3
N_BUCKETS = 32
MAX_EXACT = N_BUCKETS // 2
T5_MAX_DIST = 128
A_W = A_HEADS * HEAD_DIM
B_QK_W = B_HEADS * 2 * B_QK_DIM
B_W = B_HEADS * B_V_DIM
C_W = C_HEADS * HEAD_DIM
C_KV_W = C_GROUPS * HEAD_DIM
MIX_W = A_W + B_W + C_W
GATE_W = C_HEADS * N_BRANCH
Q_W = A_W + B_QK_W + C_W + GATE_W
KV_ROW = 2 * A_W + B_QK_W + B_W + 4 * C_KV_W
WIN_ROW = 2 * C_KV_W
N_GROUPS = 4
EXPERTS_PER_GROUP = 8
N_EXPERTS = N_GROUPS * EXPERTS_PER_GROUP
EXPERT_FF = 256
LN_EPS = 1e-5
NEG = -1e30
F32 = jnp.float32
MXU_DTYPE = jnp.bfloat16

LANES = 128
SUBLANES = 8
VMEM_LIMIT_BYTES = 56 * 1024 * 1024

T = 256
GT_PAD = LANES
QA_OFF, QB_OFF, QC_OFF, GT_OFF = 0, A_W, A_W + B_QK_W, A_W + B_QK_W + C_W
ROW_OFF = GT_OFF + GT_PAD
WIN_OFF = ROW_OFF + KV_ROW
IN_W_PACKED = WIN_OFF + WIN_ROW
KA_OFF, VA_OFF, KB_OFF, VB_OFF = 0, A_W, 2 * A_W, 2 * A_W + B_QK_W
KC_OFF = 2 * A_W + B_QK_W + B_W
KS_OFF = KC_OFF + 2 * C_KV_W


def _cparams(*sem):
    return pltpu.CompilerParams(dimension_semantics=sem, vmem_limit_bytes=VMEM_LIMIT_BYTES)


def _dot(a, b):
    return jnp.dot(a, b, preferred_element_type=F32)


def _dot_nt(a, b):
    return lax.dot_general(a, b, (((1,), (1,)), ((), ())), preferred_element_type=F32)


def _lane_iota(n):
    return lax.broadcasted_iota(jnp.int32, (1, n), 1)


def _in_range(x, lo, hi):
    return (x >= lo) & (x < hi)


def _online_update(s, m, l):
    m_new = jnp.maximum(m, jnp.max(s, axis=-1, keepdims=True))
    alpha = jnp.exp(m - m_new)
    p = jnp.exp(s - m_new)
    return p, alpha, m_new, alpha * l + jnp.sum(p, axis=-1, keepdims=True)


def _layer_norm(z, g, b):
    mu = jnp.mean(z, axis=-1, keepdims=True)
    zc = z - mu
    var = jnp.mean(zc * zc, axis=-1, keepdims=True)
    return zc * lax.rsqrt(var + LN_EPS) * g + b


def _split_hi_lo(x):
    hi = x.astype(MXU_DTYPE)
    lo = (x - hi.astype(F32)).astype(MXU_DTYPE)
    return hi, lo


def _inproj_kernel(x_ref, w_ref, qa_ref, qb_ref, qc_ref, gt_ref, row_ref, win_ref):
    x = x_ref[...].astype(MXU_DTYPE)
    qa_ref[...] = _dot(x, w_ref[:, QA_OFF:QB_OFF]).astype(qa_ref.dtype)
    qb_ref[...] = _dot(x, w_ref[:, QB_OFF:QC_OFF]).astype(qb_ref.dtype)
    qc_ref[...] = _dot(x, w_ref[:, QC_OFF:GT_OFF]).astype(qc_ref.dtype)
    gt_ref[...] = _dot(x, w_ref[:, GT_OFF:ROW_OFF])
    row_ref[...] = _dot(x, w_ref[:, ROW_OFF:WIN_OFF])
    win_ref[...] = _dot(x, w_ref[:, WIN_OFF:IN_W_PACKED])


def _inproj(x2, w_packed):
    n = x2.shape[0]
    tm = min(512, n)
    assert n % tm == 0
    widths = (A_W, B_QK_W, C_W, GT_PAD, KV_ROW, WIN_ROW)
    dtypes = (MXU_DTYPE, MXU_DTYPE, MXU_DTYPE, F32, F32, F32)
    return pl.pallas_call(
        _inproj_kernel,
        grid=(n // tm,),
        in_specs=[pl.BlockSpec((tm, D_MODEL), lambda i: (i, 0)),
                  pl.BlockSpec((D_MODEL, IN_W_PACKED), lambda i: (0, 0))],
        out_specs=[pl.BlockSpec((tm, w), lambda i: (i, 0)) for w in widths],
        out_shape=[jax.ShapeDtypeStruct((n, w), d) for w, d in zip(widths, dtypes)],
        compiler_params=_cparams("arbitrary"),
        name="inproj",
    )(x2, w_packed)


def _outproj_ln_kernel(oa_ref, ob_ref, oc_ref, x_ref, w_ref, g_ref, b_ref, y_ref, *, alpha):
    acc = _dot(oa_ref[...], w_ref[0:A_W, :])
    acc = acc + _dot(ob_ref[...], w_ref[A_W:A_W + B_W, :])
    acc = acc + _dot(oc_ref[...], w_ref[A_W + B_W:MIX_W, :])
    y_ref[...] = _layer_norm(alpha * x_ref[...] + acc, g_ref[...], b_ref[...])


def _outproj_ln(oa, ob, oc, x2, w_out_p, g, b, alpha):
    n = x2.shape[0]
    tm = min(512, n)
    assert n % tm == 0
    row = lambda w: pl.BlockSpec((tm, w), lambda i: (i, 0))
    const = lambda shape: pl.BlockSpec(shape, lambda i: (0, 0))
    return pl.pallas_call(
        functools.partial(_outproj_ln_kernel, alpha=alpha),
        grid=(n // tm,),
        in_specs=[row(A_W), row(B_W), row(C_W), row(D_MODEL), const((MIX_W, D_MODEL)),
                  const((1, D_MODEL)), const((1, D_MODEL))],
        out_specs=row(D_MODEL),
        out_shape=jax.ShapeDtypeStruct((n, D_MODEL), F32),
        compiler_params=_cparams("arbitrary"),
        name="outproj_ln",
    )(oa, ob, oc, x2, w_out_p, g, b)


ROUTER_LANES = LANES


def _route(lg):
    lane = _lane_iota(ROUTER_LANES)
    big = jnp.int32(ROUTER_LANES)
    is_g = lane < N_GROUPS
    lg1 = jnp.where(is_g, lg, -jnp.inf)
    m1 = jnp.max(lg1, axis=-1, keepdims=True)
    grp = jnp.min(jnp.where(lg1 == m1, lane, big), axis=-1, keepdims=True)
    pg = 1.0 / jnp.sum(jnp.where(is_g, jnp.exp(lg1 - m1), 0.0), axis=-1, keepdims=True)
    lo = N_GROUPS + grp * EXPERTS_PER_GROUP
    lg2 = jnp.where((lane >= lo) & (lane < lo + EXPERTS_PER_GROUP), lg, -jnp.inf)
    v1 = jnp.max(lg2, axis=-1, keepdims=True)
    i1 = jnp.min(jnp.where(lg2 == v1, lane, big), axis=-1, keepdims=True)
    lg2b = jnp.where(lane == i1, -jnp.inf, lg2)
    v2 = jnp.max(lg2b, axis=-1, keepdims=True)
    i2 = jnp.min(jnp.where(lg2b == v2, lane, big), axis=-1, keepdims=True)
    e2 = jnp.exp(v2 - v1)
    w1 = pg / (1.0 + e2)
    w2 = pg * e2 / (1.0 + e2)
    return jnp.where(lane == i1, w1, jnp.where(lane == i2, w2, 0.0)), grp


MOE_CHUNK = 256
MOE_EXPERTS_PER_STEP = 4


def _moe_kernel(x_ref, wr_ref, br_ref, w1_ref, w3_ref, w2_ref, g_ref, b_ref, y_ref,
                xs_ref, combs_ref, pos_ref, acc_ref, seg_ref, *, alpha):
    j = pl.program_id(1)
    tm = x_ref.shape[0]
    steps_per_group = EXPERTS_PER_GROUP // MOE_EXPERTS_PER_STEP
    lane = _lane_iota(ROUTER_LANES)
    rows = min(MOE_CHUNK, tm)
    shift = int(math.log2(rows))

    @pl.when(j == 0)
    def _():
        xh, xl = _split_hi_lo(x_ref[...])
        lg2 = _dot(xh, wr_ref[...])
        lg = lg2[:, 0:ROUTER_LANES] + lg2[:, ROUTER_LANES:] + _dot(xl, wr_ref[:, 0:ROUTER_LANES]) + br_ref[...]
        comb, grp = _route(lg)
        onehot = jnp.where(lane == grp, 1.0, 0.0)
        row = lax.broadcasted_iota(jnp.int32, (tm, tm), 0)
        col = lax.broadcasted_iota(jnp.int32, (tm, tm), 1)
        earlier = jnp.where(col < row, 1.0, 0.0).astype(MXU_DTYPE)
        rank = _dot(earlier, onehot.astype(MXU_DTYPE))
        cnt = jnp.sum(onehot, axis=0, keepdims=True)
        off = jnp.zeros_like(cnt)
        for g in range(1, N_GROUPS):
            off = off + jnp.where(lane >= g, cnt[:, g - 1:g], 0.0)
        pos = jnp.sum(jnp.where(lane == grp, off + rank, 0.0), axis=-1, keepdims=True)
        pos_ref[...] = jnp.broadcast_to(pos, pos_ref.shape)
        pos_row = pos_ref[...].T[0:1, :]
        sorted_row = lax.broadcasted_iota(jnp.int32, (tm, 1), 0).astype(F32)
        perm = jnp.where(pos_row == sorted_row, 1.0, 0.0).astype(MXU_DTYPE)
        xs_ref[...] = _dot(perm, xh).astype(xs_ref.dtype)
        cs = _dot(perm, jnp.concatenate(_split_hi_lo(comb), axis=1))
        combs_ref[...] = cs[:, 0:ROUTER_LANES] + cs[:, ROUTER_LANES:]
        acc_ref[...] = jnp.zeros_like(acc_ref)
        start = off.astype(jnp.int32)
        end = (off + cnt).astype(jnp.int32)
        for g in range(N_GROUPS):
            seg_ref[g] = start[0, g]
            seg_ref[N_GROUPS + g] = end[0, g]

    grp_id = j // steps_per_group
    lane0 = N_GROUPS + j * MOE_EXPERTS_PER_STEP
    first = jnp.right_shift(seg_ref[grp_id], shift)
    last = jnp.right_shift(seg_ref[N_GROUPS + grp_id] + (rows - 1), shift)

    def chunk(c, carry):
        r0 = pl.multiple_of(c * rows, rows)
        xc = xs_ref[pl.ds(r0, rows), :]
        cc = combs_ref[pl.ds(r0, rows), :]
        y = jnp.zeros((rows, D_MODEL), F32)
        for e in range(MOE_EXPERTS_PER_STEP):
            h1 = _dot(xc, w1_ref[e])
            h3 = _dot(xc, w3_ref[e])
            ce = jnp.sum(jnp.where(lane == lane0 + e, cc, 0.0), axis=-1, keepdims=True)
            hd = h1 * (1.0 / (1.0 + jnp.exp(-h1))) * h3 * ce
            y = y + _dot(hd.astype(MXU_DTYPE), w2_ref[e])
        acc_ref[pl.ds(r0, rows), :] += y
        return carry

    lax.fori_loop(first, last, chunk, 0)

    @pl.when(j == pl.num_programs(1) - 1)
    def _():
        col = lax.broadcasted_iota(jnp.int32, (1, tm), 1).astype(F32)
        unperm = jnp.where(pos_ref[:, 0:1] == col, 1.0, 0.0).astype(MXU_DTYPE)
        y_hi, y_lo = _split_hi_lo(acc_ref[...])
        y = _dot(unperm, y_hi) + _dot(unperm, y_lo)
        y_ref[...] = _layer_norm(alpha * x_ref[...] + y, g_ref[...], b_ref[...])


def _moe_ln(x2, wr_cat, br, w1, w3, w2, g, b, alpha):
    n = x2.shape[0]
    tm = min(1024, n)
    eps = MOE_EXPERTS_PER_STEP
    assert n % tm == 0 and tm % min(MOE_CHUNK, tm) == 0 and tm & (tm - 1) == 0 and EXPERTS_PER_GROUP % eps == 0
    const = lambda shape: pl.BlockSpec(shape, lambda i, j: (0,) * len(shape))
    return pl.pallas_call(
        functools.partial(_moe_kernel, alpha=alpha),
        grid=(n // tm, N_EXPERTS // eps),
        in_specs=[pl.BlockSpec((tm, D_MODEL), lambda i, j: (i, 0)),
                  const((D_MODEL, 2 * ROUTER_LANES)), const((1, ROUTER_LANES)),
                  pl.BlockSpec((eps, D_MODEL, EXPERT_FF), lambda i, j: (j, 0, 0)),
                  pl.BlockSpec((eps, D_MODEL, EXPERT_FF), lambda i, j: (j, 0, 0)),
                  pl.BlockSpec((eps, EXPERT_FF, D_MODEL), lambda i, j: (j, 0, 0)),
                  const((1, D_MODEL)), const((1, D_MODEL))],
        out_specs=pl.BlockSpec((tm, D_MODEL), lambda i, j: (i, 0)),
        out_shape=jax.ShapeDtypeStruct((n, D_MODEL), F32),
        scratch_shapes=[pltpu.VMEM((tm, D_MODEL), MXU_DTYPE), pltpu.VMEM((tm, ROUTER_LANES), F32),
                        pltpu.VMEM((tm, LANES), F32), pltpu.VMEM((tm, D_MODEL), F32),
                        pltpu.SMEM((2 * N_GROUPS,), jnp.int32)],
        compiler_params=_cparams("arbitrary", "arbitrary"),
        name="moe_ln",
    )(x2, wr_cat, br, w1, w3, w2, g, b)


def _gelu_tanh(x):
    return 0.5 * x * (1.0 + jnp.tanh(math.sqrt(2.0 / math.pi) * (x + 0.044715 * (x * x * x))))


def _compress_kernel(pt_ref, *refs, n_ops, op_rows):
    del pt_ref
    pages = (refs[:n_ops], refs[n_ops:2 * n_ops])
    pos_ref, wlo_ref, whi_ref, w2_ref, out_ref, carry_ref = refs[2 * n_ops:]
    cpo = op_rows // CMP_STRIDE
    m = n_ops * cpo

    @pl.when(pl.program_id(1) == 0)
    def _():
        carry_ref[...] = jnp.zeros_like(carry_ref)

    first = _lane_iota(C_KV_W) < HEAD_DIM
    row0 = lax.broadcasted_iota(jnp.int32, (m, 1), 0) == 0
    for kv in range(2):
        u = jnp.zeros((C_GROUPS * m, CMP_HIDDEN), F32)
        v = jnp.zeros((C_GROUPS * m, CMP_HIDDEN), F32)
        for quad in range(CMP_STRIDE // 4):
            lo, hi = [], []
            for pair in (2 * quad, 2 * quad + 1):
                x0, x1 = [jnp.concatenate([p[0, 0, pl.ds(t, cpo, stride=CMP_STRIDE), :] for p in pages[kv]], axis=0)
                          for t in (2 * pair, 2 * pair + 1)]
                r0, r1 = pltpu.roll(x0, HEAD_DIM, axis=1), pltpu.roll(x1, HEAD_DIM, axis=1)
                ab = jnp.concatenate([jnp.where(first, x0, r1), jnp.where(first, r0, x1)], axis=0)
                lo.append((ab + pos_ref[kv, 0, pair:pair + 1, :]).astype(MXU_DTYPE))
                hi.append((ab + pos_ref[kv, 1, pair:pair + 1, :]).astype(MXU_DTYPE))
            u = u + _dot(jnp.concatenate(lo, axis=1), wlo_ref[kv, quad])
            v = v + _dot(jnp.concatenate(hi, axis=1), whi_ref[kv, quad])
        out = jnp.zeros((m, C_KV_W), F32)
        for g in range(C_GROUPS):
            ug, vg = u[g * m:(g + 1) * m], v[g * m:(g + 1) * m]
            cs = slice((kv * C_GROUPS + g) * CMP_HIDDEN, (kv * C_GROUPS + g + 1) * CMP_HIDDEN)
            prev = jnp.where(row0, carry_ref[0:1, cs], pltpu.roll(ug, 1, axis=0))
            carry_ref[0:1, cs] = ug[m - 1:m, :]
            out = out + _dot(_gelu_tanh(prev + vg).astype(MXU_DTYPE), w2_ref[kv, g])
        out_ref[0, :, kv * C_KV_W:(kv + 1) * C_KV_W] = out


def _compress(cache, layer, page_table, col_block, cw, n_ops):
    pos2, wlo, whi, w2b = cw
    nb, n_pages = page_table.shape
    op_rows = cache.shape[2]
    assert n_pages % n_ops == 0
    steps = n_pages // n_ops
    cpo = op_rows // CMP_STRIDE
    kv_w = 2 * C_KV_W

    def page_map(b, s, pt, *, k, kv):
        return (layer, pt[b, s * n_ops + k], 0, col_block + kv)

    const = lambda shape: pl.BlockSpec(shape, lambda b, s, pt: (0,) * len(shape))
    grid_spec = pltpu.PrefetchScalarGridSpec(
        num_scalar_prefetch=1,
        grid=(nb, steps),
        in_specs=[pl.BlockSpec((1, 1, op_rows, C_KV_W), functools.partial(page_map, k=k, kv=kv))
                  for kv in range(2) for k in range(n_ops)]
        + [const(pos2.shape), const(wlo.shape), const(whi.shape), const(w2b.shape)],
        out_specs=pl.BlockSpec((1, n_ops * cpo, kv_w), lambda b, s, pt: (b, s, 0)),
        scratch_shapes=[pltpu.VMEM((SUBLANES, 2 * C_GROUPS * CMP_HIDDEN), F32)],
    )
    return pl.pallas_call(
        functools.partial(_compress_kernel, n_ops=n_ops, op_rows=op_rows),
        grid_spec=grid_spec,
        out_shape=jax.ShapeDtypeStruct((nb, n_pages * cpo, kv_w), F32),
        compiler_params=_cparams("arbitrary", "arbitrary"),
        name="nsa_compress",
    )(page_table, *([cache] * (2 * n_ops)), pos2, wlo, whi, w2b)


def _attend_t(qs, k_ref, vt_ref, acc_ref, v_rows, lo, qi, far_fn, own_fn):
    ns = qs.shape[0] // T
    w = (ns // len(v_rows)) * T
    acc_ref[...] = jnp.zeros_like(acc_ref)

    def tile(n, m, l, fn):
        off = pl.multiple_of(n * T, T)
        s_all = _dot_nt(k_ref[pl.ds(off, T), :], qs)
        s = jnp.concatenate([fn(c, s_all[:, c * T:(c + 1) * T]) for c in range(ns)], axis=1)
        m_new = jnp.maximum(m, jnp.max(s, axis=0, keepdims=True))
        alpha = jnp.exp(m - m_new)
        p = jnp.exp(s - m_new)
        l = alpha * l + jnp.sum(p, axis=0, keepdims=True)
        pb = p.astype(MXU_DTYPE)
        for gi, r0 in enumerate(v_rows):
            cs = slice(gi * w, (gi + 1) * w)
            acc_ref[gi] = acc_ref[gi] * alpha[:, cs] + _dot(vt_ref[n, r0:r0 + HEAD_DIM, :], pb[:, cs])
        return m_new, l

    init = (jnp.full((1, ns * T), NEG, F32), jnp.zeros((1, ns * T), F32))
    m, l = lax.fori_loop(lo, qi, lambda n, c: tile(n, c[0], c[1], far_fn(n)), init)
    _, l = tile(qi, m, l, own_fn)
    return [acc_ref[gi] / l[:, gi * w:(gi + 1) * w] for gi in range(len(v_rows))]


def _causal_penalty():
    j = lax.broadcasted_iota(jnp.int32, (T, T), 0)
    i = lax.broadcasted_iota(jnp.int32, (T, T), 1)
    return jnp.where(j <= i, 0.0, NEG)


def _cast_tiles(src_ref, col, width, dst_ref, seq):
    for n in range(seq // T):
        dst_ref[n * T:(n + 1) * T, :] = src_ref[0, n * T:(n + 1) * T, col:col + width].astype(dst_ref.dtype)


def _transpose_tiles(src_ref, col, width, dst_ref, seq):
    for n in range(seq // T):
        dst_ref[n] = src_ref[0, n * T:(n + 1) * T, col:col + width].T.astype(dst_ref.dtype)


def _rank_rows(score, n_rows, valid_fn=None):
    blk = lax.broadcasted_iota(jnp.int32, (score.shape[0], 1), 0)
    rank = jnp.zeros(score.shape, F32)
    for mb in range(n_rows):
        row = score[mb:mb + 1, :]
        beats = (row > score) | ((row == score) & (blk > mb))
        rank = rank + jnp.where(beats, 1.0 if valid_fn is None else valid_fn(mb), 0.0)
    return rank


def _moba_prompt_kernel(q_ref, kv_ref, bias_ref, o_ref, kb_ref, vt_ref, mean_ref, pen_ref, acc_ref, *, seq):
    qi = pl.program_id(1)
    nb = seq // MOBA_BLOCK

    @pl.when(qi == 0)
    def _():
        _cast_tiles(kv_ref, 0, A_W, kb_ref, seq)
        _transpose_tiles(kv_ref, A_W, A_W, vt_ref, seq)
        mean_ref[...] = jnp.zeros_like(mean_ref)
        for n in range(nb):
            mean_ref[n:n + 1, :] = jnp.mean(kv_ref[0, n * T:(n + 1) * T, 0:A_W], axis=0, keepdims=True)

    q = q_ref[0]
    lane_q = _lane_iota(A_W)
    qs = jnp.concatenate([jnp.where(_in_range(lane_q, h * HEAD_DIM, (h + 1) * HEAD_DIM), q, jnp.zeros_like(q))
                          for h in range(A_HEADS)], axis=0)
    gate = _dot_nt(mean_ref[...].astype(MXU_DTYPE), qs)
    blk = lax.broadcasted_iota(jnp.int32, (gate.shape[0], 1), 0)
    rank = _rank_rows(gate, nb, lambda mb: jnp.where(qi > mb, 1.0, 0.0))
    pen_ref[...] = jnp.where((blk < qi) & (rank < MOBA_TOPK), 0.0, NEG)
    causal = _causal_penalty()

    def far_fn(n):
        idx = jnp.minimum(qi - n, 2)
        return lambda c, s: s + bias_ref[c, idx] + pen_ref[pl.ds(n, 1), c * T:(c + 1) * T]

    outs = _attend_t(qs, kb_ref, vt_ref, acc_ref, [h * HEAD_DIM for h in range(A_HEADS)], 0, qi, far_fn,
                     lambda c, s: s + bias_ref[c, 0] + causal)
    o_ref[0] = jnp.concatenate(outs, axis=0).T.astype(o_ref.dtype)


def _moba_prompt(qa, row, bias_a):
    b, seq, _ = row.shape
    nb = seq // MOBA_BLOCK
    nbp = -(-nb // SUBLANES) * SUBLANES
    assert seq % T == 0 and T == MOBA_BLOCK and nb >= MOBA_TOPK
    return pl.pallas_call(
        functools.partial(_moba_prompt_kernel, seq=seq),
        grid=(b, seq // T),
        in_specs=[pl.BlockSpec((1, T, A_W), lambda i, j: (i, j, 0)),
                  pl.BlockSpec((1, seq, 2 * A_W), lambda i, j: (i, 0, KA_OFF // (2 * A_W))),
                  pl.BlockSpec(bias_a.shape, lambda i, j: (0, 0, 0, 0))],
        out_specs=pl.BlockSpec((1, T, A_W), lambda i, j: (i, j, 0)),
        out_shape=jax.ShapeDtypeStruct((b, seq, A_W), MXU_DTYPE),
        scratch_shapes=[pltpu.VMEM((seq, A_W), MXU_DTYPE), pltpu.VMEM((seq // T, A_W, T), MXU_DTYPE),
                        pltpu.VMEM((nbp, A_W), F32), pltpu.VMEM((nbp, A_HEADS * T), F32),
                        pltpu.VMEM((A_HEADS, HEAD_DIM, T), F32)],
        compiler_params=_cparams("arbitrary", "arbitrary"),
        name="moba_prompt",
    )(qa, row, bias_a)


def _diff_finish(o0, o1, lam, lane_v):
    o = o0 - lam * o1
    out = jnp.zeros_like(o)
    for h in range(B_HEADS):
        hm = _in_range(lane_v, h * B_V_DIM, (h + 1) * B_V_DIM)
        ms = jnp.sum(jnp.where(hm, o * o, 0.0), axis=-1, keepdims=True) * (1.0 / B_V_DIM)
        out = jnp.where(hm, o * lax.rsqrt(ms + LN_EPS), out)
    return out


def _diff_prompt_kernel(dp_ref, q_ref, kv_ref, bias_ref, subw_ref, o_ref, kb_ref, vt_ref, acc_ref, *, seq):
    qi = pl.program_id(1)

    @pl.when(qi == 0)
    def _():
        _cast_tiles(kv_ref, 0, B_QK_W, kb_ref, seq)
        _transpose_tiles(kv_ref, B_QK_W, B_W, vt_ref, seq)

    q = q_ref[0]
    lane_q = _lane_iota(B_QK_W)
    qs = jnp.concatenate([jnp.where(_in_range(lane_q, c * B_QK_DIM, (c + 1) * B_QK_DIM), q, jnp.zeros_like(q))
                          for c in range(2 * B_HEADS)], axis=0)
    causal = _causal_penalty()

    def far_fn(n):
        idx = jnp.minimum(qi - n, 2)
        return lambda c, s: s + bias_ref[c // 2, idx]

    outs = _attend_t(qs, kb_ref, vt_ref, acc_ref, [h * B_V_DIM for h in range(B_HEADS)], 0, qi, far_fn,
                     lambda c, s: s + bias_ref[c // 2, 0] + causal)
    lam = dp_ref[:, 0:1]
    heads = []
    for h in range(B_HEADS):
        o = outs[h][:, 0:T] - lam * outs[h][:, T:2 * T]
        ms = jnp.mean(o * o, axis=0, keepdims=True)
        heads.append(o * lax.rsqrt(ms + LN_EPS))
    out = jnp.concatenate(heads, axis=0) * subw_ref[...] * dp_ref[:, 1:2]
    o_ref[0] = out.T.astype(o_ref.dtype)


def _diff_prompt(dp, qb, row, bias_b, subw_t):
    b, seq, _ = row.shape
    kvw = B_QK_W + B_W
    assert B_V_DIM == HEAD_DIM
    return pl.pallas_call(
        functools.partial(_diff_prompt_kernel, seq=seq),
        grid=(b, seq // T),
        in_specs=[pl.BlockSpec((1, LANES), lambda i, j: (0, 0)),
                  pl.BlockSpec((1, T, B_QK_W), lambda i, j: (i, j, 0)),
                  pl.BlockSpec((1, seq, kvw), lambda i, j: (i, 0, KB_OFF // kvw)),
                  pl.BlockSpec(bias_b.shape, lambda i, j: (0, 0, 0, 0)),
                  pl.BlockSpec((B_W, T), lambda i, j: (0, 0))],
        out_specs=pl.BlockSpec((1, T, B_W), lambda i, j: (i, j, 0)),
        out_shape=jax.ShapeDtypeStruct((b, seq, B_W), MXU_DTYPE),
        scratch_shapes=[pltpu.VMEM((seq, B_QK_W), MXU_DTYPE), pltpu.VMEM((seq // T, B_W, T), MXU_DTYPE),
                        pltpu.VMEM((B_HEADS, B_V_DIM, 2 * T), F32)],
        compiler_params=_cparams("arbitrary", "arbitrary"),
        name="diff_prompt",
    )(dp, qb, row, bias_b, subw_t)


def _stack_group_queries(q, g):
    gm = _in_range(_lane_iota(C_KV_W), g * HEAD_DIM, (g + 1) * HEAD_DIM)
    q = q.astype(F32)
    return jnp.concatenate(
        [jnp.where(gm, q[:, hg * C_KV_W:(hg + 1) * C_KV_W], 0.0) for hg in range(C_HPG)], axis=0).astype(MXU_DTYPE)


def _stack_gate(gt, g, branch):
    cols = [gt[:, (g * C_HPG + hg) * N_BRANCH + branch:(g * C_HPG + hg) * N_BRANCH + branch + 1] for hg in range(C_HPG)]
    x = jnp.concatenate(cols, axis=0)
    return 1.0 / (1.0 + jnp.exp(-x))


def _cmp_attention(qs, kc, vc, q_pos_rows, n_cmp):
    nch = kc.shape[0]
    s = _dot_nt(qs, kc)
    r = _lane_iota(nch)
    ok = (r >= 1) & (r <= n_cmp) & ((r - 1) * CMP_STRIDE + (CMP_LEN - 1) <= q_pos_rows)
    s = jnp.where(ok, s, NEG)
    p = jnp.where(ok, jnp.exp(s - jnp.max(s, axis=-1, keepdims=True)), 0.0)
    l = jnp.sum(p, axis=-1, keepdims=True)
    p = p / jnp.where(l > 0.0, l, 1.0)
    return p, _dot(p.astype(MXU_DTYPE), vc)


def _nsa_prompt_kernel(q_ref, gt_ref, cmp_ref, kv_ref, win_ref, bias_ref, wsel_ref, o_ref,
                       ks_ref, vst_ref, kw_ref, vwt_ref, kc_ref, vct_ref, pen_ref, acc_ref, *, seq, n_cmp):
    qi = pl.program_id(1)
    nsel = seq // SEL_BLOCK
    bpt = T // SEL_BLOCK

    @pl.when(qi == 0)
    def _():
        _cast_tiles(kv_ref, 0, C_KV_W, ks_ref, seq)
        _transpose_tiles(kv_ref, C_KV_W, C_KV_W, vst_ref, seq)
        _cast_tiles(win_ref, 0, C_KV_W, kw_ref, seq)
        _transpose_tiles(win_ref, C_KV_W, C_KV_W, vwt_ref, seq)
        kc_ref[...] = cmp_ref[0, :, 0:C_KV_W].astype(kc_ref.dtype)
        vct_ref[...] = cmp_ref[0, :, C_KV_W:2 * C_KV_W].T.astype(vct_ref.dtype)

    q = q_ref[0]
    gates = 1.0 / (1.0 + jnp.exp(-gt_ref[0].T))
    q_pos = qi * T + _lane_iota(T)
    q_pos4 = jnp.concatenate([q_pos] * C_HPG, axis=1)
    nch = kc_ref.shape[0]
    r = lax.broadcasted_iota(jnp.int32, (nch, 1), 0)
    cmp_ok = (r >= 1) & (r <= n_cmp) & ((r - 1) * CMP_STRIDE + (CMP_LEN - 1) <= q_pos4)
    blk = lax.broadcasted_iota(jnp.int32, (pen_ref.shape[1], 1), 0)
    own = jnp.right_shift(q_pos, int(math.log2(SEL_BLOCK)))
    causal = _causal_penalty()
    jj = lax.broadcasted_iota(jnp.int32, (T, T), 0)
    ii = lax.broadcasted_iota(jnp.int32, (T, T), 1)
    qs = jnp.concatenate([_stack_group_queries(q, g) for g in range(C_GROUPS)], axis=0)
    vrows = [g * HEAD_DIM for g in range(C_GROUPS)]
    o_cmp = []
    for g in range(C_GROUPS):
        s = jnp.where(cmp_ok, _dot_nt(kc_ref[...], qs[g * C_HPG * T:(g + 1) * C_HPG * T]), NEG)
        p = jnp.where(cmp_ok, jnp.exp(s - jnp.max(s, axis=0, keepdims=True)), 0.0)
        l = jnp.sum(p, axis=0, keepdims=True)
        p = p / jnp.where(l > 0.0, l, 1.0)
        o_cmp.append(_dot(vct_ref[g * HEAD_DIM:(g + 1) * HEAD_DIM, :], p.astype(MXU_DTYPE)))
        pg = p[:, 0:T] + p[:, T:2 * T] + p[:, 2 * T:3 * T] + p[:, 3 * T:4 * T]
        ph, plo = _split_hi_lo(pg)
        p_slc = _dot(wsel_ref[...], ph) + _dot(wsel_ref[...], plo)
        forced = (blk == 0) | (blk == own) | (blk == own - 1)
        score = jnp.where(blk <= own, jnp.where(forced, jnp.inf, p_slc), -jnp.inf)
        rank = _rank_rows(score, nsel)
        pen_ref[g] = jnp.where((blk <= own) & (rank < SEL_TOPN), 0.0, NEG)

    def pen_tile(g, n):
        return jnp.concatenate([jnp.broadcast_to(pen_ref[g, pl.ds(n * bpt + b, 1), :], (SEL_BLOCK, T))
                                for b in range(bpt)], axis=0)

    def slc_far(n):
        idx = jnp.minimum(qi - n, 2)
        pens = [pen_tile(g, n) for g in range(C_GROUPS)]
        return lambda c, s: s + bias_ref[idx, c] + pens[c // C_HPG]

    own_pens = [pen_tile(g, qi) + causal for g in range(C_GROUPS)]
    o_slc = _attend_t(qs, ks_ref, vst_ref, acc_ref, vrows, 0, qi, slc_far,
                      lambda c, s: s + bias_ref[0, c] + own_pens[c // C_HPG])

    def win_far(n):
        d = qi - n
        idx = jnp.minimum(d, 2)
        pen = jnp.where(ii >= jj, jnp.where(d * T >= WINDOW, NEG, 0.0), 0.0)
        return lambda c, s: s + bias_ref[idx, c] + pen

    o_win = _attend_t(qs, kw_ref, vwt_ref, acc_ref, vrows, jnp.maximum(qi - WINDOW // T, 0), qi, win_far,
                      lambda c, s: s + bias_ref[0, c] + causal)

    def gate_row(g, branch):
        return jnp.concatenate([gates[(g * C_HPG + hg) * N_BRANCH + branch:(g * C_HPG + hg) * N_BRANCH + branch + 1, :]
                                for hg in range(C_HPG)], axis=1)

    outs = [gate_row(g, 0) * o_cmp[g] + gate_row(g, 1) * o_slc[g] + gate_row(g, 2) * o_win[g]
            for g in range(C_GROUPS)]
    out_t = jnp.concatenate([outs[g][:, hg * T:(hg + 1) * T] for hg in range(C_HPG) for g in range(C_GROUPS)], axis=0)
    o_ref[0] = out_t.T.astype(o_ref.dtype)


def _nsa_prompt(qc, gt, cmp, row, win, bias_c, wsel_t):
    b, seq, _ = row.shape
    nch = cmp.shape[1]
    n_cmp = (seq - CMP_LEN) // CMP_STRIDE + 1
    nselp = wsel_t.shape[0]
    assert seq // SEL_BLOCK >= SEL_TOPN and WINDOW == 2 * T and nselp >= seq // SEL_BLOCK
    kvw = 2 * C_KV_W
    nk = seq // T
    full = lambda a: pl.BlockSpec(a.shape, lambda i, j: (0,) * a.ndim)
    return pl.pallas_call(
        functools.partial(_nsa_prompt_kernel, seq=seq, n_cmp=n_cmp),
        grid=(b, nk),
        in_specs=[pl.BlockSpec((1, T, C_W), lambda i, j: (i, j, 0)),
                  pl.BlockSpec((1, T, GT_PAD), lambda i, j: (i, j, 0)),
                  pl.BlockSpec((1, nch, kvw), lambda i, j: (i, 0, 0)),
                  pl.BlockSpec((1, seq, kvw), lambda i, j: (i, 0, KS_OFF // kvw)),
                  pl.BlockSpec((1, seq, kvw), lambda i, j: (i, 0, 0)),
                  full(bias_c), full(wsel_t)],
        out_specs=pl.BlockSpec((1, T, C_W), lambda i, j: (i, j, 0)),
        out_shape=jax.ShapeDtypeStruct((b, seq, C_W), MXU_DTYPE),
        scratch_shapes=[pltpu.VMEM((seq, C_KV_W), MXU_DTYPE), pltpu.VMEM((nk, C_KV_W, T), MXU_DTYPE),
                        pltpu.VMEM((seq, C_KV_W), MXU_DTYPE), pltpu.VMEM((nk, C_KV_W, T), MXU_DTYPE),
                        pltpu.VMEM((nch, C_KV_W), MXU_DTYPE), pltpu.VMEM((C_KV_W, nch), MXU_DTYPE),
                        pltpu.VMEM((C_GROUPS, nselp, T), F32), pltpu.VMEM((C_GROUPS, HEAD_DIM, C_HPG * T), F32)],
        compiler_params=_cparams("arbitrary", "arbitrary"),
        name="nsa_prompt",
    )(qc, gt, cmp, row, win, bias_c, wsel_t)


PG = 16
PG_CMP = 32
NEW_PAD = LANES


def _rows_iota(n_rep, n):
    return jnp.concatenate([lax.broadcasted_iota(jnp.int32, (n, 1), 0)] * n_rep, axis=0)


def _pad_rows(x, rows):
    return jnp.concatenate([x, jnp.zeros((rows - x.shape[0], x.shape[1]), x.dtype)], axis=0)


def _page_specs(layer, width, col_block, n_ops):
    def page_map(b, s, pt, *, k):
        return (layer, pt[b, s * n_ops + k], 0, col_block)
    return [pl.BlockSpec((1, 1, PAGE_SIZE, width), functools.partial(page_map, k=k)) for k in range(n_ops)]


def _step_bias_spec(shape, steps):
    return pl.BlockSpec((1,) + shape[1:], lambda b, s, pt: (jnp.where(s == steps - 1, 1, 0),) + (0,) * (len(shape) - 1))


def _new_token_logits(qs, k_new, bias_new, nq):
    s = _dot_nt(qs, _pad_rows(k_new, NEW_PAD).astype(MXU_DTYPE)) + bias_new
    j = _lane_iota(NEW_PAD)
    i = _rows_iota(qs.shape[0] // nq, nq)
    return jnp.where((j < nq) & (j <= i), s, NEG)


def _nsa_sample_pre_kernel(q_ref, cmp_ref, state_ref, winnew_ref, bias_ref, wsel_ref, sel_ref, ocw_ref,
                           *, past, n_cmp, nq):
    q = q_ref[0]
    wb = state_ref.shape[2]
    wk = bias_ref.shape[-1]
    rows = C_HPG * nq
    kc = cmp_ref[0, :, 0:C_KV_W].astype(MXU_DTYPE)
    vc = cmp_ref[0, :, C_KV_W:2 * C_KV_W].astype(MXU_DTYPE)
    pad = jnp.zeros((wk - wb - nq, C_KV_W), F32)
    kw = jnp.concatenate([state_ref[0, 0, :, 0:C_KV_W], winnew_ref[0, :, 0:C_KV_W], pad], axis=0).astype(MXU_DTYPE)
    vw = jnp.concatenate([state_ref[0, 0, :, C_KV_W:], winnew_ref[0, :, C_KV_W:], pad], axis=0).astype(MXU_DTYPE)
    nsl = wsel_ref.shape[1]
    lane = _lane_iota(nsl)
    big = jnp.int32(nsl)
    i_q = lax.broadcasted_iota(jnp.int32, (nq, 1), 0)
    q_pos = past + i_q
    q_pos4 = past + _rows_iota(C_HPG, nq)
    for g in range(C_GROUPS):
        qs = _stack_group_queries(q, g)
        p, o_cmp = _cmp_attention(qs, kc, vc, q_pos4, n_cmp)
        pg = p[0:nq] + p[nq:2 * nq] + p[2 * nq:3 * nq] + p[3 * nq:4 * nq]
        ph, plo = _split_hi_lo(pg)
        p_slc = _dot(ph, wsel_ref[...]) + _dot(plo, wsel_ref[...])
        own = jnp.right_shift(q_pos, int(math.log2(SEL_BLOCK)))
        forced = (lane == 0) | (lane == own) | (lane == own - 1)
        score = jnp.where(lane <= own, jnp.where(forced, jnp.inf, p_slc), -jnp.inf)
        sel = jnp.zeros((nq, nsl), F32)
        for _ in range(SEL_TOPN):
            v = jnp.max(score, axis=-1, keepdims=True)
            idx = jnp.min(jnp.where(score == v, lane, big), axis=-1, keepdims=True)
            hit = lane == idx
            sel = jnp.where(hit & (v > -jnp.inf), 1.0, sel)
            score = jnp.where(hit, -jnp.inf, score)
        sel_ref[0, g * rows:(g + 1) * rows, :] = jnp.concatenate([sel] * C_HPG, axis=0)
        s = _dot_nt(qs, kw) +bias_ref[g * C_HPG:(g + 1) * C_HPG].reshape(rows, wk)
        j = _lane_iota(wk)
        rel = wb + _rows_iota(C_HPG, nq) - j
        ok = (j < wb + nq) & (rel >= 0) & (rel < WINDOW)
        s = jnp.where(ok, s, NEG)
        pw = jnp.exp(s - jnp.max(s, axis=-1, keepdims=True))
        o_win = _dot(pw.astype(MXU_DTYPE), vw) / jnp.sum(pw, axis=-1, keepdims=True)
        ocw_ref[0, g * rows:(g + 1) * rows, 0:C_KV_W] = o_cmp
        ocw_ref[0, g * rows:(g + 1) * rows, C_KV_W:] = o_win


def _nsa_sample_pre(qc, cmp, state_win, layer, win_new, bias_w, wsel, past):
    bs, nq, _ = qc.shape
    nch = cmp.shape[1]
    wb = state_win.shape[2]
    n_cmp = (past + nq - CMP_LEN) // CMP_STRIDE + 1
    rows = C_HEADS * nq
    full = lambda a: pl.BlockSpec(a.shape, lambda b: (0,) * a.ndim)
    return pl.pallas_call(
        functools.partial(_nsa_sample_pre_kernel, past=past, n_cmp=n_cmp, nq=nq),
        grid=(bs,),
        in_specs=[pl.BlockSpec((1, nq, C_W), lambda b: (b, 0, 0)),
                  pl.BlockSpec((1, nch, 2 * C_KV_W), lambda b: (b, 0, 0)),
                  pl.BlockSpec((1, 1, wb, WIN_ROW), lambda b: (layer, b, 0, 0)),
                  pl.BlockSpec((1, nq, WIN_ROW), lambda b: (b, 0, 0)),
                  full(bias_w), full(wsel)],
        out_specs=[pl.BlockSpec((1, rows, wsel.shape[1]), lambda b: (b, 0, 0)),
                   pl.BlockSpec((1, rows, 2 * C_KV_W), lambda b: (b, 0, 0))],
        out_shape=[jax.ShapeDtypeStruct((bs, rows, wsel.shape[1]), F32),
                   jax.ShapeDtypeStruct((bs, rows, 2 * C_KV_W), F32)],
        compiler_params=_cparams("arbitrary"),
        name="nsa_sample_pre",
    )(qc, cmp, state_win, win_new, bias_w, wsel)


def _nsa_sample_slc_kernel(pt_ref, *refs, nq):
    del pt_ref
    pages = refs[:PG]
    (q_ref, gt_ref, selw_ref, exp_ref, bias_ref, rownew_ref, biasnew_ref, ocw_ref,
     o_ref, m_ref, l_ref, acc_ref) = refs[PG:]
    rows = C_HPG * nq
    q = q_ref[0]

    def init():
        m_ref[...] = jnp.full_like(m_ref, NEG)
        l_ref[...] = jnp.zeros_like(l_ref)
        acc_ref[...] = jnp.zeros_like(acc_ref)

    def main():
        k = jnp.concatenate([p[0, 0, :, 0:C_KV_W] for p in pages], axis=0).astype(MXU_DTYPE)
        v = jnp.concatenate([p[0, 0, :, C_KV_W:] for p in pages], axis=0).astype(MXU_DTYPE)
        for g in range(C_GROUPS):
            rs = slice(g * rows, (g + 1) * rows)
            qs = _stack_group_queries(q, g)
            s = _dot_nt(qs, k) + bias_ref[0, g * C_HPG:(g + 1) * C_HPG].reshape(rows, PG * PAGE_SIZE)
            ok = _dot(selw_ref[0, 0, rs, :].astype(MXU_DTYPE), exp_ref[...]) > 0.5
            p, alpha, m, l = _online_update(jnp.where(ok, s, NEG), m_ref[rs], l_ref[rs])
            m_ref[rs] = m
            l_ref[rs] = l
            acc_ref[rs] = acc_ref[rs] * alpha + _dot(p.astype(MXU_DTYPE), v)

    def final():
        gt = gt_ref[0]
        k_new = rownew_ref[0, :, 0:C_KV_W]
        v_new = _pad_rows(rownew_ref[0, :, C_KV_W:], NEW_PAD).astype(MXU_DTYPE)
        outs = []
        for g in range(C_GROUPS):
            rs = slice(g * rows, (g + 1) * rows)
            qs = _stack_group_queries(q, g)
            s = _new_token_logits(qs, k_new, biasnew_ref[g * C_HPG:(g + 1) * C_HPG].reshape(rows, NEW_PAD), nq)
            p, alpha, _, l = _online_update(s, m_ref[rs], l_ref[rs])
            o_slc = (acc_ref[rs] * alpha + _dot(p.astype(MXU_DTYPE), v_new)) / l
            ocw = ocw_ref[0, rs, :]
            outs.append(_stack_gate(gt, g, 0) * ocw[:, 0:C_KV_W] + _stack_gate(gt, g, 1) * o_slc
                        + _stack_gate(gt, g, 2) * ocw[:, C_KV_W:])
        g0 = _lane_iota(C_KV_W) < HEAD_DIM
        for hg in range(C_HPG):
            chunk = jnp.where(g0, outs[0][hg * nq:(hg + 1) * nq], outs[1][hg * nq:(hg + 1) * nq])
            o_ref[0, :, hg * C_KV_W:(hg + 1) * C_KV_W] = chunk.astype(o_ref.dtype)

    return init, main, final


def _nsa_sample_slc(cache, layer, page_table, qc, gt, selw, expand, bias_steps, row_new, bias_new, ocw):
    bs, nq, _ = qc.shape
    steps = page_table.shape[1] // PG
    rows = C_HEADS * nq
    kvw = 2 * C_KV_W
    per_seq = lambda shape: pl.BlockSpec((1,) + shape, lambda b, s, pt: (b,) + (0,) * len(shape))
    full = lambda a: pl.BlockSpec(a.shape, lambda b, s, pt: (0,) * a.ndim)
    return dict(
        body=functools.partial(_nsa_sample_slc_kernel, nq=nq),
        in_specs=_page_specs(layer, kvw, KS_OFF // kvw, PG)
        + [per_seq((nq, C_W)), per_seq((nq, GT_PAD)),
           pl.BlockSpec((1, 1, rows, LANES), lambda b, s, pt: (b, s, 0, 0)),
           full(expand), _step_bias_spec(bias_steps.shape, steps),
           pl.BlockSpec((1, nq, kvw), lambda b, s, pt: (b, 0, KS_OFF // kvw)),
           full(bias_new), per_seq((rows, kvw))],
        args=[cache] * PG + [qc, gt, selw, expand, bias_steps, row_new, bias_new, ocw],
        out_spec=per_seq((nq, C_W)),
        out_shape=jax.ShapeDtypeStruct((bs, nq, C_W), F32),
        scratch=[pltpu.VMEM((rows, 1), F32), pltpu.VMEM((rows, 1), F32), pltpu.VMEM((rows, C_KV_W), F32)])


def _diff_sample_kernel(pt_ref, *refs, nq):
    del pt_ref
    pages = refs[:PG]
    dp_ref, q_ref, bias_ref, rownew_ref, biasnew_ref, subw_ref, o_ref, m_ref, l_ref, acc_ref = refs[PG:]

    def init():
        m_ref[...] = jnp.full_like(m_ref, NEG)
        l_ref[...] = jnp.zeros_like(l_ref)
        acc_ref[...] = jnp.zeros_like(acc_ref)

    q = q_ref[0]
    lane_q = _lane_iota(B_QK_W)
    qs = jnp.concatenate([jnp.where(_in_range(lane_q, c * B_QK_DIM, (c + 1) * B_QK_DIM), q, 0.0)
                          for c in range(2 * B_HEADS)], axis=0).astype(MXU_DTYPE)

    def head_rows(b):
        return jnp.concatenate([b[c // 2] for c in range(2 * B_HEADS)], axis=0)

    def main():
        k = jnp.concatenate([p[0, 0, :, 0:B_QK_W] for p in pages], axis=0).astype(MXU_DTYPE)
        v = jnp.concatenate([p[0, 0, :, B_QK_W:] for p in pages], axis=0).astype(MXU_DTYPE)
        s = _dot_nt(qs, k) + head_rows(bias_ref[0])
        p, alpha, m, l = _online_update(s, m_ref[...], l_ref[...])
        m_ref[...] = m
        l_ref[...] = l
        acc_ref[...] = acc_ref[...] * alpha + _dot(p.astype(MXU_DTYPE), v)

    def final():
        k_new = rownew_ref[0, :, 0:B_QK_W]
        v_new = _pad_rows(rownew_ref[0, :, B_QK_W:], NEW_PAD).astype(MXU_DTYPE)
        s = _new_token_logits(qs, k_new, head_rows(biasnew_ref[...]), nq)
        p, alpha, _, l = _online_update(s, m_ref[...], l_ref[...])
        o = (acc_ref[...] * alpha + _dot(p.astype(MXU_DTYPE), v_new)) / l
        lane_v = _lane_iota(B_W)
        maps = [jnp.zeros((nq, B_W), F32), jnp.zeros((nq, B_W), F32)]
        for c in range(2 * B_HEADS):
            hm = _in_range(lane_v, (c // 2) * B_V_DIM, (c // 2 + 1) * B_V_DIM)
            maps[c % 2] = jnp.where(hm, o[c * nq:(c + 1) * nq], maps[c % 2])
        out = _diff_finish(maps[0], maps[1], dp_ref[:, 0:1], lane_v) * subw_ref[...] * dp_ref[:, 1:2]
        o_ref[0] = out.astype(o_ref.dtype)

    return init, main, final


def _diff_sample(cache, layer, page_table, dp, qb, bias_steps, row_new, bias_new, subw):
    bs, nq, _ = qb.shape
    steps = page_table.shape[1] // PG
    rows = 2 * B_HEADS * nq
    kvw = B_QK_W + B_W
    per_seq = lambda shape: pl.BlockSpec((1,) + shape, lambda b, s, pt: (b,) + (0,) * len(shape))
    full = lambda a: pl.BlockSpec(a.shape, lambda b, s, pt: (0,) * a.ndim)
    return dict(
        body=functools.partial(_diff_sample_kernel, nq=nq),
        in_specs=_page_specs(layer, kvw, KB_OFF // kvw, PG)
        + [full(dp), per_seq((nq, B_QK_W)), _step_bias_spec(bias_steps.shape, steps),
           pl.BlockSpec((1, nq, kvw), lambda b, s, pt: (b, 0, KB_OFF // kvw)), full(bias_new), full(subw)],
        args=[cache] * PG + [dp, qb, bias_steps, row_new, bias_new, subw],
        out_spec=per_seq((nq, B_W)),
        out_shape=jax.ShapeDtypeStruct((bs, nq, B_W), F32),
        scratch=[pltpu.VMEM((rows, 1), F32), pltpu.VMEM((rows, 1), F32), pltpu.VMEM((rows, B_W), F32)])


def _moba_sample_kernel(pt_ref, *refs, nq, n_past_blocks):
    del pt_ref
    pages = refs[:PG]
    q_ref, bias_ref, rownew_ref, biasnew_ref, o_ref, oblk_ref, gate_ref, mst_ref, lst_ref = refs[PG:]
    step = pl.program_id(1)
    rows = A_HEADS * nq
    bps = PG * PAGE_SIZE // MOBA_BLOCK
    nbl = gate_ref.shape[1]
    lane_b = _lane_iota(nbl)
    lane_q = _lane_iota(A_W)
    q = q_ref[0]
    qf = jnp.concatenate([jnp.where(_in_range(lane_q, h * HEAD_DIM, (h + 1) * HEAD_DIM), q, 0.0)
                          for h in range(A_HEADS)], axis=0)
    qs = qf.astype(MXU_DTYPE)

    def head_rows(b):
        return jnp.concatenate([b[h] for h in range(A_HEADS)], axis=0)

    def init():
        gate_ref[...] = jnp.zeros_like(gate_ref)
        mst_ref[...] = jnp.zeros_like(mst_ref)
        lst_ref[...] = jnp.zeros_like(lst_ref)

    def main():
        kf = jnp.concatenate([p[0, 0, :, 0:A_W] for p in pages], axis=0)
        k = kf.astype(MXU_DTYPE)
        v = jnp.concatenate([p[0, 0, :, A_W:] for p in pages], axis=0).astype(MXU_DTYPE)
        s = _dot_nt(qs, k) + head_rows(bias_ref[0])
        gate, mst, lst = gate_ref[...], mst_ref[...], lst_ref[...]
        for j in range(bps):
            cs = slice(j * MOBA_BLOCK, (j + 1) * MOBA_BLOCK)
            sj = s[:, cs]
            mj = jnp.max(sj, axis=-1, keepdims=True)
            pj = jnp.exp(sj - mj)
            n = step * bps + j
            oblk_ref[n] = _dot(pj.astype(MXU_DTYPE), v[cs, :])
            mean = jnp.mean(kf[cs, :], axis=0, keepdims=True)
            hit = lane_b == n
            gate = jnp.where(hit, jnp.sum(qf * mean, axis=-1, keepdims=True), gate)
            mst = jnp.where(hit, mj, mst)
            lst = jnp.where(hit, jnp.sum(pj, axis=-1, keepdims=True), lst)
        gate_ref[...] = gate
        mst_ref[...] = mst
        lst_ref[...] = lst

    def final():
        gate, mst, lst = gate_ref[...], mst_ref[...], lst_ref[...]
        big = jnp.int32(nbl)
        score = jnp.where(lane_b < n_past_blocks, gate, -jnp.inf)
        sel = jnp.zeros((rows, nbl), F32)
        for _ in range(MOBA_TOPK):
            vmax = jnp.max(score, axis=-1, keepdims=True)
            idx = jnp.min(jnp.where(score == vmax, lane_b, big), axis=-1, keepdims=True)
            hit = lane_b == idx
            sel = jnp.where(hit & (vmax > -jnp.inf), 1.0, sel)
            score = jnp.where(hit, -jnp.inf, score)
        k_new = rownew_ref[0, :, 0:A_W]
        v_new = _pad_rows(rownew_ref[0, :, A_W:], NEW_PAD).astype(MXU_DTYPE)
        s_new = _new_token_logits(qs, k_new, head_rows(biasnew_ref[...]), nq)
        m_new = jnp.max(s_new, axis=-1, keepdims=True)
        p_new = jnp.exp(s_new - m_new)
        chosen = sel > 0.5
        m_all = jnp.maximum(m_new, jnp.max(jnp.where(chosen, mst, NEG), axis=-1, keepdims=True))
        w = jnp.where(chosen, jnp.exp(mst - m_all), 0.0)
        w_new = jnp.exp(m_new - m_all)
        l_all = jnp.sum(w * lst, axis=-1, keepdims=True) + w_new * jnp.sum(p_new, axis=-1, keepdims=True)
        o = w_new * _dot(p_new.astype(MXU_DTYPE), v_new)
        for n in range(n_past_blocks):
            o = o + w[:, n:n + 1] * oblk_ref[n]
        o = o / l_all
        out = jnp.zeros((nq, A_W), F32)
        for h in range(A_HEADS):
            out = jnp.where(_in_range(lane_q, h * HEAD_DIM, (h + 1) * HEAD_DIM), o[h * nq:(h + 1) * nq], out)
        o_ref[0] = out.astype(o_ref.dtype)

    return init, main, final


def _moba_sample(cache, layer, page_table, qa, bias_steps, row_new, bias_new, past):
    bs, nq, _ = qa.shape
    steps = page_table.shape[1] // PG
    rows = A_HEADS * nq
    kvw = 2 * A_W
    n_past_blocks = past // MOBA_BLOCK
    nbl = -(-n_past_blocks // LANES) * LANES
    per_seq = lambda shape: pl.BlockSpec((1,) + shape, lambda b, s, pt: (b,) + (0,) * len(shape))
    full = lambda a: pl.BlockSpec(a.shape, lambda b, s, pt: (0,) * a.ndim)
    return dict(
        body=functools.partial(_moba_sample_kernel, nq=nq, n_past_blocks=n_past_blocks),
        in_specs=_page_specs(layer, kvw, KA_OFF // kvw, PG)
        + [per_seq((nq, A_W)), _step_bias_spec(bias_steps.shape, steps),
           pl.BlockSpec((1, nq, kvw), lambda b, s, pt: (b, 0, KA_OFF // kvw)), full(bias_new)],
        args=[cache] * PG + [qa, bias_steps, row_new, bias_new],
        out_spec=per_seq((nq, A_W)),
        out_shape=jax.ShapeDtypeStruct((bs, nq, A_W), F32),
        scratch=[pltpu.VMEM((n_past_blocks, rows, A_W), F32)] + [pltpu.VMEM((rows, nbl), F32)] * 3)


def _sample_stream_kernel(pt_ref, *refs, parts):
    n_in = [len(p["in_specs"]) for p in parts]
    n_sc = [len(p["scratch"]) for p in parts]
    outs = refs[sum(n_in):sum(n_in) + len(parts)]
    i0, s0 = 0, sum(n_in) + len(parts)
    phases = []
    for k, p in enumerate(parts):
        phases.append(p["body"](pt_ref, *refs[i0:i0 + n_in[k]], outs[k], *refs[s0:s0 + n_sc[k]]))
        i0 += n_in[k]
        s0 += n_sc[k]
    step = pl.program_id(1)

    @pl.when(step == 0)
    def _():
        for init, _, _ in phases:
            init()

    for _, main, _ in phases:
        main()

    @pl.when(step == pl.num_programs(1) - 1)
    def _():
        for _, _, final in phases:
            final()


def _sample_stream(page_table, parts):
    bs = page_table.shape[0]
    steps = page_table.shape[1] // PG
    grid_spec = pltpu.PrefetchScalarGridSpec(
        num_scalar_prefetch=1,
        grid=(bs, steps),
        in_specs=[s for p in parts for s in p["in_specs"]],
        out_specs=[p["out_spec"] for p in parts],
        scratch_shapes=[s for p in parts for s in p["scratch"]],
    )
    return pl.pallas_call(
        functools.partial(_sample_stream_kernel, parts=parts),
        grid_spec=grid_spec,
        out_shape=[p["out_shape"] for p in parts],
        compiler_params=_cparams("arbitrary", "arbitrary"),
        name="sample_stream",
    )(page_table, *[a for p in parts for a in p["args"]])


def _t5_bucket(rel):
    n = jnp.maximum(rel, 0)
    nf = jnp.maximum(n, 1).astype(F32)
    large = MAX_EXACT + (jnp.log(nf / MAX_EXACT) / math.log(T5_MAX_DIST / MAX_EXACT)
                         * (N_BUCKETS - MAX_EXACT)).astype(jnp.int32)
    return jnp.where(n < MAX_EXACT, n, jnp.minimum(large, N_BUCKETS - 1))


def _nsa_perm():
    return np.array([g * C_HPG * HEAD_DIM + hg * HEAD_DIM + d
                     for hg in range(C_HPG) for g in range(C_GROUPS) for d in range(HEAD_DIM)], np.int32)


def _sel_weights(nch, n_cmp, nsl):
    r = np.arange(nch)[:, None]
    s = (r - 1) * CMP_STRIDE
    b0 = np.arange(nsl)[None, :] * SEL_BLOCK
    ov = np.clip(np.minimum(s + CMP_LEN, b0 + SEL_BLOCK) - np.maximum(s, b0), 0, None)
    ov = np.where((r >= 1) & (r <= n_cmp), ov, 0)
    return jnp.asarray(ov.astype(np.float32) / np.float32(CMP_STRIDE), MXU_DTYPE)


def _block_expand(n_tiles, lanes, keys_per_tile):
    key = np.arange(keys_per_tile)[None, None, :] + np.arange(n_tiles)[:, None, None] * keys_per_tile
    blk = np.arange(lanes)[None, :, None]
    return jnp.asarray((key // SEL_BLOCK == blk).astype(np.float32), MXU_DTYPE)


def _compress_weights(pos_k, pos_v, k_w1, k_w2, v_w1, v_w2):
    half_rows = CMP_STRIDE * HEAD_DIM

    def first_layer(w1, half):
        return w1[half * half_rows:(half + 1) * half_rows].reshape(CMP_STRIDE // 4, 4 * HEAD_DIM, CMP_HIDDEN)

    def second_layer(w2):
        z = jnp.zeros_like(w2)
        return jnp.stack([jnp.concatenate([w2, z], axis=1), jnp.concatenate([z, w2], axis=1)])

    def pair_pos(pos):
        return pos.astype(F32).reshape(2, CMP_STRIDE // 2, 2 * HEAD_DIM)

    pos2 = jnp.stack([pair_pos(pos_k), pair_pos(pos_v)])
    wlo = jnp.stack([first_layer(k_w1, 0), first_layer(v_w1, 0)]).astype(MXU_DTYPE)
    whi = jnp.stack([first_layer(k_w1, 1), first_layer(v_w1, 1)]).astype(MXU_DTYPE)
    w2b = jnp.stack([second_layer(k_w2), second_layer(v_w2)]).astype(MXU_DTYPE)
    return pos2, wlo, whi, w2b


def kernel(x_prompt, x_sample, cache_kv, state_win, page_table, rel_bias_table, w_in, w_out, lam_q1, lam_k1, lam_q2, lam_k2, diff_subln_w, cmp_pos_k, cmp_pos_v, cmp_k_w1, cmp_k_w2, cmp_v_w1, cmp_v_w2, ln_mix_g, ln_mix_b, ln_ffn_g, ln_ffn_b, router_w1, router_b1, router_w2, router_b2, expert_w1, expert_w3, expert_w2):
    depth = w_in.shape[0]
    bp, seq, _ = x_prompt.shape
    bs, nq, _ = x_sample.shape
    n_pages = page_table.shape[1]
    past = n_pages * PAGE_SIZE
    wb = state_win.shape[2]
    alpha = (2.0 * depth) ** 0.25
    assert cache_kv.shape[2] == PAGE_SIZE and nq < CMP_STRIDE and nq % SUBLANES == 0
    assert past % MOBA_BLOCK == 0 and n_pages % PG == 0 and PG * PAGE_SIZE // SEL_BLOCK <= LANES
    assert wb == WINDOW and wb <= past

    perm = _nsa_perm()
    tab = rel_bias_table.astype(F32)

    def toeplitz(rel):
        bucket = _t5_bucket(jnp.asarray(rel.astype(np.int32))).reshape(1, -1)
        onehot = (bucket == jnp.arange(N_BUCKETS, dtype=jnp.int32)[:, None]).astype(F32)
        out = jnp.dot(tab, onehot, precision=lax.Precision.HIGHEST)
        return out.reshape((tab.shape[0],) + rel.shape)

    assert T + 1 >= T5_MAX_DIST and PG * PAGE_SIZE >= T5_MAX_DIST
    ti = np.arange(T)[None, :] - np.arange(T)[:, None]
    bias_tiles = toeplitz(np.stack([ti + d * T for d in range(3)]))
    ha, hb = A_HEADS, A_HEADS + B_HEADS
    bias_a_p, bias_b_p = bias_tiles[:ha], bias_tiles[ha:hb]
    bias_c_p = jnp.swapaxes(bias_tiles[hb:], 0, 1)
    qi = np.arange(nq)[:, None]
    step_keys = np.arange(PG * PAGE_SIZE)[None, :]
    bias_steps = jnp.swapaxes(toeplitz(np.stack([2 * PG * PAGE_SIZE + qi - step_keys,
                                                 PG * PAGE_SIZE + qi - step_keys])), 0, 1)
    bias_new = toeplitz(qi - np.arange(NEW_PAD)[None, :])
    wk = -(-(wb + nq) // LANES) * LANES
    bias_win = toeplitz(wb + qi - np.arange(wk)[None, :])[hb:]

    nch_p = seq // CMP_STRIDE
    nselp = -(-(seq // SEL_BLOCK) // SUBLANES) * SUBLANES
    wsel_p = _sel_weights(nch_p, (seq - CMP_LEN) // CMP_STRIDE + 1, nselp).T
    nch_s = past // CMP_STRIDE
    nsel_s = past // SEL_BLOCK + 1
    steps = n_pages // PG
    bps = PG * PAGE_SIZE // SEL_BLOCK
    nsl = -(-max(nsel_s, steps * bps) // LANES) * LANES
    wsel_s = _sel_weights(nch_s, (past + nq - CMP_LEN) // CMP_STRIDE + 1, nsl)
    expand_s = _block_expand(1, LANES, PG * PAGE_SIZE)[0]
    pt_prompt = jnp.arange(bp, dtype=jnp.int32)[:, None]
    pg_cmp = math.gcd(PG_CMP, n_pages)

    xp = x_prompt.reshape(bp * seq, D_MODEL)
    xs = x_sample.reshape(bs * nq, D_MODEL)
    kv_p, win_p, kv_s, win_s = [], [], [], []
    for l in range(depth):
        w = w_in[l]
        w_packed = jnp.concatenate(
            [w[:, :A_W] * HEAD_DIM ** -0.5, w[:, A_W:A_W + B_QK_W] * B_QK_DIM ** -0.5,
             w[:, A_W + B_QK_W:A_W + B_QK_W + C_W][:, perm] * HEAD_DIM ** -0.5,
             jnp.pad(w[:, Q_W - GATE_W:Q_W], ((0, 0), (0, GT_PAD - GATE_W))), w[:, Q_W:]], axis=1).astype(MXU_DTYPE)
        wo = w_out[l]
        w_out_p = jnp.concatenate([wo[:A_W + B_W], wo[A_W + B_W:][perm]], axis=0).astype(MXU_DTYPE)
        lam_init = 0.8 - 0.6 * math.exp(-0.3 * l)
        lam = (jnp.exp(jnp.sum(lam_q1[l].astype(F32) * lam_k1[l].astype(F32)))
               - jnp.exp(jnp.sum(lam_q2[l].astype(F32) * lam_k2[l].astype(F32))) + lam_init)
        dp = jnp.zeros((1, LANES), F32).at[0, 0].set(lam).at[0, 1].set(1.0 - lam_init)
        subw = jnp.tile(diff_subln_w[l].astype(F32), B_HEADS)[None, :]
        cw = _compress_weights(cmp_pos_k[l], cmp_pos_v[l], cmp_k_w1[l], cmp_k_w2[l], cmp_v_w1[l], cmp_v_w2[l])
        wr = jnp.concatenate([router_w1[l], jnp.moveaxis(router_w2[l], 0, 1).reshape(D_MODEL, N_EXPERTS)], axis=1)
        wr = jnp.pad(wr.astype(F32), ((0, 0), (0, ROUTER_LANES - wr.shape[1])))
        wr_hi = wr.astype(MXU_DTYPE)
        wr_cat = jnp.concatenate([wr_hi, (wr - wr_hi.astype(F32)).astype(MXU_DTYPE)], axis=1)
        br = jnp.concatenate([router_b1[l], router_b2[l].reshape(-1)]).astype(F32)
        br = jnp.pad(br, (0, ROUTER_LANES - br.shape[0]))[None, :]
        ew1, ew3, ew2 = expert_w1[l].astype(MXU_DTYPE), expert_w3[l].astype(MXU_DTYPE), expert_w2[l].astype(MXU_DTYPE)
        g_mix, b_mix = ln_mix_g[l].astype(F32)[None, :], ln_mix_b[l].astype(F32)[None, :]
        g_ffn, b_ffn = ln_ffn_g[l].astype(F32)[None, :], ln_ffn_b[l].astype(F32)[None, :]

        qa, qb, qc, gt, row, win = _inproj(xp, w_packed)
        r3 = lambda a: a.reshape(bp, seq, a.shape[-1])
        row3, win3 = r3(row), r3(win)
        cmp_p = _compress(row3[None], 0, pt_prompt, KC_OFF // C_KV_W, cw, 1)
        oa = _moba_prompt(r3(qa), row3, bias_a_p)
        ob = _diff_prompt(dp, r3(qb), row3, bias_b_p, jnp.broadcast_to(subw.T, (B_W, T)))
        oc = _nsa_prompt(r3(qc), r3(gt), cmp_p, row3, win3, bias_c_p, wsel_p)
        f2 = lambda a: a.reshape(bp * seq, a.shape[-1])
        xp = _outproj_ln(f2(oa), f2(ob), f2(oc), xp, w_out_p, g_mix, b_mix, alpha)
        xp = _moe_ln(xp, wr_cat, br, ew1, ew3, ew2, g_ffn, b_ffn, alpha)
        kv_p.append(row3)
        win_p.append(win3[:, seq - min(WINDOW, seq):])

        qa, qb, qc, gt, row, win = _inproj(xs, w_packed)
        s3 = lambda a: a.reshape(bs, nq, a.shape[-1]).astype(F32)
        row3, win3 = s3(row), s3(win)
        cmp_s = _compress(cache_kv, l, page_table, KC_OFF // C_KV_W, cw, pg_cmp)
        sel, ocw = _nsa_sample_pre(s3(qc), cmp_s, state_win, l, win3, bias_win, wsel_s, past)
        selw = sel[:, :, :steps * bps].reshape(bs, C_HEADS * nq, steps, bps)
        selw = jnp.pad(jnp.moveaxis(selw, 2, 1), ((0, 0), (0, 0), (0, 0), (0, LANES - bps)))
        oc, ob, oa = _sample_stream(page_table, [
            _nsa_sample_slc(cache_kv, l, page_table, s3(qc), s3(gt), selw, expand_s, bias_steps[:, hb:], row3,
                            bias_new[hb:], ocw),
            _diff_sample(cache_kv, l, page_table, dp, s3(qb), bias_steps[:, ha:hb], row3, bias_new[ha:hb], subw),
            _moba_sample(cache_kv, l, page_table, s3(qa), bias_steps[:, :ha], row3, bias_new[:ha], past)])
        f2 = lambda a: a.reshape(bs * nq, a.shape[-1]).astype(MXU_DTYPE)
        xs = _outproj_ln(f2(oa), f2(ob), f2(oc), xs, w_out_p, g_mix, b_mix, alpha)
        xs = _moe_ln(xs, wr_cat, br, ew1, ew3, ew2, g_ffn, b_ffn, alpha)
        kv_s.append(row3)
        win_s.append(jnp.concatenate([state_win[l], win3.astype(state_win.dtype)], axis=1)[:, nq:])

    return (xp.reshape(bp, seq, D_MODEL), xs.reshape(bs, nq, D_MODEL),
            jnp.stack(kv_p), jnp.stack(win_p), jnp.stack(kv_s), jnp.stack(win_s))
```

```python
import functools
import math

import numpy as np
import jax
import jax.numpy as jnp
from jax import lax
from jax.experimental import pallas as pl
from jax.experimental.pallas import tpu as pltpu

D_MODEL = 1024
PAGE_SIZE = 128
HEAD_DIM = 64
A_HEADS = 4
MOBA_BLOCK = 256
MOBA_TOPK = 3
B_HEADS = 4
B_QK_DIM = 32
B_V_DIM = 2 * B_QK_DIM
C_HEADS = 8
C_GROUPS = 2
C_HPG = C_HEADS // C_GROUPS
CMP_LEN = 32
CMP_STRIDE = 16
CMP_HIDDEN = 256
SEL_BLOCK = 64
SEL_TOPN = 16
WINDOW = 512
N_BRANCH = 3
N_BUCKETS = 32
MAX_EXACT = N_BUCKETS // 2
T5_MAX_DIST = 128
A_W = A_HEADS * HEAD_DIM
B_QK_W = B_HEADS * 2 * B_QK_DIM
B_W = B_HEADS * B_V_DIM
C_W = C_HEADS * HEAD_DIM
C_KV_W = C_GROUPS * HEAD_DIM
MIX_W = A_W + B_W + C_W
GATE_W = C_HEADS * N_BRANCH
Q_W = A_W + B_QK_W + C_W + GATE_W
KV_ROW = 2 * A_W + B_QK_W + B_W + 4 * C_KV_W
WIN_ROW = 2 * C_KV_W
N_GROUPS = 4
EXPERTS_PER_GROUP = 8
N_EXPERTS = N_GROUPS * EXPERTS_PER_GROUP
EXPERT_FF = 256
LN_EPS = 1e-5
NEG = -1e30
F32 = jnp.float32
MXU_DTYPE = jnp.bfloat16

LANES = 128
SUBLANES = 8
VMEM_LIMIT_BYTES = 56 * 1024 * 1024

T = 256
GT_PAD = LANES
QA_OFF, QB_OFF, QC_OFF, GT_OFF = 0, A_W, A_W + B_QK_W, A_W + B_QK_W + C_W
ROW_OFF = GT_OFF + GT_PAD
WIN_OFF = ROW_OFF + KV_ROW
IN_W_PACKED = WIN_OFF + WIN_ROW
KA_OFF, VA_OFF, KB_OFF, VB_OFF = 0, A_W, 2 * A_W, 2 * A_W + B_QK_W
KC_OFF = 2 * A_W + B_QK_W + B_W
KS_OFF = KC_OFF + 2 * C_KV_W


def _cparams(*sem):
    return pltpu.CompilerParams(dimension_semantics=sem, vmem_limit_bytes=VMEM_LIMIT_BYTES)


def _dot(a, b):
    return jnp.dot(a, b, preferred_element_type=F32)


def _dot_nt(a, b):
    return lax.dot_general(a, b, (((1,), (1,)), ((), ())), preferred_element_type=F32)


def _lane_iota(n):
    return lax.broadcasted_iota(jnp.int32, (1, n), 1)


def _in_range(x, lo, hi):
    return (x >= lo) & (x < hi)


def _online_update(s, m, l):
    m_new = jnp.maximum(m, jnp.max(s, axis=-1, keepdims=True))
    alpha = jnp.exp(m - m_new)
    p = jnp.exp(s - m_new)
    return p, alpha, m_new, alpha * l + jnp.sum(p, axis=-1, keepdims=True)


def _layer_norm(z, g, b):
    mu = jnp.mean(z, axis=-1, keepdims=True)
    zc = z - mu
    var = jnp.mean(zc * zc, axis=-1, keepdims=True)
    return zc * lax.rsqrt(var + LN_EPS) * g + b


def _split_hi_lo(x):
    hi = x.astype(MXU_DTYPE)
    lo = (x - hi.astype(F32)).astype(MXU_DTYPE)
    return hi, lo


def _inproj_kernel(x_ref, w_ref, *refs):
    qa_ref, qb_ref, qc_ref, gt_ref, row_ref, win_ref = refs[-6:]
    x = x_ref[...].astype(MXU_DTYPE)
    qa_ref[...] = _dot(x, w_ref[:, QA_OFF:QB_OFF]).astype(qa_ref.dtype)
    qb_ref[...] = _dot(x, w_ref[:, QB_OFF:QC_OFF]).astype(qb_ref.dtype)
    qc_ref[...] = _dot(x, w_ref[:, QC_OFF:GT_OFF]).astype(qc_ref.dtype)
    gt_ref[...] = _dot(x, w_ref[:, GT_OFF:ROW_OFF])
    row_ref[...] = _dot(x, w_ref[:, ROW_OFF:WIN_OFF])
    win_ref[...] = _dot(x, w_ref[:, WIN_OFF:IN_W_PACKED])


ROW_OUT = 4


def _inproj(x2, w_packed, layer, depth, rows_buf=None):
    n = x2.shape[0]
    tm = min(512, n)
    assert n % tm == 0
    widths = (A_W, B_QK_W, C_W, GT_PAD, KV_ROW, WIN_ROW)
    dtypes = (MXU_DTYPE, MXU_DTYPE, MXU_DTYPE, F32, F32, F32)
    out_specs = [pl.BlockSpec((tm, w), lambda i: (i, 0)) for w in widths]
    out_shape = [jax.ShapeDtypeStruct((n, w), d) for w, d in zip(widths, dtypes)]
    out_specs[ROW_OUT] = pl.BlockSpec((None, tm, KV_ROW), lambda i: (layer, i, 0))
    out_shape[ROW_OUT] = jax.ShapeDtypeStruct((depth, n, KV_ROW), F32)
    in_specs = [pl.BlockSpec((tm, D_MODEL), lambda i: (i, 0)),
                pl.BlockSpec((D_MODEL, IN_W_PACKED), lambda i: (0, 0))]
    args, aliases = [x2, w_packed], {}
    if rows_buf is not None:
        in_specs.append(pl.BlockSpec(memory_space=pl.ANY))
        args.append(rows_buf)
        aliases = {2: ROW_OUT}
    return pl.pallas_call(
        _inproj_kernel,
        grid=(n // tm,),
        in_specs=in_specs,
        out_specs=out_specs,
        out_shape=out_shape,
        input_output_aliases=aliases,
        compiler_params=_cparams("arbitrary"),
        name="inproj",
    )(*args)


def _outproj_ln_kernel(oa_ref, ob_ref, oc_ref, x_ref, w_ref, g_ref, b_ref, y_ref, *, alpha):
    acc = _dot(oa_ref[...], w_ref[0:A_W, :])
    acc = acc + _dot(ob_ref[...], w_ref[A_W:A_W + B_W, :])
    acc = acc + _dot(oc_ref[...], w_ref[A_W + B_W:MIX_W, :])
    y_ref[...] = _layer_norm(alpha * x_ref[...] + acc, g_ref[...], b_ref[...])


def _outproj_ln(oa, ob, oc, x2, w_out_p, g, b, alpha):
    n = x2.shape[0]
    tm = min(512, n)
    assert n % tm == 0
    row = lambda w: pl.BlockSpec((tm, w), lambda i: (i, 0))
    const = lambda shape: pl.BlockSpec(shape, lambda i: (0, 0))
    return pl.pallas_call(
        functools.partial(_outproj_ln_kernel, alpha=alpha),
        grid=(n // tm,),
        in_specs=[row(A_W), row(B_W), row(C_W), row(D_MODEL), const((MIX_W, D_MODEL)),
                  const((1, D_MODEL)), const((1, D_MODEL))],
        out_specs=row(D_MODEL),
        out_shape=jax.ShapeDtypeStruct((n, D_MODEL), F32),
        compiler_params=_cparams("arbitrary"),
        name="outproj_ln",
    )(oa, ob, oc, x2, w_out_p, g, b)


ROUTER_LANES = LANES


def _route(lg):
    lane = _lane_iota(ROUTER_LANES)
    big = jnp.int32(ROUTER_LANES)
    is_g = lane < N_GROUPS
    lg1 = jnp.where(is_g, lg, -jnp.inf)
    m1 = jnp.max(lg1, axis=-1, keepdims=True)
    grp = jnp.min(jnp.where(lg1 == m1, lane, big), axis=-1, keepdims=True)
    pg = 1.0 / jnp.sum(jnp.where(is_g, jnp.exp(lg1 - m1), 0.0), axis=-1, keepdims=True)
    lo = N_GROUPS + grp * EXPERTS_PER_GROUP
    lg2 = jnp.where((lane >= lo) & (lane < lo + EXPERTS_PER_GROUP), lg, -jnp.inf)
    v1 = jnp.max(lg2, axis=-1, keepdims=True)
    i1 = jnp.min(jnp.where(lg2 == v1, lane, big), axis=-1, keepdims=True)
    lg2b = jnp.where(lane == i1, -jnp.inf, lg2)
    v2 = jnp.max(lg2b, axis=-1, keepdims=True)
    i2 = jnp.min(jnp.where(lg2b == v2, lane, big), axis=-1, keepdims=True)
    e2 = jnp.exp(v2 - v1)
    w1 = pg / (1.0 + e2)
    w2 = pg * e2 / (1.0 + e2)
    return jnp.where(lane == i1, w1, jnp.where(lane == i2, w2, 0.0)), grp


MOE_CHUNK = 256
MOE_EXPERTS_PER_STEP = 4


def _moe_kernel(x_ref, wr_ref, br_ref, w1_ref, w3_ref, w2_ref, g_ref, b_ref, y_ref,
                xs_ref, combs_ref, pos_ref, acc_ref, seg_ref, *, alpha):
    j = pl.program_id(1)
    tm = x_ref.shape[0]
    steps_per_group = EXPERTS_PER_GROUP // MOE_EXPERTS_PER_STEP
    lane = _lane_iota(ROUTER_LANES)
    rows = min(MOE_CHUNK, tm)
    shift = int(math.log2(rows))

    @pl.when(j == 0)
    def _():
        xh, xl = _split_hi_lo(x_ref[...])
        lg2 = _dot(xh, wr_ref[...])
        lg = lg2[:, 0:ROUTER_LANES] + lg2[:, ROUTER_LANES:] + _dot(xl, wr_ref[:, 0:ROUTER_LANES]) + br_ref[...]
        comb, grp = _route(lg)
        onehot = jnp.where(lane == grp, 1.0, 0.0)
        row = lax.broadcasted_iota(jnp.int32, (tm, tm), 0)
        col = lax.broadcasted_iota(jnp.int32, (tm, tm), 1)
        earlier = jnp.where(col < row, 1.0, 0.0).astype(MXU_DTYPE)
        rank = _dot(earlier, onehot.astype(MXU_DTYPE))
        cnt = jnp.sum(onehot, axis=0, keepdims=True)
        off = jnp.zeros_like(cnt)
        for g in range(1, N_GROUPS):
            off = off + jnp.where(lane >= g, cnt[:, g - 1:g], 0.0)
        pos = jnp.sum(jnp.where(lane == grp, off + rank, 0.0), axis=-1, keepdims=True)
        pos_ref[...] = jnp.broadcast_to(pos, pos_ref.shape)
        pos_row = pos_ref[...].T[0:1, :]
        sorted_row = lax.broadcasted_iota(jnp.int32, (tm, 1), 0).astype(F32)
        perm = jnp.where(pos_row == sorted_row, 1.0, 0.0).astype(MXU_DTYPE)
        xs_ref[...] = _dot(perm, xh).astype(xs_ref.dtype)
        cs = _dot(perm, jnp.concatenate(_split_hi_lo(comb), axis=1))
        combs_ref[...] = cs[:, 0:ROUTER_LANES] + cs[:, ROUTER_LANES:]
        acc_ref[...] = jnp.zeros_like(acc_ref)
        start = off.astype(jnp.int32)
        end = (off + cnt).astype(jnp.int32)
        for g in range(N_GROUPS):
            seg_ref[g] = start[0, g]
            seg_ref[N_GROUPS + g] = end[0, g]

    grp_id = j // steps_per_group
    lane0 = N_GROUPS + j * MOE_EXPERTS_PER_STEP
    first = jnp.right_shift(seg_ref[grp_id], shift)
    last = jnp.right_shift(seg_ref[N_GROUPS + grp_id] + (rows - 1), shift)

    def chunk(c, carry):
        r0 = pl.multiple_of(c * rows, rows)
        xc = xs_ref[pl.ds(r0, rows), :]
        cc = combs_ref[pl.ds(r0, rows), :]
        y = jnp.zeros((rows, D_MODEL), F32)
        for e in range(MOE_EXPERTS_PER_STEP):
            h1 = _dot(xc, w1_ref[e])
            h3 = _dot(xc, w3_ref[e])
            ce = jnp.sum(jnp.where(lane == lane0 + e, cc, 0.0), axis=-1, keepdims=True)
            hd = h1 * (1.0 / (1.0 + jnp.exp(-h1))) * h3 * ce
            y = y + _dot(hd.astype(MXU_DTYPE), w2_ref[e])
        acc_ref[pl.ds(r0, rows), :] += y
        return carry

    lax.fori_loop(first, last, chunk, 0)

    @pl.when(j == pl.num_programs(1) - 1)
    def _():
        col = lax.broadcasted_iota(jnp.int32, (1, tm), 1).astype(F32)
        unperm = jnp.where(pos_ref[:, 0:1] == col, 1.0, 0.0).astype(MXU_DTYPE)
        y_hi, y_lo = _split_hi_lo(acc_ref[...])
        y = _dot(unperm, y_hi) + _dot(unperm, y_lo)
        y_ref[...] = _layer_norm(alpha * x_ref[...] + y, g_ref[...], b_ref[...])


def _moe_ln(x2, wr_cat, br, w1, w3, w2, g, b, alpha):
    n = x2.shape[0]
    tm = min(1024, n)
    eps = MOE_EXPERTS_PER_STEP
    assert n % tm == 0 and tm % min(MOE_CHUNK, tm) == 0 and tm & (tm - 1) == 0 and EXPERTS_PER_GROUP % eps == 0
    const = lambda shape: pl.BlockSpec(shape, lambda i, j: (0,) * len(shape))
    return pl.pallas_call(
        functools.partial(_moe_kernel, alpha=alpha),
        grid=(n // tm, N_EXPERTS // eps),
        in_specs=[pl.BlockSpec((tm, D_MODEL), lambda i, j: (i, 0)),
                  const((D_MODEL, 2 * ROUTER_LANES)), const((1, ROUTER_LANES)),
                  pl.BlockSpec((eps, D_MODEL, EXPERT_FF), lambda i, j: (j, 0, 0)),
                  pl.BlockSpec((eps, D_MODEL, EXPERT_FF), lambda i, j: (j, 0, 0)),
                  pl.BlockSpec((eps, EXPERT_FF, D_MODEL), lambda i, j: (j, 0, 0)),
                  const((1, D_MODEL)), const((1, D_MODEL))],
        out_specs=pl.BlockSpec((tm, D_MODEL), lambda i, j: (i, 0)),
        out_shape=jax.ShapeDtypeStruct((n, D_MODEL), F32),
        scratch_shapes=[pltpu.VMEM((tm, D_MODEL), MXU_DTYPE), pltpu.VMEM((tm, ROUTER_LANES), F32),
                        pltpu.VMEM((tm, LANES), F32), pltpu.VMEM((tm, D_MODEL), F32),
                        pltpu.SMEM((2 * N_GROUPS,), jnp.int32)],
        compiler_params=_cparams("arbitrary", "arbitrary"),
        name="moe_ln",
    )(x2, wr_cat, br, w1, w3, w2, g, b)


def _gelu_tanh(x):
    return 0.5 * x * (1.0 + jnp.tanh(math.sqrt(2.0 / math.pi) * (x + 0.044715 * (x * x * x))))


def _compress_kernel(pt_ref, *refs, n_ops, op_rows):
    del pt_ref
    pages = (refs[:n_ops], refs[n_ops:2 * n_ops])
    pos_ref, wlo_ref, whi_ref, w2_ref, out_ref, carry_ref = refs[2 * n_ops:]
    cpo = op_rows // CMP_STRIDE
    m = n_ops * cpo

    @pl.when(pl.program_id(1) == 0)
    def _():
        carry_ref[...] = jnp.zeros_like(carry_ref)

    first = _lane_iota(C_KV_W) < HEAD_DIM
    row0 = lax.broadcasted_iota(jnp.int32, (m, 1), 0) == 0
    for kv in range(2):
        u = jnp.zeros((C_GROUPS * m, CMP_HIDDEN), F32)
        v = jnp.zeros((C_GROUPS * m, CMP_HIDDEN), F32)
        for quad in range(CMP_STRIDE // 4):
            lo, hi = [], []
            for pair in (2 * quad, 2 * quad + 1):
                x0, x1 = [jnp.concatenate([p[0, 0, pl.ds(t, cpo, stride=CMP_STRIDE), :] for p in pages[kv]], axis=0)
                          for t in (2 * pair, 2 * pair + 1)]
                r0, r1 = pltpu.roll(x0, HEAD_DIM, axis=1), pltpu.roll(x1, HEAD_DIM, axis=1)
                ab = jnp.concatenate([jnp.where(first, x0, r1), jnp.where(first, r0, x1)], axis=0)
                lo.append((ab + pos_ref[kv, 0, pair:pair + 1, :]).astype(MXU_DTYPE))
                hi.append((ab + pos_ref[kv, 1, pair:pair + 1, :]).astype(MXU_DTYPE))
            u = u + _dot(jnp.concatenate(lo, axis=1), wlo_ref[kv, quad])
            v = v + _dot(jnp.concatenate(hi, axis=1), whi_ref[kv, quad])
        out = jnp.zeros((m, C_KV_W), F32)
        for g in range(C_GROUPS):
            ug, vg = u[g * m:(g + 1) * m], v[g * m:(g + 1) * m]
            cs = slice((kv * C_GROUPS + g) * CMP_HIDDEN, (kv * C_GROUPS + g + 1) * CMP_HIDDEN)
            prev = jnp.where(row0, carry_ref[0:1, cs], pltpu.roll(ug, 1, axis=0))
            carry_ref[0:1, cs] = ug[m - 1:m, :]
            out = out + _dot(_gelu_tanh(prev + vg).astype(MXU_DTYPE), w2_ref[kv, g])
        out_ref[0, :, kv * C_KV_W:(kv + 1) * C_KV_W] = out


def _compress(cache, layer, page_table, col_block, cw, n_ops):
    pos2, wlo, whi, w2b = cw
    nb, n_pages = page_table.shape
    op_rows = cache.shape[2]
    assert n_pages % n_ops == 0
    steps = n_pages // n_ops
    cpo = op_rows // CMP_STRIDE
    kv_w = 2 * C_KV_W

    def page_map(b, s, pt, *, k, kv):
        return (layer, pt[b, s * n_ops + k], 0, col_block + kv)

    const = lambda shape: pl.BlockSpec(shape, lambda b, s, pt: (0,) * len(shape))
    grid_spec = pltpu.PrefetchScalarGridSpec(
        num_scalar_prefetch=1,
        grid=(nb, steps),
        in_specs=[pl.BlockSpec((1, 1, op_rows, C_KV_W), functools.partial(page_map, k=k, kv=kv))
                  for kv in range(2) for k in range(n_ops)]
        + [const(pos2.shape), const(wlo.shape), const(whi.shape), const(w2b.shape)],
        out_specs=pl.BlockSpec((1, n_ops * cpo, kv_w), lambda b, s, pt: (b, s, 0)),
        scratch_shapes=[pltpu.VMEM((SUBLANES, 2 * C_GROUPS * CMP_HIDDEN), F32)],
    )
    return pl.pallas_call(
        functools.partial(_compress_kernel, n_ops=n_ops, op_rows=op_rows),
        grid_spec=grid_spec,
        out_shape=jax.ShapeDtypeStruct((nb, n_pages * cpo, kv_w), F32),
        compiler_params=_cparams("arbitrary", "arbitrary"),
        name="nsa_compress",
    )(page_table, *([cache] * (2 * n_ops)), pos2, wlo, whi, w2b)


def _attend_t(qs, k_ref, vt_ref, acc_ref, v_rows, lo, qi, far_fn, own_fn):
    ns = qs.shape[0] // T
    w = (ns // len(v_rows)) * T
    acc_ref[...] = jnp.zeros_like(acc_ref)

    def tile(n, m, l, fn):
        off = pl.multiple_of(n * T, T)
        s_all = _dot_nt(k_ref[pl.ds(off, T), :], qs)
        s = jnp.concatenate([fn(c, s_all[:, c * T:(c + 1) * T]) for c in range(ns)], axis=1)
        m_new = jnp.maximum(m, jnp.max(s, axis=0, keepdims=True))
        alpha = jnp.exp(m - m_new)
        p = jnp.exp(s - m_new)
        l = alpha * l + jnp.sum(p, axis=0, keepdims=True)
        pb = p.astype(MXU_DTYPE)
        for gi, r0 in enumerate(v_rows):
            cs = slice(gi * w, (gi + 1) * w)
            acc_ref[gi] = acc_ref[gi] * alpha[:, cs] + _dot(vt_ref[n, r0:r0 + HEAD_DIM, :], pb[:, cs])
        return m_new, l

    init = (jnp.full((1, ns * T), NEG, F32), jnp.zeros((1, ns * T), F32))
    m, l = lax.fori_loop(lo, qi, lambda n, c: tile(n, c[0], c[1], far_fn(n)), init)
    _, l = tile(qi, m, l, own_fn)
    return [acc_ref[gi] / l[:, gi * w:(gi + 1) * w] for gi in range(len(v_rows))]


def _causal_penalty():
    j = lax.broadcasted_iota(jnp.int32, (T, T), 0)
    i = lax.broadcasted_iota(jnp.int32, (T, T), 1)
    return jnp.where(j <= i, 0.0, NEG)


def _cast_tiles(src_ref, col, width, dst_ref, seq):
    for n in range(seq // T):
        dst_ref[n * T:(n + 1) * T, :] = src_ref[0, n * T:(n + 1) * T, col:col + width].astype(dst_ref.dtype)


def _transpose_tiles(src_ref, col, width, dst_ref, seq):
    for n in range(seq // T):
        dst_ref[n] = src_ref[0, n * T:(n + 1) * T, col:col + width].T.astype(dst_ref.dtype)


def _rank_rows(score, n_rows, valid_fn=None):
    blk = lax.broadcasted_iota(jnp.int32, (score.shape[0], 1), 0)
    rank = jnp.zeros(score.shape, F32)
    for mb in range(n_rows):
        row = score[mb:mb + 1, :]
        beats = (row > score) | ((row == score) & (blk > mb))
        rank = rank + jnp.where(beats, 1.0 if valid_fn is None else valid_fn(mb), 0.0)
    return rank


def _moba_prompt_kernel(q_ref, kv_ref, bias_ref, o_ref, kb_ref, vt_ref, mean_ref, pen_ref, acc_ref, *, seq):
    qi = pl.program_id(1)
    nb = seq // MOBA_BLOCK

    @pl.when(qi == 0)
    def _():
        _cast_tiles(kv_ref, 0, A_W, kb_ref, seq)
        _transpose_tiles(kv_ref, A_W, A_W, vt_ref, seq)
        mean_ref[...] = jnp.zeros_like(mean_ref)
        for n in range(nb):
            mean_ref[n:n + 1, :] = jnp.mean(kv_ref[0, n * T:(n + 1) * T, 0:A_W], axis=0, keepdims=True)

    q = q_ref[0]
    lane_q = _lane_iota(A_W)
    qs = jnp.concatenate([jnp.where(_in_range(lane_q, h * HEAD_DIM, (h + 1) * HEAD_DIM), q, jnp.zeros_like(q))
                          for h in range(A_HEADS)], axis=0)
    gate = _dot_nt(mean_ref[...].astype(MXU_DTYPE), qs)
    blk = lax.broadcasted_iota(jnp.int32, (gate.shape[0], 1), 0)
    rank = _rank_rows(gate, nb, lambda mb: jnp.where(qi > mb, 1.0, 0.0))
    pen_ref[...] = jnp.where((blk < qi) & (rank < MOBA_TOPK), 0.0, NEG)
    causal = _causal_penalty()

    def far_fn(n):
        idx = jnp.minimum(qi - n, 2)
        return lambda c, s: s + bias_ref[c, idx] + pen_ref[pl.ds(n, 1), c * T:(c + 1) * T]

    outs = _attend_t(qs, kb_ref, vt_ref, acc_ref, [h * HEAD_DIM for h in range(A_HEADS)], 0, qi, far_fn,
                     lambda c, s: s + bias_ref[c, 0] + causal)
    o_ref[0] = jnp.concatenate(outs, axis=0).T.astype(o_ref.dtype)


def _moba_prompt(qa, row, layer, bias_a):
    _, b, seq, _ = row.shape
    nb = seq // MOBA_BLOCK
    nbp = -(-nb // SUBLANES) * SUBLANES
    assert seq % T == 0 and T == MOBA_BLOCK and nb >= MOBA_TOPK
    return pl.pallas_call(
        functools.partial(_moba_prompt_kernel, seq=seq),
        grid=(b, seq // T),
        in_specs=[pl.BlockSpec((1, T, A_W), lambda i, j: (i, j, 0)),
                  pl.BlockSpec((None, 1, seq, 2 * A_W), lambda i, j: (layer, i, 0, KA_OFF // (2 * A_W))),
                  pl.BlockSpec(bias_a.shape, lambda i, j: (0, 0, 0, 0))],
        out_specs=pl.BlockSpec((1, T, A_W), lambda i, j: (i, j, 0)),
        out_shape=jax.ShapeDtypeStruct((b, seq, A_W), MXU_DTYPE),
        scratch_shapes=[pltpu.VMEM((seq, A_W), MXU_DTYPE), pltpu.VMEM((seq // T, A_W, T), MXU_DTYPE),
                        pltpu.VMEM((nbp, A_W), F32), pltpu.VMEM((nbp, A_HEADS * T), F32),
                        pltpu.VMEM((A_HEADS, HEAD_DIM, T), F32)],
        compiler_params=_cparams("arbitrary", "arbitrary"),
        name="moba_prompt",
    )(qa, row, bias_a)


def _diff_finish(o0, o1, lam, lane_v):
    o = o0 - lam * o1
    out = jnp.zeros_like(o)
    for h in range(B_HEADS):
        hm = _in_range(lane_v, h * B_V_DIM, (h + 1) * B_V_DIM)
        ms = jnp.sum(jnp.where(hm, o * o, 0.0), axis=-1, keepdims=True) * (1.0 / B_V_DIM)
        out = jnp.where(hm, o * lax.rsqrt(ms + LN_EPS), out)
    return out


def _diff_prompt_kernel(dp_ref, q_ref, kv_ref, bias_ref, subw_ref, o_ref, kb_ref, vt_ref, acc_ref, *, seq):
    qi = pl.program_id(1)

    @pl.when(qi == 0)
    def _():
        _cast_tiles(kv_ref, 0, B_QK_W, kb_ref, seq)
        _transpose_tiles(kv_ref, B_QK_W, B_W, vt_ref, seq)

    q = q_ref[0]
    lane_q = _lane_iota(B_QK_W)
    qs = jnp.concatenate([jnp.where(_in_range(lane_q, c * B_QK_DIM, (c + 1) * B_QK_DIM), q, jnp.zeros_like(q))
                          for c in range(2 * B_HEADS)], axis=0)
    causal = _causal_penalty()

    def far_fn(n):
        idx = jnp.minimum(qi - n, 2)
        return lambda c, s: s + bias_ref[c // 2, idx]

    outs = _attend_t(qs, kb_ref, vt_ref, acc_ref, [h * B_V_DIM for h in range(B_HEADS)], 0, qi, far_fn,
                     lambda c, s: s + bias_ref[c // 2, 0] + causal)
    lam = dp_ref[:, 0:1]
    heads = []
    for h in range(B_HEADS):
        o = outs[h][:, 0:T] - lam * outs[h][:, T:2 * T]
        ms = jnp.mean(o * o, axis=0, keepdims=True)
        heads.append(o * lax.rsqrt(ms + LN_EPS))
    out = jnp.concatenate(heads, axis=0) * subw_ref[...] * dp_ref[:, 1:2]
    o_ref[0] = out.T.astype(o_ref.dtype)


def _diff_prompt(dp, qb, row, layer, bias_b, subw_t):
    _, b, seq, _ = row.shape
    kvw = B_QK_W + B_W
    assert B_V_DIM == HEAD_DIM
    return pl.pallas_call(
        functools.partial(_diff_prompt_kernel, seq=seq),
        grid=(b, seq // T),
        in_specs=[pl.BlockSpec((1, LANES), lambda i, j: (0, 0)),
                  pl.BlockSpec((1, T, B_QK_W), lambda i, j: (i, j, 0)),
                  pl.BlockSpec((None, 1, seq, kvw), lambda i, j: (layer, i, 0, KB_OFF // kvw)),
                  pl.BlockSpec(bias_b.shape, lambda i, j: (0, 0, 0, 0)),
                  pl.BlockSpec((B_W, T), lambda i, j: (0, 0))],
        out_specs=pl.BlockSpec((1, T, B_W), lambda i, j: (i, j, 0)),
        out_shape=jax.ShapeDtypeStruct((b, seq, B_W), MXU_DTYPE),
        scratch_shapes=[pltpu.VMEM((seq, B_QK_W), MXU_DTYPE), pltpu.VMEM((seq // T, B_W, T), MXU_DTYPE),
                        pltpu.VMEM((B_HEADS, B_V_DIM, 2 * T), F32)],
        compiler_params=_cparams("arbitrary", "arbitrary"),
        name="diff_prompt",
    )(dp, qb, row, bias_b, subw_t)


def _stack_group_queries(q, g):
    gm = _in_range(_lane_iota(C_KV_W), g * HEAD_DIM, (g + 1) * HEAD_DIM)
    q = q.astype(F32)
    return jnp.concatenate(
        [jnp.where(gm, q[:, hg * C_KV_W:(hg + 1) * C_KV_W], 0.0) for hg in range(C_HPG)], axis=0).astype(MXU_DTYPE)


def _stack_gate(gt, g, branch):
    cols = [gt[:, (g * C_HPG + hg) * N_BRANCH + branch:(g * C_HPG + hg) * N_BRANCH + branch + 1] for hg in range(C_HPG)]
    x = jnp.concatenate(cols, axis=0)
    return 1.0 / (1.0 + jnp.exp(-x))


def _cmp_attention(qs, kc, vc, q_pos_rows, n_cmp):
    nch = kc.shape[0]
    s = _dot_nt(qs, kc)
    r = _lane_iota(nch)
    ok = (r >= 1) & (r <= n_cmp) & ((r - 1) * CMP_STRIDE + (CMP_LEN - 1) <= q_pos_rows)
    s = jnp.where(ok, s, NEG)
    p = jnp.where(ok, jnp.exp(s - jnp.max(s, axis=-1, keepdims=True)), 0.0)
    l = jnp.sum(p, axis=-1, keepdims=True)
    p = p / jnp.where(l > 0.0, l, 1.0)
    return p, _dot(p.astype(MXU_DTYPE), vc)


def _nsa_prompt_kernel(q_ref, gt_ref, cmp_ref, kv_ref, win_ref, bias_ref, wsel_ref, o_ref,
                       ks_ref, vst_ref, kw_ref, vwt_ref, kc_ref, vct_ref, pen_ref, acc_ref, *, seq, n_cmp):
    qi = pl.program_id(1)
    nsel = seq // SEL_BLOCK
    bpt = T // SEL_BLOCK

    @pl.when(qi == 0)
    def _():
        _cast_tiles(kv_ref, 0, C_KV_W, ks_ref, seq)
        _transpose_tiles(kv_ref, C_KV_W, C_KV_W, vst_ref, seq)
        _cast_tiles(win_ref, 0, C_KV_W, kw_ref, seq)
        _transpose_tiles(win_ref, C_KV_W, C_KV_W, vwt_ref, seq)
        kc_ref[...] = cmp_ref[0, :, 0:C_KV_W].astype(kc_ref.dtype)
        vct_ref[...] = cmp_ref[0, :, C_KV_W:2 * C_KV_W].T.astype(vct_ref.dtype)

    q = q_ref[0]
    gates = 1.0 / (1.0 + jnp.exp(-gt_ref[0].T))
    q_pos = qi * T + _lane_iota(T)
    q_pos4 = jnp.concatenate([q_pos] * C_HPG, axis=1)
    nch = kc_ref.shape[0]
    r = lax.broadcasted_iota(jnp.int32, (nch, 1), 0)
    cmp_ok = (r >= 1) & (r <= n_cmp) & ((r - 1) * CMP_STRIDE + (CMP_LEN - 1) <= q_pos4)
    blk = lax.broadcasted_iota(jnp.int32, (pen_ref.shape[1], 1), 0)
    own = jnp.right_shift(q_pos, int(math.log2(SEL_BLOCK)))
    causal = _causal_penalty()
    jj = lax.broadcasted_iota(jnp.int32, (T, T), 0)
    ii = lax.broadcasted_iota(jnp.int32, (T, T), 1)
    qs = jnp.concatenate([_stack_group_queries(q, g) for g in range(C_GROUPS)], axis=0)
    vrows = [g * HEAD_DIM for g in range(C_GROUPS)]
    o_cmp = []
    for g in range(C_GROUPS):
        s = jnp.where(cmp_ok, _dot_nt(kc_ref[...], qs[g * C_HPG * T:(g + 1) * C_HPG * T]), NEG)
        p = jnp.where(cmp_ok, jnp.exp(s - jnp.max(s, axis=0, keepdims=True)), 0.0)
        l = jnp.sum(p, axis=0, keepdims=True)
        p = p / jnp.where(l > 0.0, l, 1.0)
        o_cmp.append(_dot(vct_ref[g * HEAD_DIM:(g + 1) * HEAD_DIM, :], p.astype(MXU_DTYPE)))
        pg = p[:, 0:T] + p[:, T:2 * T] + p[:, 2 * T:3 * T] + p[:, 3 * T:4 * T]
        ph, plo = _split_hi_lo(pg)
        p_slc = _dot(wsel_ref[...], ph) + _dot(wsel_ref[...], plo)
        forced = (blk == 0) | (blk == own) | (blk == own - 1)
        score = jnp.where(blk <= own, jnp.where(forced, jnp.inf, p_slc), -jnp.inf)
        rank = _rank_rows(score, nsel)
        pen_ref[g] = jnp.where((blk <= own) & (rank < SEL_TOPN), 0.0, NEG)

    def pen_tile(g, n):
        return jnp.concatenate([jnp.broadcast_to(pen_ref[g, pl.ds(n * bpt + b, 1), :], (SEL_BLOCK, T))
                                for b in range(bpt)], axis=0)

    def slc_far(n):
        idx = jnp.minimum(qi - n, 2)
        pens = [pen_tile(g, n) for g in range(C_GROUPS)]
        return lambda c, s: s + bias_ref[idx, c] + pens[c // C_HPG]

    own_pens = [pen_tile(g, qi) + causal for g in range(C_GROUPS)]
    o_slc = _attend_t(qs, ks_ref, vst_ref, acc_ref, vrows, 0, qi, slc_far,
                      lambda c, s: s + bias_ref[0, c] + own_pens[c // C_HPG])

    def win_far(n):
        d = qi - n
        idx = jnp.minimum(d, 2)
        pen = jnp.where(ii >= jj, jnp.where(d * T >= WINDOW, NEG, 0.0), 0.0)
        return lambda c, s: s + bias_ref[idx, c] + pen

    o_win = _attend_t(qs, kw_ref, vwt_ref, acc_ref, vrows, jnp.maximum(qi - WINDOW // T, 0), qi, win_far,
                      lambda c, s: s + bias_ref[0, c] + causal)

    def gate_row(g, branch):
        return jnp.concatenate([gates[(g * C_HPG + hg) * N_BRANCH + branch:(g * C_HPG + hg) * N_BRANCH + branch + 1, :]
                                for hg in range(C_HPG)], axis=1)

    outs = [gate_row(g, 0) * o_cmp[g] + gate_row(g, 1) * o_slc[g] + gate_row(g, 2) * o_win[g]
            for g in range(C_GROUPS)]
    out_t = jnp.concatenate([outs[g][:, hg * T:(hg + 1) * T] for hg in range(C_HPG) for g in range(C_GROUPS)], axis=0)
    o_ref[0] = out_t.T.astype(o_ref.dtype)


def _nsa_prompt(qc, gt, cmp, row, layer, win, bias_c, wsel_t):
    _, b, seq, _ = row.shape
    nch = cmp.shape[1]
    n_cmp = (seq - CMP_LEN) // CMP_STRIDE + 1
    nselp = wsel_t.shape[0]
    assert seq // SEL_BLOCK >= SEL_TOPN and WINDOW == 2 * T and nselp >= seq // SEL_BLOCK
    kvw = 2 * C_KV_W
    nk = seq // T
    full = lambda a: pl.BlockSpec(a.shape, lambda i, j: (0,) * a.ndim)
    return pl.pallas_call(
        functools.partial(_nsa_prompt_kernel, seq=seq, n_cmp=n_cmp),
        grid=(b, nk),
        in_specs=[pl.BlockSpec((1, T, C_W), lambda i, j: (i, j, 0)),
                  pl.BlockSpec((1, T, GT_PAD), lambda i, j: (i, j, 0)),
                  pl.BlockSpec((1, nch, kvw), lambda i, j: (i, 0, 0)),
                  pl.BlockSpec((None, 1, seq, kvw), lambda i, j: (layer, i, 0, KS_OFF // kvw)),
                  pl.BlockSpec((1, seq, kvw), lambda i, j: (i, 0, 0)),
                  full(bias_c), full(wsel_t)],
        out_specs=pl.BlockSpec((1, T, C_W), lambda i, j: (i, j, 0)),
        out_shape=jax.ShapeDtypeStruct((b, seq, C_W), MXU_DTYPE),
        scratch_shapes=[pltpu.VMEM((seq, C_KV_W), MXU_DTYPE), pltpu.VMEM((nk, C_KV_W, T), MXU_DTYPE),
                        pltpu.VMEM((seq, C_KV_W), MXU_DTYPE), pltpu.VMEM((nk, C_KV_W, T), MXU_DTYPE),
                        pltpu.VMEM((nch, C_KV_W), MXU_DTYPE), pltpu.VMEM((C_KV_W, nch), MXU_DTYPE),
                        pltpu.VMEM((C_GROUPS, nselp, T), F32), pltpu.VMEM((C_GROUPS, HEAD_DIM, C_HPG * T), F32)],
        compiler_params=_cparams("arbitrary", "arbitrary"),
        name="nsa_prompt",
    )(qc, gt, cmp, row, win, bias_c, wsel_t)


PG = 16
PG_CMP = 32
NEW_PAD = LANES


def _rows_iota(n_rep, n):
    return jnp.concatenate([lax.broadcasted_iota(jnp.int32, (n, 1), 0)] * n_rep, axis=0)


def _pad_rows(x, rows):
    return jnp.concatenate([x, jnp.zeros((rows - x.shape[0], x.shape[1]), x.dtype)], axis=0)


def _page_specs(layer, width, col_block, n_ops):
    def page_map(b, s, pt, *, k):
        return (layer, pt[b, s * n_ops + k], 0, col_block)
    return [pl.BlockSpec((1, 1, PAGE_SIZE, width), functools.partial(page_map, k=k)) for k in range(n_ops)]


def _step_bias_spec(shape, steps):
    return pl.BlockSpec((1,) + shape[1:], lambda b, s, pt: (jnp.where(s == steps - 1, 1, 0),) + (0,) * (len(shape) - 1))


def _new_token_logits(qs, k_new, bias_new, nq):
    s = _dot_nt(qs, _pad_rows(k_new, NEW_PAD).astype(MXU_DTYPE)) + bias_new
    j = _lane_iota(NEW_PAD)
    i = _rows_iota(qs.shape[0] // nq, nq)
    return jnp.where((j < nq) & (j <= i), s, NEG)


def _nsa_sample_pre_kernel(q_ref, cmp_ref, state_ref, winnew_ref, bias_ref, wsel_ref, sel_ref, ocw_ref,
                           *, past, n_cmp, nq):
    q = q_ref[0]
    wb = state_ref.shape[2]
    wk = bias_ref.shape[-1]
    rows = C_HPG * nq
    kc = cmp_ref[0, :, 0:C_KV_W].astype(MXU_DTYPE)
    vc = cmp_ref[0, :, C_KV_W:2 * C_KV_W].astype(MXU_DTYPE)
    pad = jnp.zeros((wk - wb - nq, C_KV_W), F32)
    kw = jnp.concatenate([state_ref[0, 0, :, 0:C_KV_W], winnew_ref[0, :, 0:C_KV_W], pad], axis=0).astype(MXU_DTYPE)
    vw = jnp.concatenate([state_ref[0, 0, :, C_KV_W:], winnew_ref[0, :, C_KV_W:], pad], axis=0).astype(MXU_DTYPE)
    nsl = wsel_ref.shape[1]
    lane = _lane_iota(nsl)
    big = jnp.int32(nsl)
    i_q = lax.broadcasted_iota(jnp.int32, (nq, 1), 0)
    q_pos = past + i_q
    q_pos4 = past + _rows_iota(C_HPG, nq)
    scores = []
    for g in range(C_GROUPS):
        qs = _stack_group_queries(q, g)
        p, o_cmp = _cmp_attention(qs, kc, vc, q_pos4, n_cmp)
        pg = p[0:nq] + p[nq:2 * nq] + p[2 * nq:3 * nq] + p[3 * nq:4 * nq]
        ph, plo = _split_hi_lo(pg)
        p_slc = _dot(ph, wsel_ref[...]) + _dot(plo, wsel_ref[...])
        own = jnp.right_shift(q_pos, int(math.log2(SEL_BLOCK)))
        forced = (lane == 0) | (lane == own) | (lane == own - 1)
        scores.append(jnp.where(lane <= own, jnp.where(forced, jnp.inf, p_slc), -jnp.inf))
        s = _dot_nt(qs, kw) +bias_ref[g * C_HPG:(g + 1) * C_HPG].reshape(rows, wk)
        j = _lane_iota(wk)
        rel = wb + _rows_iota(C_HPG, nq) - j
        ok = (j < wb + nq) & (rel >= 0) & (rel < WINDOW)
        s = jnp.where(ok, s, NEG)
        pw = jnp.exp(s - jnp.max(s, axis=-1, keepdims=True))
        o_win = _dot(pw.astype(MXU_DTYPE), vw) / jnp.sum(pw, axis=-1, keepdims=True)
        ocw_ref[0, g * rows:(g + 1) * rows, 0:C_KV_W] = o_cmp
        ocw_ref[0, g * rows:(g + 1) * rows, C_KV_W:] = o_win
    score = jnp.concatenate(scores, axis=0)
    sel = jnp.zeros(score.shape, F32)
    for _ in range(SEL_TOPN):
        v = jnp.max(score, axis=-1, keepdims=True)
        idx = jnp.min(jnp.where(score == v, lane, big), axis=-1, keepdims=True)
        hit = lane == idx
        sel = jnp.where(hit & (v > -jnp.inf), 1.0, sel)
        score = jnp.where(hit, -jnp.inf, score)
    for g in range(C_GROUPS):
        sel_ref[0, g * rows:(g + 1) * rows, :] = jnp.concatenate([sel[g * nq:(g + 1) * nq]] * C_HPG, axis=0)


def _nsa_sample_pre(qc, cmp, state_win, layer, win_new, bias_w, wsel, past):
    bs, nq, _ = qc.shape
    nch = cmp.shape[1]
    wb = state_win.shape[2]
    n_cmp = (past + nq - CMP_LEN) // CMP_STRIDE + 1
    rows = C_HEADS * nq
    full = lambda a: pl.BlockSpec(a.shape, lambda b: (0,) * a.ndim)
    return pl.pallas_call(
        functools.partial(_nsa_sample_pre_kernel, past=past, n_cmp=n_cmp, nq=nq),
        grid=(bs,),
        in_specs=[pl.BlockSpec((1, nq, C_W), lambda b: (b, 0, 0)),
                  pl.BlockSpec((1, nch, 2 * C_KV_W), lambda b: (b, 0, 0)),
                  pl.BlockSpec((1, 1, wb, WIN_ROW), lambda b: (layer, b, 0, 0)),
                  pl.BlockSpec((1, nq, WIN_ROW), lambda b: (b, 0, 0)),
                  full(bias_w), full(wsel)],
        out_specs=[pl.BlockSpec((1, rows, wsel.shape[1]), lambda b: (b, 0, 0)),
                   pl.BlockSpec((1, rows, 2 * C_KV_W), lambda b: (b, 0, 0))],
        out_shape=[jax.ShapeDtypeStruct((bs, rows, wsel.shape[1]), F32),
                   jax.ShapeDtypeStruct((bs, rows, 2 * C_KV_W), F32)],
        compiler_params=_cparams("arbitrary"),
        name="nsa_sample_pre",
    )(qc, cmp, state_win, win_new, bias_w, wsel)


def _nsa_sample_slc_kernel(pt_ref, *refs, nq):
    del pt_ref
    pages = refs[:PG]
    (q_ref, gt_ref, selw_ref, exp_ref, bias_ref, rownew_ref, biasnew_ref, ocw_ref,
     o_ref, m_ref, l_ref, acc_ref) = refs[PG:]
    rows = C_HPG * nq
    q = q_ref[0]

    def init():
        m_ref[...] = jnp.full_like(m_ref, NEG)
        l_ref[...] = jnp.zeros_like(l_ref)
        acc_ref[...] = jnp.zeros_like(acc_ref)

    def main():
        k = jnp.concatenate([p[0, 0, :, 0:C_KV_W] for p in pages], axis=0).astype(MXU_DTYPE)
        v = jnp.concatenate([p[0, 0, :, C_KV_W:] for p in pages], axis=0).astype(MXU_DTYPE)
        for g in range(C_GROUPS):
            rs = slice(g * rows, (g + 1) * rows)
            qs = _stack_group_queries(q, g)
            s = _dot_nt(qs, k) + bias_ref[0, g * C_HPG:(g + 1) * C_HPG].reshape(rows, PG * PAGE_SIZE)
            ok = _dot(selw_ref[0, 0, rs, :].astype(MXU_DTYPE), exp_ref[...]) > 0.5
            p, alpha, m, l = _online_update(jnp.where(ok, s, NEG), m_ref[rs], l_ref[rs])
            m_ref[rs] = m
            l_ref[rs] = l
            acc_ref[rs] = acc_ref[rs] * alpha + _dot(p.astype(MXU_DTYPE), v)

    def final():
        gt = gt_ref[0]
        k_new = rownew_ref[0, :, 0:C_KV_W]
        v_new = _pad_rows(rownew_ref[0, :, C_KV_W:], NEW_PAD).astype(MXU_DTYPE)
        outs = []
        for g in range(C_GROUPS):
            rs = slice(g * rows, (g + 1) * rows)
            qs = _stack_group_queries(q, g)
            s = _new_token_logits(qs, k_new, biasnew_ref[g * C_HPG:(g + 1) * C_HPG].reshape(rows, NEW_PAD), nq)
            p, alpha, _, l = _online_update(s, m_ref[rs], l_ref[rs])
            o_slc = (acc_ref[rs] * alpha + _dot(p.astype(MXU_DTYPE), v_new)) / l
            ocw = ocw_ref[0, rs, :]
            outs.append(_stack_gate(gt, g, 0) * ocw[:, 0:C_KV_W] + _stack_gate(gt, g, 1) * o_slc
                        + _stack_gate(gt, g, 2) * ocw[:, C_KV_W:])
        g0 = _lane_iota(C_KV_W) < HEAD_DIM
        for hg in range(C_HPG):
            chunk = jnp.where(g0, outs[0][hg * nq:(hg + 1) * nq], outs[1][hg * nq:(hg + 1) * nq])
            o_ref[0, :, hg * C_KV_W:(hg + 1) * C_KV_W] = chunk.astype(o_ref.dtype)

    return init, main, final


def _nsa_sample_slc(cache, layer, page_table, qc, gt, selw, expand, bias_steps, row_new, bias_new, ocw):
    bs, nq, _ = qc.shape
    steps = page_table.shape[1] // PG
    rows = C_HEADS * nq
    kvw = 2 * C_KV_W
    per_seq = lambda shape: pl.BlockSpec((1,) + shape, lambda b, s, pt: (b,) + (0,) * len(shape))
    full = lambda a: pl.BlockSpec(a.shape, lambda b, s, pt: (0,) * a.ndim)
    return dict(
        body=functools.partial(_nsa_sample_slc_kernel, nq=nq),
        in_specs=_page_specs(layer, kvw, KS_OFF // kvw, PG)
        + [per_seq((nq, C_W)), per_seq((nq, GT_PAD)),
           pl.BlockSpec((1, 1, rows, LANES), lambda b, s, pt: (b, s, 0, 0)),
           full(expand), _step_bias_spec(bias_steps.shape, steps),
           pl.BlockSpec((1, nq, kvw), lambda b, s, pt: (b, 0, KS_OFF // kvw)),
           full(bias_new), per_seq((rows, kvw))],
        args=[cache] * PG + [qc, gt, selw, expand, bias_steps, row_new, bias_new, ocw],
        out_spec=per_seq((nq, C_W)),
        out_shape=jax.ShapeDtypeStruct((bs, nq, C_W), F32),
        scratch=[pltpu.VMEM((rows, 1), F32), pltpu.VMEM((rows, 1), F32), pltpu.VMEM((rows, C_KV_W), F32)])


def _diff_sample_kernel(pt_ref, *refs, nq):
    del pt_ref
    pages = refs[:PG]
    dp_ref, q_ref, bias_ref, rownew_ref, biasnew_ref, subw_ref, o_ref, m_ref, l_ref, acc_ref = refs[PG:]

    def init():
        m_ref[...] = jnp.full_like(m_ref, NEG)
        l_ref[...] = jnp.zeros_like(l_ref)
        acc_ref[...] = jnp.zeros_like(acc_ref)

    q = q_ref[0]
    lane_q = _lane_iota(B_QK_W)
    qs = jnp.concatenate([jnp.where(_in_range(lane_q, c * B_QK_DIM, (c + 1) * B_QK_DIM), q, 0.0)
                          for c in range(2 * B_HEADS)], axis=0).astype(MXU_DTYPE)

    def head_rows(b):
        return jnp.concatenate([b[c // 2] for c in range(2 * B_HEADS)], axis=0)

    def main():
        k = jnp.concatenate([p[0, 0, :, 0:B_QK_W] for p in pages], axis=0).astype(MXU_DTYPE)
        v = jnp.concatenate([p[0, 0, :, B_QK_W:] for p in pages], axis=0).astype(MXU_DTYPE)
        s = _dot_nt(qs, k) + head_rows(bias_ref[0])
        p, alpha, m, l = _online_update(s, m_ref[...], l_ref[...])
        m_ref[...] = m
        l_ref[...] = l
        acc_ref[...] = acc_ref[...] * alpha + _dot(p.astype(MXU_DTYPE), v)

    def final():
        k_new = rownew_ref[0, :, 0:B_QK_W]
        v_new = _pad_rows(rownew_ref[0, :, B_QK_W:], NEW_PAD).astype(MXU_DTYPE)
        s = _new_token_logits(qs, k_new, head_rows(biasnew_ref[...]), nq)
        p, alpha, _, l = _online_update(s, m_ref[...], l_ref[...])
        o = (acc_ref[...] * alpha + _dot(p.astype(MXU_DTYPE), v_new)) / l
        lane_v = _lane_iota(B_W)
        maps = [jnp.zeros((nq, B_W), F32), jnp.zeros((nq, B_W), F32)]
        for c in range(2 * B_HEADS):
            hm = _in_range(lane_v, (c // 2) * B_V_DIM, (c // 2 + 1) * B_V_DIM)
            maps[c % 2] = jnp.where(hm, o[c * nq:(c + 1) * nq], maps[c % 2])
        out = _diff_finish(maps[0], maps[1], dp_ref[:, 0:1], lane_v) * subw_ref[...] * dp_ref[:, 1:2]
        o_ref[0] = out.astype(o_ref.dtype)

    return init, main, final


def _diff_sample(cache, layer, page_table, dp, qb, bias_steps, row_new, bias_new, subw):
    bs, nq, _ = qb.shape
    steps = page_table.shape[1] // PG
    rows = 2 * B_HEADS * nq
    kvw = B_QK_W + B_W
    per_seq = lambda shape: pl.BlockSpec((1,) + shape, lambda b, s, pt: (b,) + (0,) * len(shape))
    full = lambda a: pl.BlockSpec(a.shape, lambda b, s, pt: (0,) * a.ndim)
    return dict(
        body=functools.partial(_diff_sample_kernel, nq=nq),
        in_specs=_page_specs(layer, kvw, KB_OFF // kvw, PG)
        + [full(dp), per_seq((nq, B_QK_W)), _step_bias_spec(bias_steps.shape, steps),
           pl.BlockSpec((1, nq, kvw), lambda b, s, pt: (b, 0, KB_OFF // kvw)), full(bias_new), full(subw)],
        args=[cache] * PG + [dp, qb, bias_steps, row_new, bias_new, subw],
        out_spec=per_seq((nq, B_W)),
        out_shape=jax.ShapeDtypeStruct((bs, nq, B_W), F32),
        scratch=[pltpu.VMEM((rows, 1), F32), pltpu.VMEM((rows, 1), F32), pltpu.VMEM((rows, B_W), F32)])


def _moba_sample_kernel(pt_ref, *refs, nq, n_past_blocks):
    del pt_ref
    pages = refs[:PG]
    q_ref, bias_ref, rownew_ref, biasnew_ref, o_ref, oblk_ref, gate_ref, mst_ref, lst_ref = refs[PG:]
    step = pl.program_id(1)
    rows = A_HEADS * nq
    bps = PG * PAGE_SIZE // MOBA_BLOCK
    nbl = gate_ref.shape[1]
    lane_b = _lane_iota(nbl)
    lane_q = _lane_iota(A_W)
    q = q_ref[0]
    qf = jnp.concatenate([jnp.where(_in_range(lane_q, h * HEAD_DIM, (h + 1) * HEAD_DIM), q, 0.0)
                          for h in range(A_HEADS)], axis=0)
    qs = qf.astype(MXU_DTYPE)

    def head_rows(b):
        return jnp.concatenate([b[h] for h in range(A_HEADS)], axis=0)

    def init():
        gate_ref[...] = jnp.zeros_like(gate_ref)
        mst_ref[...] = jnp.zeros_like(mst_ref)
        lst_ref[...] = jnp.zeros_like(lst_ref)

    def main():
        kf = jnp.concatenate([p[0, 0, :, 0:A_W] for p in pages], axis=0)
        k = kf.astype(MXU_DTYPE)
        v = jnp.concatenate([p[0, 0, :, A_W:] for p in pages], axis=0).astype(MXU_DTYPE)
        s = _dot_nt(qs, k) + head_rows(bias_ref[0])
        gate, mst, lst = gate_ref[...], mst_ref[...], lst_ref[...]
        for j in range(bps):
            cs = slice(j * MOBA_BLOCK, (j + 1) * MOBA_BLOCK)
            sj = s[:, cs]
            mj = jnp.max(sj, axis=-1, keepdims=True)
            pj = jnp.exp(sj - mj)
            n = step * bps + j
            oblk_ref[n] = _dot(pj.astype(MXU_DTYPE), v[cs, :])
            mean = jnp.mean(kf[cs, :], axis=0, keepdims=True)
            hit = lane_b == n
            gate = jnp.where(hit, jnp.sum(qf * mean, axis=-1, keepdims=True), gate)
            mst = jnp.where(hit, mj, mst)
            lst = jnp.where(hit, jnp.sum(pj, axis=-1, keepdims=True), lst)
        gate_ref[...] = gate
        mst_ref[...] = mst
        lst_ref[...] = lst

    def final():
        gate, mst, lst = gate_ref[...], mst_ref[...], lst_ref[...]
        big = jnp.int32(nbl)
        score = jnp.where(lane_b < n_past_blocks, gate, -jnp.inf)
        sel = jnp.zeros((rows, nbl), F32)
        for _ in range(MOBA_TOPK):
            vmax = jnp.max(score, axis=-1, keepdims=True)
            idx = jnp.min(jnp.where(score == vmax, lane_b, big), axis=-1, keepdims=True)
            hit = lane_b == idx
            sel = jnp.where(hit & (vmax > -jnp.inf), 1.0, sel)
            score = jnp.where(hit, -jnp.inf, score)
        k_new = rownew_ref[0, :, 0:A_W]
        v_new = _pad_rows(rownew_ref[0, :, A_W:], NEW_PAD).astype(MXU_DTYPE)
        s_new = _new_token_logits(qs, k_new, head_rows(biasnew_ref[...]), nq)
        m_new = jnp.max(s_new, axis=-1, keepdims=True)
        p_new = jnp.exp(s_new - m_new)
        chosen = sel > 0.5
        m_all = jnp.maximum(m_new, jnp.max(jnp.where(chosen, mst, NEG), axis=-1, keepdims=True))
        w = jnp.where(chosen, jnp.exp(mst - m_all), 0.0)
        w_new = jnp.exp(m_new - m_all)
        l_all = jnp.sum(w * lst, axis=-1, keepdims=True) + w_new * jnp.sum(p_new, axis=-1, keepdims=True)
        o = w_new * _dot(p_new.astype(MXU_DTYPE), v_new)
        for n in range(n_past_blocks):
            o = o + w[:, n:n + 1] * oblk_ref[n]
        o = o / l_all
        out = jnp.zeros((nq, A_W), F32)
        for h in range(A_HEADS):
            out = jnp.where(_in_range(lane_q, h * HEAD_DIM, (h + 1) * HEAD_DIM), o[h * nq:(h + 1) * nq], out)
        o_ref[0] = out.astype(o_ref.dtype)

    return init, main, final


def _moba_sample(cache, layer, page_table, qa, bias_steps, row_new, bias_new, past):
    bs, nq, _ = qa.shape
    steps = page_table.shape[1] // PG
    rows = A_HEADS * nq
    kvw = 2 * A_W
    n_past_blocks = past // MOBA_BLOCK
    nbl = -(-n_past_blocks // LANES) * LANES
    per_seq = lambda shape: pl.BlockSpec((1,) + shape, lambda b, s, pt: (b,) + (0,) * len(shape))
    full = lambda a: pl.BlockSpec(a.shape, lambda b, s, pt: (0,) * a.ndim)
    return dict(
        body=functools.partial(_moba_sample_kernel, nq=nq, n_past_blocks=n_past_blocks),
        in_specs=_page_specs(layer, kvw, KA_OFF // kvw, PG)
        + [per_seq((nq, A_W)), _step_bias_spec(bias_steps.shape, steps),
           pl.BlockSpec((1, nq, kvw), lambda b, s, pt: (b, 0, KA_OFF // kvw)), full(bias_new)],
        args=[cache] * PG + [qa, bias_steps, row_new, bias_new],
        out_spec=per_seq((nq, A_W)),
        out_shape=jax.ShapeDtypeStruct((bs, nq, A_W), F32),
        scratch=[pltpu.VMEM((n_past_blocks, rows, A_W), F32)] + [pltpu.VMEM((rows, nbl), F32)] * 3)


def _sample_stream_kernel(pt_ref, *refs, parts):
    n_in = [len(p["in_specs"]) for p in parts]
    n_sc = [len(p["scratch"]) for p in parts]
    outs = refs[sum(n_in):sum(n_in) + len(parts)]
    i0, s0 = 0, sum(n_in) + len(parts)
    phases = []
    for k, p in enumerate(parts):
        phases.append(p["body"](pt_ref, *refs[i0:i0 + n_in[k]], outs[k], *refs[s0:s0 + n_sc[k]]))
        i0 += n_in[k]
        s0 += n_sc[k]
    step = pl.program_id(1)

    @pl.when(step == 0)
    def _():
        for init, _, _ in phases:
            init()

    for _, main, _ in phases:
        main()

    @pl.when(step == pl.num_programs(1) - 1)
    def _():
        for _, _, final in phases:
            final()


def _sample_stream(page_table, parts):
    bs = page_table.shape[0]
    steps = page_table.shape[1] // PG
    grid_spec = pltpu.PrefetchScalarGridSpec(
        num_scalar_prefetch=1,
        grid=(bs, steps),
        in_specs=[s for p in parts for s in p["in_specs"]],
        out_specs=[p["out_spec"] for p in parts],
        scratch_shapes=[s for p in parts for s in p["scratch"]],
    )
    return pl.pallas_call(
        functools.partial(_sample_stream_kernel, parts=parts),
        grid_spec=grid_spec,
        out_shape=[p["out_shape"] for p in parts],
        compiler_params=_cparams("arbitrary", "arbitrary"),
        name="sample_stream",
    )(page_table, *[a for p in parts for a in p["args"]])


def _t5_bucket(rel):
    n = jnp.maximum(rel, 0)
    nf = jnp.maximum(n, 1).astype(F32)
    large = MAX_EXACT + (jnp.log(nf / MAX_EXACT) / math.log(T5_MAX_DIST / MAX_EXACT)
                         * (N_BUCKETS - MAX_EXACT)).astype(jnp.int32)
    return jnp.where(n < MAX_EXACT, n, jnp.minimum(large, N_BUCKETS - 1))


def _nsa_perm():
    return np.array([g * C_HPG * HEAD_DIM + hg * HEAD_DIM + d
                     for hg in range(C_HPG) for g in range(C_GROUPS) for d in range(HEAD_DIM)], np.int32)


def _sel_weights(nch, n_cmp, nsl):
    r = np.arange(nch)[:, None]
    s = (r - 1) * CMP_STRIDE
    b0 = np.arange(nsl)[None, :] * SEL_BLOCK
    ov = np.clip(np.minimum(s + CMP_LEN, b0 + SEL_BLOCK) - np.maximum(s, b0), 0, None)
    ov = np.where((r >= 1) & (r <= n_cmp), ov, 0)
    return jnp.asarray(ov.astype(np.float32) / np.float32(CMP_STRIDE), MXU_DTYPE)


def _block_expand(n_tiles, lanes, keys_per_tile):
    key = np.arange(keys_per_tile)[None, None, :] + np.arange(n_tiles)[:, None, None] * keys_per_tile
    blk = np.arange(lanes)[None, :, None]
    return jnp.asarray((key // SEL_BLOCK == blk).astype(np.float32), MXU_DTYPE)


def _compress_weights(pos_k, pos_v, k_w1, k_w2, v_w1, v_w2):
    half_rows = CMP_STRIDE * HEAD_DIM

    def first_layer(w1, half):
        return w1[half * half_rows:(half + 1) * half_rows].reshape(CMP_STRIDE // 4, 4 * HEAD_DIM, CMP_HIDDEN)

    def second_layer(w2):
        z = jnp.zeros_like(w2)
        return jnp.stack([jnp.concatenate([w2, z], axis=1), jnp.concatenate([z, w2], axis=1)])

    def pair_pos(pos):
        return pos.astype(F32).reshape(2, CMP_STRIDE // 2, 2 * HEAD_DIM)

    pos2 = jnp.stack([pair_pos(pos_k), pair_pos(pos_v)])
    wlo = jnp.stack([first_layer(k_w1, 0), first_layer(v_w1, 0)]).astype(MXU_DTYPE)
    whi = jnp.stack([first_layer(k_w1, 1), first_layer(v_w1, 1)]).astype(MXU_DTYPE)
    w2b = jnp.stack([second_layer(k_w2), second_layer(v_w2)]).astype(MXU_DTYPE)
    return pos2, wlo, whi, w2b


def kernel(x_prompt, x_sample, cache_kv, state_win, page_table, rel_bias_table, w_in, w_out, lam_q1, lam_k1, lam_q2, lam_k2, diff_subln_w, cmp_pos_k, cmp_pos_v, cmp_k_w1, cmp_k_w2, cmp_v_w1, cmp_v_w2, ln_mix_g, ln_mix_b, ln_ffn_g, ln_ffn_b, router_w1, router_b1, router_w2, router_b2, expert_w1, expert_w3, expert_w2):
    depth = w_in.shape[0]
    bp, seq, _ = x_prompt.shape
    bs, nq, _ = x_sample.shape
    n_pages = page_table.shape[1]
    past = n_pages * PAGE_SIZE
    wb = state_win.shape[2]
    alpha = (2.0 * depth) ** 0.25
    assert cache_kv.shape[2] == PAGE_SIZE and nq < CMP_STRIDE and nq % SUBLANES == 0
    assert past % MOBA_BLOCK == 0 and n_pages % PG == 0 and PG * PAGE_SIZE // SEL_BLOCK <= LANES
    assert wb == WINDOW and wb <= past

    perm = _nsa_perm()
    tab = rel_bias_table.astype(F32)

    def toeplitz(rel):
        bucket = _t5_bucket(jnp.asarray(rel.astype(np.int32))).reshape(1, -1)
        onehot = (bucket == jnp.arange(N_BUCKETS, dtype=jnp.int32)[:, None]).astype(F32)
        out = jnp.dot(tab, onehot, precision=lax.Precision.HIGHEST)
        return out.reshape((tab.shape[0],) + rel.shape)

    assert T + 1 >= T5_MAX_DIST and PG * PAGE_SIZE >= T5_MAX_DIST
    ti = np.arange(T)[None, :] - np.arange(T)[:, None]
    bias_tiles = toeplitz(np.stack([ti + d * T for d in range(3)]))
    ha, hb = A_HEADS, A_HEADS + B_HEADS
    bias_a_p, bias_b_p = bias_tiles[:ha], bias_tiles[ha:hb]
    bias_c_p = jnp.swapaxes(bias_tiles[hb:], 0, 1)
    qi = np.arange(nq)[:, None]
    step_keys = np.arange(PG * PAGE_SIZE)[None, :]
    bias_steps = jnp.swapaxes(toeplitz(np.stack([2 * PG * PAGE_SIZE + qi - step_keys,
                                                 PG * PAGE_SIZE + qi - step_keys])), 0, 1)
    bias_new = toeplitz(qi - np.arange(NEW_PAD)[None, :])
    wk = -(-(wb + nq) // LANES) * LANES
    bias_win = toeplitz(wb + qi - np.arange(wk)[None, :])[hb:]

    nch_p = seq // CMP_STRIDE
    nselp = -(-(seq // SEL_BLOCK) // SUBLANES) * SUBLANES
    wsel_p = _sel_weights(nch_p, (seq - CMP_LEN) // CMP_STRIDE + 1, nselp).T
    nch_s = past // CMP_STRIDE
    nsel_s = past // SEL_BLOCK + 1
    steps = n_pages // PG
    bps = PG * PAGE_SIZE // SEL_BLOCK
    nsl = -(-max(nsel_s, steps * bps) // LANES) * LANES
    wsel_s = _sel_weights(nch_s, (past + nq - CMP_LEN) // CMP_STRIDE + 1, nsl)
    expand_s = _block_expand(1, LANES, PG * PAGE_SIZE)[0]
    pt_prompt = jnp.arange(bp, dtype=jnp.int32)[:, None]
    pg_cmp = math.gcd(PG_CMP, n_pages)

    xp = x_prompt.reshape(bp * seq, D_MODEL)
    xs = x_sample.reshape(bs * nq, D_MODEL)
    kv_p, kv_s = None, None
    win_p, win_s = [], []
    for l in range(depth):
        w = w_in[l]
        w_packed = jnp.concatenate(
            [w[:, :A_W] * HEAD_DIM ** -0.5, w[:, A_W:A_W + B_QK_W] * B_QK_DIM ** -0.5,
             w[:, A_W + B_QK_W:A_W + B_QK_W + C_W][:, perm] * HEAD_DIM ** -0.5,
             jnp.pad(w[:, Q_W - GATE_W:Q_W], ((0, 0), (0, GT_PAD - GATE_W))), w[:, Q_W:]], axis=1).astype(MXU_DTYPE)
        wo = w_out[l]
        w_out_p = jnp.concatenate([wo[:A_W + B_W], wo[A_W + B_W:][perm]], axis=0).astype(MXU_DTYPE)
        lam_init = 0.8 - 0.6 * math.exp(-0.3 * l)
        lam = (jnp.exp(jnp.sum(lam_q1[l].astype(F32) * lam_k1[l].astype(F32)))
               - jnp.exp(jnp.sum(lam_q2[l].astype(F32) * lam_k2[l].astype(F32))) + lam_init)
        dp = jnp.zeros((1, LANES), F32).at[0, 0].set(lam).at[0, 1].set(1.0 - lam_init)
        subw = jnp.tile(diff_subln_w[l].astype(F32), B_HEADS)[None, :]
        cw = _compress_weights(cmp_pos_k[l], cmp_pos_v[l], cmp_k_w1[l], cmp_k_w2[l], cmp_v_w1[l], cmp_v_w2[l])
        wr = jnp.concatenate([router_w1[l], jnp.moveaxis(router_w2[l], 0, 1).reshape(D_MODEL, N_EXPERTS)], axis=1)
        wr = jnp.pad(wr.astype(F32), ((0, 0), (0, ROUTER_LANES - wr.shape[1])))
        wr_hi = wr.astype(MXU_DTYPE)
        wr_cat = jnp.concatenate([wr_hi, (wr - wr_hi.astype(F32)).astype(MXU_DTYPE)], axis=1)
        br = jnp.concatenate([router_b1[l], router_b2[l].reshape(-1)]).astype(F32)
        br = jnp.pad(br, (0, ROUTER_LANES - br.shape[0]))[None, :]
        ew1, ew3, ew2 = expert_w1[l].astype(MXU_DTYPE), expert_w3[l].astype(MXU_DTYPE), expert_w2[l].astype(MXU_DTYPE)
        g_mix, b_mix = ln_mix_g[l].astype(F32)[None, :], ln_mix_b[l].astype(F32)[None, :]
        g_ffn, b_ffn = ln_ffn_g[l].astype(F32)[None, :], ln_ffn_b[l].astype(F32)[None, :]

        qa, qb, qc, gt, kv_p, win = _inproj(xp, w_packed, l, depth, kv_p)
        r3 = lambda a: a.reshape(bp, seq, a.shape[-1])
        rows4, win3 = kv_p.reshape(depth, bp, seq, KV_ROW), r3(win)
        cmp_p = _compress(rows4, l, pt_prompt, KC_OFF // C_KV_W, cw, 1)
        oa = _moba_prompt(r3(qa), rows4, l, bias_a_p)
        ob = _diff_prompt(dp, r3(qb), rows4, l, bias_b_p, jnp.broadcast_to(subw.T, (B_W, T)))
        oc = _nsa_prompt(r3(qc), r3(gt), cmp_p, rows4, l, win3, bias_c_p, wsel_p)
        f2 = lambda a: a.reshape(bp * seq, a.shape[-1])
        xp = _outproj_ln(f2(oa), f2(ob), f2(oc), xp, w_out_p, g_mix, b_mix, alpha)
        xp = _moe_ln(xp, wr_cat, br, ew1, ew3, ew2, g_ffn, b_ffn, alpha)
        win_p.append(win3[:, seq - min(WINDOW, seq):])

        qa, qb, qc, gt, kv_s, win = _inproj(xs, w_packed, l, depth, kv_s)
        s3 = lambda a: a.reshape(bs, nq, a.shape[-1]).astype(F32)
        row3, win3 = s3(kv_s[l]), s3(win)
        cmp_s = _compress(cache_kv, l, page_table, KC_OFF // C_KV_W, cw, pg_cmp)
        sel, ocw = _nsa_sample_pre(s3(qc), cmp_s, state_win, l, win3, bias_win, wsel_s, past)
        selw = sel[:, :, :steps * bps].reshape(bs, C_HEADS * nq, steps, bps)
        selw = jnp.pad(jnp.moveaxis(selw, 2, 1), ((0, 0), (0, 0), (0, 0), (0, LANES - bps)))
        oc, ob, oa = _sample_stream(page_table, [
            _nsa_sample_slc(cache_kv, l, page_table, s3(qc), s3(gt), selw, expand_s, bias_steps[:, hb:], row3,
                            bias_new[hb:], ocw),
            _diff_sample(cache_kv, l, page_table, dp, s3(qb), bias_steps[:, ha:hb], row3, bias_new[ha:hb], subw),
            _moba_sample(cache_kv, l, page_table, s3(qa), bias_steps[:, :ha], row3, bias_new[:ha], past)])
        f2 = lambda a: a.reshape(bs * nq, a.shape[-1]).astype(MXU_DTYPE)
        xs = _outproj_ln(f2(oa), f2(ob), f2(oc), xs, w_out_p, g_mix, b_mix, alpha)
        xs = _moe_ln(xs, wr_cat, br, ew1, ew3, ew2, g_ffn, b_ffn, alpha)
        win_s.append(jnp.concatenate([state_win[l], win3.astype(state_win.dtype)], axis=1)[:, nq:])

    return (xp.reshape(bp, seq, D_MODEL), xs.reshape(bs, nq, D_MODEL),
            kv_p.reshape(depth, bp, seq, KV_ROW), jnp.stack(win_p), kv_s.reshape(depth, bs, nq, KV_ROW),
            jnp.stack(win_s))
```

```python
import functools
import math

import numpy as np
import jax
import jax.numpy as jnp
from jax import lax
from jax.experimental import pallas as pl
from jax.experimental.pallas import tpu as pltpu

D_MODEL = 1024
PAGE_SIZE = 128
HEAD_DIM = 64
A_HEADS = 4
MOBA_BLOCK = 256
MOBA_TOPK = 3
B_HEADS = 4
B_QK_DIM = 32
B_V_DIM = 2 * B_QK_DIM
C_HEADS = 8
C_GROUPS = 2
C_HPG = C_HEADS // C_GROUPS
CMP_LEN = 32
CMP_STRIDE = 16
CMP_HIDDEN = 256
SEL_BLOCK = 64
SEL_TOPN = 16
WINDOW = 512
N_BRANCH = 3
N_BUCKETS = 32
MAX_EXACT = N_BUCKETS // 2
T5_MAX_DIST = 128
A_W = A_HEADS * HEAD_DIM
B_QK_W = B_HEADS * 2 * B_QK_DIM
B_W = B_HEADS * B_V_DIM
C_W = C_HEADS * HEAD_DIM
C_KV_W = C_GROUPS * HEAD_DIM
MIX_W = A_W + B_W + C_W
GATE_W = C_HEADS * N_BRANCH
Q_W = A_W + B_QK_W + C_W + GATE_W
KV_ROW = 2 * A_W + B_QK_W + B_W + 4 * C_KV_W
WIN_ROW = 2 * C_KV_W
N_GROUPS = 4
EXPERTS_PER_GROUP = 8
N_EXPERTS = N_GROUPS * EXPERTS_PER_GROUP
EXPERT_FF = 256
LN_EPS = 1e-5
NEG = -1e30
F32 = jnp.float32
MXU_DTYPE = jnp.bfloat16

LANES = 128
SUBLANES = 8
VMEM_LIMIT_BYTES = 56 * 1024 * 1024

T = 256
GT_PAD = LANES
QA_OFF, QB_OFF, QC_OFF, GT_OFF = 0, A_W, A_W + B_QK_W, A_W + B_QK_W + C_W
ROW_OFF = GT_OFF + GT_PAD
WIN_OFF = ROW_OFF + KV_ROW
IN_W_PACKED = WIN_OFF + WIN_ROW
KA_OFF, VA_OFF, KB_OFF, VB_OFF = 0, A_W, 2 * A_W, 2 * A_W + B_QK_W
KC_OFF = 2 * A_W + B_QK_W + B_W
KS_OFF = KC_OFF + 2 * C_KV_W


def _cparams(*sem):
    return pltpu.CompilerParams(dimension_semantics=sem, vmem_limit_bytes=VMEM_LIMIT_BYTES)


def _dot(a, b):
    return jnp.dot(a, b, preferred_element_type=F32)


def _dot_nt(a, b):
    return lax.dot_general(a, b, (((1,), (1,)), ((), ())), preferred_element_type=F32)


def _lane_iota(n):
    return lax.broadcasted_iota(jnp.int32, (1, n), 1)


def _in_range(x, lo, hi):
    return (x >= lo) & (x < hi)


def _online_update(s, m, l):
    m_new = jnp.maximum(m, jnp.max(s, axis=-1, keepdims=True))
    alpha = jnp.exp(m - m_new)
    p = jnp.exp(s - m_new)
    return p, alpha, m_new, alpha * l + jnp.sum(p, axis=-1, keepdims=True)


def _layer_norm(z, g, b):
    mu = jnp.mean(z, axis=-1, keepdims=True)
    zc = z - mu
    var = jnp.mean(zc * zc, axis=-1, keepdims=True)
    return zc * lax.rsqrt(var + LN_EPS) * g + b


def _split_hi_lo(x):
    hi = x.astype(MXU_DTYPE)
    lo = (x - hi.astype(F32)).astype(MXU_DTYPE)
    return hi, lo


def _inproj_kernel(x_ref, w_ref, *refs):
    qa_ref, qb_ref, qc_ref, gt_ref, row_ref, win_ref = refs[-6:]
    x = x_ref[...].astype(MXU_DTYPE)
    qa_ref[...] = _dot(x, w_ref[:, QA_OFF:QB_OFF]).astype(qa_ref.dtype)
    qb_ref[...] = _dot(x, w_ref[:, QB_OFF:QC_OFF]).astype(qb_ref.dtype)
    qc_ref[...] = _dot(x, w_ref[:, QC_OFF:GT_OFF]).astype(qc_ref.dtype)
    gt_ref[...] = _dot(x, w_ref[:, GT_OFF:ROW_OFF])
    row_ref[...] = _dot(x, w_ref[:, ROW_OFF:WIN_OFF])
    win_ref[...] = _dot(x, w_ref[:, WIN_OFF:IN_W_PACKED])


ROW_OUT = 4


def _inproj(x2, w_packed, layer, depth, rows_buf=None):
    n = x2.shape[0]
    tm = min(512, n)
    assert n % tm == 0
    widths = (A_W, B_QK_W, C_W, GT_PAD, KV_ROW, WIN_ROW)
    dtypes = (MXU_DTYPE, MXU_DTYPE, MXU_DTYPE, F32, F32, F32)
    out_specs = [pl.BlockSpec((tm, w), lambda i: (i, 0)) for w in widths]
    out_shape = [jax.ShapeDtypeStruct((n, w), d) for w, d in zip(widths, dtypes)]
    out_specs[ROW_OUT] = pl.BlockSpec((None, tm, KV_ROW), lambda i: (layer, i, 0))
    out_shape[ROW_OUT] = jax.ShapeDtypeStruct((depth, n, KV_ROW), F32)
    in_specs = [pl.BlockSpec((tm, D_MODEL), lambda i: (i, 0)),
                pl.BlockSpec((D_MODEL, IN_W_PACKED), lambda i: (0, 0))]
    args, aliases = [x2, w_packed], {}
    if rows_buf is not None:
        in_specs.append(pl.BlockSpec(memory_space=pl.ANY))
        args.append(rows_buf)
        aliases = {2: ROW_OUT}
    return pl.pallas_call(
        _inproj_kernel,
        grid=(n // tm,),
        in_specs=in_specs,
        out_specs=out_specs,
        out_shape=out_shape,
        input_output_aliases=aliases,
        compiler_params=_cparams("arbitrary"),
        name="inproj",
    )(*args)


def _outproj_ln_kernel(oa_ref, ob_ref, oc_ref, x_ref, w_ref, g_ref, b_ref, y_ref, *, alpha):
    acc = _dot(oa_ref[...], w_ref[0:A_W, :])
    acc = acc + _dot(ob_ref[...], w_ref[A_W:A_W + B_W, :])
    acc = acc + _dot(oc_ref[...], w_ref[A_W + B_W:MIX_W, :])
    y_ref[...] = _layer_norm(alpha * x_ref[...] + acc, g_ref[...], b_ref[...])


def _outproj_ln(oa, ob, oc, x2, w_out_p, g, b, alpha):
    n = x2.shape[0]
    tm = min(512, n)
    assert n % tm == 0
    row = lambda w: pl.BlockSpec((tm, w), lambda i: (i, 0))
    const = lambda shape: pl.BlockSpec(shape, lambda i: (0, 0))
    return pl.pallas_call(
        functools.partial(_outproj_ln_kernel, alpha=alpha),
        grid=(n // tm,),
        in_specs=[row(A_W), row(B_W), row(C_W), row(D_MODEL), const((MIX_W, D_MODEL)),
                  const((1, D_MODEL)), const((1, D_MODEL))],
        out_specs=row(D_MODEL),
        out_shape=jax.ShapeDtypeStruct((n, D_MODEL), F32),
        compiler_params=_cparams("arbitrary"),
        name="outproj_ln",
    )(oa, ob, oc, x2, w_out_p, g, b)


ROUTER_LANES = LANES


def _route(lg):
    lane = _lane_iota(ROUTER_LANES)
    big = jnp.int32(ROUTER_LANES)
    is_g = lane < N_GROUPS
    lg1 = jnp.where(is_g, lg, -jnp.inf)
    m1 = jnp.max(lg1, axis=-1, keepdims=True)
    grp = jnp.min(jnp.where(lg1 == m1, lane, big), axis=-1, keepdims=True)
    pg = 1.0 / jnp.sum(jnp.where(is_g, jnp.exp(lg1 - m1), 0.0), axis=-1, keepdims=True)
    lo = N_GROUPS + grp * EXPERTS_PER_GROUP
    lg2 = jnp.where((lane >= lo) & (lane < lo + EXPERTS_PER_GROUP), lg, -jnp.inf)
    v1 = jnp.max(lg2, axis=-1, keepdims=True)
    i1 = jnp.min(jnp.where(lg2 == v1, lane, big), axis=-1, keepdims=True)
    lg2b = jnp.where(lane == i1, -jnp.inf, lg2)
    v2 = jnp.max(lg2b, axis=-1, keepdims=True)
    i2 = jnp.min(jnp.where(lg2b == v2, lane, big), axis=-1, keepdims=True)
    e2 = jnp.exp(v2 - v1)
    w1 = pg / (1.0 + e2)
    w2 = pg * e2 / (1.0 + e2)
    return jnp.where(lane == i1, w1, jnp.where(lane == i2, w2, 0.0)), grp


MOE_CHUNK = 256
MOE_EXPERTS_PER_STEP = 4


def _moe_kernel(x_ref, wr_ref, br_ref, w1_ref, w3_ref, w2_ref, g_ref, b_ref, y_ref,
                xs_ref, combs_ref, pos_ref, acc_ref, seg_ref, *, alpha):
    j = pl.program_id(1)
    tm = x_ref.shape[0]
    steps_per_group = EXPERTS_PER_GROUP // MOE_EXPERTS_PER_STEP
    lane = _lane_iota(ROUTER_LANES)
    rows = min(MOE_CHUNK, tm)
    shift = int(math.log2(rows))

    @pl.when(j == 0)
    def _():
        xh, xl = _split_hi_lo(x_ref[...])
        lg2 = _dot(xh, wr_ref[...])
        lg = lg2[:, 0:ROUTER_LANES] + lg2[:, ROUTER_LANES:] + _dot(xl, wr_ref[:, 0:ROUTER_LANES]) + br_ref[...]
        comb, grp = _route(lg)
        onehot = jnp.where(lane == grp, 1.0, 0.0)
        row = lax.broadcasted_iota(jnp.int32, (tm, tm), 0)
        col = lax.broadcasted_iota(jnp.int32, (tm, tm), 1)
        earlier = jnp.where(col < row, 1.0, 0.0).astype(MXU_DTYPE)
        rank = _dot(earlier, onehot.astype(MXU_DTYPE))
        cnt = jnp.sum(onehot, axis=0, keepdims=True)
        off = jnp.zeros_like(cnt)
        for g in range(1, N_GROUPS):
            off = off + jnp.where(lane >= g, cnt[:, g - 1:g], 0.0)
        pos = jnp.sum(jnp.where(lane == grp, off + rank, 0.0), axis=-1, keepdims=True)
        pos_ref[...] = jnp.broadcast_to(pos, pos_ref.shape)
        pos_row = pos_ref[...].T[0:1, :]
        sorted_row = lax.broadcasted_iota(jnp.int32, (tm, 1), 0).astype(F32)
        perm = jnp.where(pos_row == sorted_row, 1.0, 0.0).astype(MXU_DTYPE)
        xs_ref[...] = _dot(perm, xh).astype(xs_ref.dtype)
        cs = _dot(perm, jnp.concatenate(_split_hi_lo(comb), axis=1))
        combs_ref[...] = cs[:, 0:ROUTER_LANES] + cs[:, ROUTER_LANES:]
        acc_ref[...] = jnp.zeros_like(acc_ref)
        start = off.astype(jnp.int32)
        end = (off + cnt).astype(jnp.int32)
        for g in range(N_GROUPS):
            seg_ref[g] = start[0, g]
            seg_ref[N_GROUPS + g] = end[0, g]

    grp_id = j // steps_per_group
    lane0 = N_GROUPS + j * MOE_EXPERTS_PER_STEP
    first = jnp.right_shift(seg_ref[grp_id], shift)
    last = jnp.right_shift(seg_ref[N_GROUPS + grp_id] + (rows - 1), shift)

    def chunk(c, carry):
        r0 = pl.multiple_of(c * rows, rows)
        xc = xs_ref[pl.ds(r0, rows), :]
        cc = combs_ref[pl.ds(r0, rows), :]
        y = jnp.zeros((rows, D_MODEL), F32)
        for e in range(MOE_EXPERTS_PER_STEP):
            h1 = _dot(xc, w1_ref[e])
            h3 = _dot(xc, w3_ref[e])
            ce = jnp.sum(jnp.where(lane == lane0 + e, cc, 0.0), axis=-1, keepdims=True)
            hd = h1 * (1.0 / (1.0 + jnp.exp(-h1))) * h3 * ce
            y = y + _dot(hd.astype(MXU_DTYPE), w2_ref[e])
        acc_ref[pl.ds(r0, rows), :] += y
        return carry

    lax.fori_loop(first, last, chunk, 0)

    @pl.when(j == pl.num_programs(1) - 1)
    def _():
        col = lax.broadcasted_iota(jnp.int32, (1, tm), 1).astype(F32)
        unperm = jnp.where(pos_ref[:, 0:1] == col, 1.0, 0.0).astype(MXU_DTYPE)
        y_hi, y_lo = _split_hi_lo(acc_ref[...])
        y = _dot(unperm, y_hi) + _dot(unperm, y_lo)
        y_ref[...] = _layer_norm(alpha * x_ref[...] + y, g_ref[...], b_ref[...])


def _moe_ln(x2, wr_cat, br, w1, w3, w2, g, b, alpha):
    n = x2.shape[0]
    tm = min(1024, n)
    eps = MOE_EXPERTS_PER_STEP
    assert n % tm == 0 and tm % min(MOE_CHUNK, tm) == 0 and tm & (tm - 1) == 0 and EXPERTS_PER_GROUP % eps == 0
    const = lambda shape: pl.BlockSpec(shape, lambda i, j: (0,) * len(shape))
    return pl.pallas_call(
        functools.partial(_moe_kernel, alpha=alpha),
        grid=(n // tm, N_EXPERTS // eps),
        in_specs=[pl.BlockSpec((tm, D_MODEL), lambda i, j: (i, 0)),
                  const((D_MODEL, 2 * ROUTER_LANES)), const((1, ROUTER_LANES)),
                  pl.BlockSpec((eps, D_MODEL, EXPERT_FF), lambda i, j: (j, 0, 0)),
                  pl.BlockSpec((eps, D_MODEL, EXPERT_FF), lambda i, j: (j, 0, 0)),
                  pl.BlockSpec((eps, EXPERT_FF, D_MODEL), lambda i, j: (j, 0, 0)),
                  const((1, D_MODEL)), const((1, D_MODEL))],
        out_specs=pl.BlockSpec((tm, D_MODEL), lambda i, j: (i, 0)),
        out_shape=jax.ShapeDtypeStruct((n, D_MODEL), F32),
        scratch_shapes=[pltpu.VMEM((tm, D_MODEL), MXU_DTYPE), pltpu.VMEM((tm, ROUTER_LANES), F32),
                        pltpu.VMEM((tm, LANES), F32), pltpu.VMEM((tm, D_MODEL), F32),
                        pltpu.SMEM((2 * N_GROUPS,), jnp.int32)],
        compiler_params=_cparams("arbitrary", "arbitrary"),
        name="moe_ln",
    )(x2, wr_cat, br, w1, w3, w2, g, b)


def _gelu_tanh(x):
    return 0.5 * x * (1.0 + jnp.tanh(math.sqrt(2.0 / math.pi) * (x + 0.044715 * (x * x * x))))


def _compress_kernel(pt_ref, *refs, n_ops, op_rows):
    del pt_ref
    pages = (refs[:n_ops], refs[n_ops:2 * n_ops])
    pos_ref, wlo_ref, whi_ref, w2_ref, out_ref, carry_ref = refs[2 * n_ops:]
    cpo = op_rows // CMP_STRIDE
    m = n_ops * cpo

    @pl.when(pl.program_id(1) == 0)
    def _():
        carry_ref[...] = jnp.zeros_like(carry_ref)

    first = _lane_iota(C_KV_W) < HEAD_DIM
    row0 = lax.broadcasted_iota(jnp.int32, (m, 1), 0) == 0
    for kv in range(2):
        u = jnp.zeros((C_GROUPS * m, CMP_HIDDEN), F32)
        v = jnp.zeros((C_GROUPS * m, CMP_HIDDEN), F32)
        for quad in range(CMP_STRIDE // 4):
            lo, hi = [], []
            for pair in (2 * quad, 2 * quad + 1):
                x0, x1 = [jnp.concatenate([p[0, 0, pl.ds(t, cpo, stride=CMP_STRIDE), :] for p in pages[kv]], axis=0)
                          for t in (2 * pair, 2 * pair + 1)]
                r0, r1 = pltpu.roll(x0, HEAD_DIM, axis=1), pltpu.roll(x1, HEAD_DIM, axis=1)
                ab = jnp.concatenate([jnp.where(first, x0, r1), jnp.where(first, r0, x1)], axis=0)
                lo.append((ab + pos_ref[kv, 0, pair:pair + 1, :]).astype(MXU_DTYPE))
                hi.append((ab + pos_ref[kv, 1, pair:pair + 1, :]).astype(MXU_DTYPE))
            u = u + _dot(jnp.concatenate(lo, axis=1), wlo_ref[kv, quad])
            v = v + _dot(jnp.concatenate(hi, axis=1), whi_ref[kv, quad])
        out = jnp.zeros((m, C_KV_W), F32)
        for g in range(C_GROUPS):
            ug, vg = u[g * m:(g + 1) * m], v[g * m:(g + 1) * m]
            cs = slice((kv * C_GROUPS + g) * CMP_HIDDEN, (kv * C_GROUPS + g + 1) * CMP_HIDDEN)
            prev = jnp.where(row0, carry_ref[0:1, cs], pltpu.roll(ug, 1, axis=0))
            carry_ref[0:1, cs] = ug[m - 1:m, :]
            out = out + _dot(_gelu_tanh(prev + vg).astype(MXU_DTYPE), w2_ref[kv, g])
        out_ref[0, :, kv * C_KV_W:(kv + 1) * C_KV_W] = out


def _compress(cache, layer, page_table, col_block, cw, n_ops):
    pos2, wlo, whi, w2b = cw
    nb, n_pages = page_table.shape
    op_rows = cache.shape[2]
    assert n_pages % n_ops == 0
    steps = n_pages // n_ops
    cpo = op_rows // CMP_STRIDE
    kv_w = 2 * C_KV_W

    def page_map(b, s, pt, *, k, kv):
        return (layer, pt[b, s * n_ops + k], 0, col_block + kv)

    const = lambda shape: pl.BlockSpec(shape, lambda b, s, pt: (0,) * len(shape))
    grid_spec = pltpu.PrefetchScalarGridSpec(
        num_scalar_prefetch=1,
        grid=(nb, steps),
        in_specs=[pl.BlockSpec((1, 1, op_rows, C_KV_W), functools.partial(page_map, k=k, kv=kv))
                  for kv in range(2) for k in range(n_ops)]
        + [const(pos2.shape), const(wlo.shape), const(whi.shape), const(w2b.shape)],
        out_specs=pl.BlockSpec((1, n_ops * cpo, kv_w), lambda b, s, pt: (b, s, 0)),
        scratch_shapes=[pltpu.VMEM((SUBLANES, 2 * C_GROUPS * CMP_HIDDEN), F32)],
    )
    return pl.pallas_call(
        functools.partial(_compress_kernel, n_ops=n_ops, op_rows=op_rows),
        grid_spec=grid_spec,
        out_shape=jax.ShapeDtypeStruct((nb, n_pages * cpo, kv_w), F32),
        compiler_params=_cparams("arbitrary", "arbitrary"),
        name="nsa_compress",
    )(page_table, *([cache] * (2 * n_ops)), pos2, wlo, whi, w2b)


def _attend_t(qs, k_ref, vt_ref, acc_ref, v_rows, lo, qi, far_fn, own_fn):
    ns = qs.shape[0] // T
    w = (ns // len(v_rows)) * T
    acc_ref[...] = jnp.zeros_like(acc_ref)

    def tile(n, m, l, fn):
        off = pl.multiple_of(n * T, T)
        s_all = _dot_nt(k_ref[pl.ds(off, T), :], qs)
        s = jnp.concatenate([fn(c, s_all[:, c * T:(c + 1) * T]) for c in range(ns)], axis=1)
        m_new = jnp.maximum(m, jnp.max(s, axis=0, keepdims=True))
        alpha = jnp.exp(m - m_new)
        p = jnp.exp(s - m_new)
        l = alpha * l + jnp.sum(p, axis=0, keepdims=True)
        pb = p.astype(MXU_DTYPE)
        for gi, r0 in enumerate(v_rows):
            cs = slice(gi * w, (gi + 1) * w)
            acc_ref[gi] = acc_ref[gi] * alpha[:, cs] + _dot(vt_ref[n, r0:r0 + HEAD_DIM, :], pb[:, cs])
        return m_new, l

    init = (jnp.full((1, ns * T), NEG, F32), jnp.zeros((1, ns * T), F32))

    def pair(i, c):
        n = lo + 2 * i
        m1, l1 = tile(n, c[0], c[1], far_fn(n))
        return tile(n + 1, m1, l1, far_fn(n + 1))

    n_far = qi - lo
    m, l = lax.fori_loop(0, n_far // 2, pair, init)
    m, l = lax.fori_loop(qi - n_far % 2, qi, lambda n, c: tile(n, c[0], c[1], far_fn(n)), (m, l))
    _, l = tile(qi, m, l, own_fn)
    return [acc_ref[gi] / l[:, gi * w:(gi + 1) * w] for gi in range(len(v_rows))]


def _causal_penalty():
    j = lax.broadcasted_iota(jnp.int32, (T, T), 0)
    i = lax.broadcasted_iota(jnp.int32, (T, T), 1)
    return jnp.where(j <= i, 0.0, NEG)


def _cast_tiles(src_ref, col, width, dst_ref, seq):
    for n in range(seq // T):
        dst_ref[n * T:(n + 1) * T, :] = src_ref[0, n * T:(n + 1) * T, col:col + width].astype(dst_ref.dtype)


def _transpose_tiles(src_ref, col, width, dst_ref, seq):
    for n in range(seq // T):
        dst_ref[n] = src_ref[0, n * T:(n + 1) * T, col:col + width].T.astype(dst_ref.dtype)


def _rank_rows(score, n_rows, valid_fn=None):
    blk = lax.broadcasted_iota(jnp.int32, (score.shape[0], 1), 0)
    rank = jnp.zeros(score.shape, F32)
    for mb in range(n_rows):
        row = score[mb:mb + 1, :]
        beats = (row > score) | ((row == score) & (blk > mb))
        rank = rank + jnp.where(beats, 1.0 if valid_fn is None else valid_fn(mb), 0.0)
    return rank


def _moba_prompt_kernel(q_ref, kv_ref, bias_ref, o_ref, kb_ref, vt_ref, mean_ref, pen_ref, acc_ref, *, seq):
    qi = pl.program_id(1)
    nb = seq // MOBA_BLOCK

    @pl.when(qi == 0)
    def _():
        _cast_tiles(kv_ref, 0, A_W, kb_ref, seq)
        _transpose_tiles(kv_ref, A_W, A_W, vt_ref, seq)
        mean_ref[...] = jnp.zeros_like(mean_ref)
        for n in range(nb):
            mean_ref[n:n + 1, :] = jnp.mean(kv_ref[0, n * T:(n + 1) * T, 0:A_W], axis=0, keepdims=True)

    q = q_ref[0]
    lane_q = _lane_iota(A_W)
    qs = jnp.concatenate([jnp.where(_in_range(lane_q, h * HEAD_DIM, (h + 1) * HEAD_DIM), q, jnp.zeros_like(q))
                          for h in range(A_HEADS)], axis=0)
    gate = _dot_nt(mean_ref[...].astype(MXU_DTYPE), qs)
    blk = lax.broadcasted_iota(jnp.int32, (gate.shape[0], 1), 0)
    rank = _rank_rows(gate, nb, lambda mb: jnp.where(qi > mb, 1.0, 0.0))
    pen_ref[...] = jnp.where((blk < qi) & (rank < MOBA_TOPK), 0.0, NEG)
    causal = _causal_penalty()

    def far_fn(n):
        idx = jnp.minimum(qi - n, 2)
        return lambda c, s: s + bias_ref[c, idx] + pen_ref[pl.ds(n, 1), c * T:(c + 1) * T]

    outs = _attend_t(qs, kb_ref, vt_ref, acc_ref, [h * HEAD_DIM for h in range(A_HEADS)], 0, qi, far_fn,
                     lambda c, s: s + bias_ref[c, 0] + causal)
    o_ref[0] = jnp.concatenate(outs, axis=0).T.astype(o_ref.dtype)


def _moba_prompt(qa, row, layer, bias_a):
    _, b, seq, _ = row.shape
    nb = seq // MOBA_BLOCK
    nbp = -(-nb // SUBLANES) * SUBLANES
    assert seq % T == 0 and T == MOBA_BLOCK and nb >= MOBA_TOPK
    return pl.pallas_call(
        functools.partial(_moba_prompt_kernel, seq=seq),
        grid=(b, seq // T),
        in_specs=[pl.BlockSpec((1, T, A_W), lambda i, j: (i, j, 0)),
                  pl.BlockSpec((None, 1, seq, 2 * A_W), lambda i, j: (layer, i, 0, KA_OFF // (2 * A_W))),
                  pl.BlockSpec(bias_a.shape, lambda i, j: (0, 0, 0, 0))],
        out_specs=pl.BlockSpec((1, T, A_W), lambda i, j: (i, j, 0)),
        out_shape=jax.ShapeDtypeStruct((b, seq, A_W), MXU_DTYPE),
        scratch_shapes=[pltpu.VMEM((seq, A_W), MXU_DTYPE), pltpu.VMEM((seq // T, A_W, T), MXU_DTYPE),
                        pltpu.VMEM((nbp, A_W), F32), pltpu.VMEM((nbp, A_HEADS * T), F32),
                        pltpu.VMEM((A_HEADS, HEAD_DIM, T), F32)],
        compiler_params=_cparams("arbitrary", "arbitrary"),
        name="moba_prompt",
    )(qa, row, bias_a)


def _diff_finish(o0, o1, lam, lane_v):
    o = o0 - lam * o1
    out = jnp.zeros_like(o)
    for h in range(B_HEADS):
        hm = _in_range(lane_v, h * B_V_DIM, (h + 1) * B_V_DIM)
        ms = jnp.sum(jnp.where(hm, o * o, 0.0), axis=-1, keepdims=True) * (1.0 / B_V_DIM)
        out = jnp.where(hm, o * lax.rsqrt(ms + LN_EPS), out)
    return out


def _diff_prompt_kernel(dp_ref, q_ref, kv_ref, bias_ref, subw_ref, o_ref, kb_ref, vt_ref, acc_ref, *, seq):
    qi = pl.program_id(1)

    @pl.when(qi == 0)
    def _():
        _cast_tiles(kv_ref, 0, B_QK_W, kb_ref, seq)
        _transpose_tiles(kv_ref, B_QK_W, B_W, vt_ref, seq)

    q = q_ref[0]
    lane_q = _lane_iota(B_QK_W)
    qs = jnp.concatenate([jnp.where(_in_range(lane_q, c * B_QK_DIM, (c + 1) * B_QK_DIM), q, jnp.zeros_like(q))
                          for c in range(2 * B_HEADS)], axis=0)
    causal = _causal_penalty()

    def far_fn(n):
        idx = jnp.minimum(qi - n, 2)
        return lambda c, s: s + bias_ref[c // 2, idx]

    outs = _attend_t(qs, kb_ref, vt_ref, acc_ref, [h * B_V_DIM for h in range(B_HEADS)], 0, qi, far_fn,
                     lambda c, s: s + bias_ref[c // 2, 0] + causal)
    lam = dp_ref[:, 0:1]
    heads = []
    for h in range(B_HEADS):
        o = outs[h][:, 0:T] - lam * outs[h][:, T:2 * T]
        ms = jnp.mean(o * o, axis=0, keepdims=True)
        heads.append(o * lax.rsqrt(ms + LN_EPS))
    out = jnp.concatenate(heads, axis=0) * subw_ref[...] * dp_ref[:, 1:2]
    o_ref[0] = out.T.astype(o_ref.dtype)


def _diff_prompt(dp, qb, row, layer, bias_b, subw_t):
    _, b, seq, _ = row.shape
    kvw = B_QK_W + B_W
    assert B_V_DIM == HEAD_DIM
    return pl.pallas_call(
        functools.partial(_diff_prompt_kernel, seq=seq),
        grid=(b, seq // T),
        in_specs=[pl.BlockSpec((1, LANES), lambda i, j: (0, 0)),
                  pl.BlockSpec((1, T, B_QK_W), lambda i, j: (i, j, 0)),
                  pl.BlockSpec((None, 1, seq, kvw), lambda i, j: (layer, i, 0, KB_OFF // kvw)),
                  pl.BlockSpec(bias_b.shape, lambda i, j: (0, 0, 0, 0)),
                  pl.BlockSpec((B_W, T), lambda i, j: (0, 0))],
        out_specs=pl.BlockSpec((1, T, B_W), lambda i, j: (i, j, 0)),
        out_shape=jax.ShapeDtypeStruct((b, seq, B_W), MXU_DTYPE),
        scratch_shapes=[pltpu.VMEM((seq, B_QK_W), MXU_DTYPE), pltpu.VMEM((seq // T, B_W, T), MXU_DTYPE),
                        pltpu.VMEM((B_HEADS, B_V_DIM, 2 * T), F32)],
        compiler_params=_cparams("arbitrary", "arbitrary"),
        name="diff_prompt",
    )(dp, qb, row, bias_b, subw_t)


def _stack_group_queries(q, g):
    gm = _in_range(_lane_iota(C_KV_W), g * HEAD_DIM, (g + 1) * HEAD_DIM)
    q = q.astype(F32)
    return jnp.concatenate(
        [jnp.where(gm, q[:, hg * C_KV_W:(hg + 1) * C_KV_W], 0.0) for hg in range(C_HPG)], axis=0).astype(MXU_DTYPE)


def _stack_gate(gt, g, branch):
    cols = [gt[:, (g * C_HPG + hg) * N_BRANCH + branch:(g * C_HPG + hg) * N_BRANCH + branch + 1] for hg in range(C_HPG)]
    x = jnp.concatenate(cols, axis=0)
    return 1.0 / (1.0 + jnp.exp(-x))


def _cmp_attention(qs, kc, vc, q_pos_rows, n_cmp):
    nch = kc.shape[0]
    s = _dot_nt(qs, kc)
    r = _lane_iota(nch)
    ok = (r >= 1) & (r <= n_cmp) & ((r - 1) * CMP_STRIDE + (CMP_LEN - 1) <= q_pos_rows)
    s = jnp.where(ok, s, NEG)
    p = jnp.where(ok, jnp.exp(s - jnp.max(s, axis=-1, keepdims=True)), 0.0)
    l = jnp.sum(p, axis=-1, keepdims=True)
    p = p / jnp.where(l > 0.0, l, 1.0)
    return p, _dot(p.astype(MXU_DTYPE), vc)


def _nsa_prompt_kernel(q_ref, gt_ref, cmp_ref, kv_ref, win_ref, bias_ref, wsel_ref, o_ref,
                       ks_ref, vst_ref, kw_ref, vwt_ref, kc_ref, vct_ref, pen_ref, acc_ref, *, seq, n_cmp):
    qi = pl.program_id(1)
    nsel = seq // SEL_BLOCK
    bpt = T // SEL_BLOCK

    @pl.when(qi == 0)
    def _():
        _cast_tiles(kv_ref, 0, C_KV_W, ks_ref, seq)
        _transpose_tiles(kv_ref, C_KV_W, C_KV_W, vst_ref, seq)
        _cast_tiles(win_ref, 0, C_KV_W, kw_ref, seq)
        _transpose_tiles(win_ref, C_KV_W, C_KV_W, vwt_ref, seq)
        kc_ref[...] = cmp_ref[0, :, 0:C_KV_W].astype(kc_ref.dtype)
        vct_ref[...] = cmp_ref[0, :, C_KV_W:2 * C_KV_W].T.astype(vct_ref.dtype)

    q = q_ref[0]
    gates = 1.0 / (1.0 + jnp.exp(-gt_ref[0].T))
    q_pos = qi * T + _lane_iota(T)
    q_pos4 = jnp.concatenate([q_pos] * C_HPG, axis=1)
    nch = kc_ref.shape[0]
    r = lax.broadcasted_iota(jnp.int32, (nch, 1), 0)
    cmp_ok = (r >= 1) & (r <= n_cmp) & ((r - 1) * CMP_STRIDE + (CMP_LEN - 1) <= q_pos4)
    blk = lax.broadcasted_iota(jnp.int32, (pen_ref.shape[1], 1), 0)
    own = jnp.right_shift(q_pos, int(math.log2(SEL_BLOCK)))
    causal = _causal_penalty()
    jj = lax.broadcasted_iota(jnp.int32, (T, T), 0)
    ii = lax.broadcasted_iota(jnp.int32, (T, T), 1)
    qs = jnp.concatenate([_stack_group_queries(q, g) for g in range(C_GROUPS)], axis=0)
    vrows = [g * HEAD_DIM for g in range(C_GROUPS)]
    o_cmp = []
    for g in range(C_GROUPS):
        s = jnp.where(cmp_ok, _dot_nt(kc_ref[...], qs[g * C_HPG * T:(g + 1) * C_HPG * T]), NEG)
        p = jnp.where(cmp_ok, jnp.exp(s - jnp.max(s, axis=0, keepdims=True)), 0.0)
        l = jnp.sum(p, axis=0, keepdims=True)
        p = p / jnp.where(l > 0.0, l, 1.0)
        o_cmp.append(_dot(vct_ref[g * HEAD_DIM:(g + 1) * HEAD_DIM, :], p.astype(MXU_DTYPE)))
        pg = p[:, 0:T] + p[:, T:2 * T] + p[:, 2 * T:3 * T] + p[:, 3 * T:4 * T]
        ph, plo = _split_hi_lo(pg)
        p_slc = _dot(wsel_ref[...], ph) + _dot(wsel_ref[...], plo)
        forced = (blk == 0) | (blk == own) | (blk == own - 1)
        score = jnp.where(blk <= own, jnp.where(forced, jnp.inf, p_slc), -jnp.inf)
        rank = _rank_rows(score, nsel)
        pen_ref[g] = jnp.where((blk <= own) & (rank < SEL_TOPN), 0.0, NEG)

    def pen_tile(g, n):
        return jnp.concatenate([jnp.broadcast_to(pen_ref[g, pl.ds(n * bpt + b, 1), :], (SEL_BLOCK, T))
                                for b in range(bpt)], axis=0)

    def slc_far(n):
        idx = jnp.minimum(qi - n, 2)
        pens = [pen_tile(g, n) for g in range(C_GROUPS)]
        return lambda c, s: s + bias_ref[idx, c] + pens[c // C_HPG]

    own_pens = [pen_tile(g, qi) + causal for g in range(C_GROUPS)]
    o_slc = _attend_t(qs, ks_ref, vst_ref, acc_ref, vrows, 0, qi, slc_far,
                      lambda c, s: s + bias_ref[0, c] + own_pens[c // C_HPG])

    def win_far(n):
        d = qi - n
        idx = jnp.minimum(d, 2)
        pen = jnp.where(ii >= jj, jnp.where(d * T >= WINDOW, NEG, 0.0), 0.0)
        return lambda c, s: s + bias_ref[idx, c] + pen

    o_win = _attend_t(qs, kw_ref, vwt_ref, acc_ref, vrows, jnp.maximum(qi - WINDOW // T, 0), qi, win_far,
                      lambda c, s: s + bias_ref[0, c] + causal)

    def gate_row(g, branch):
        return jnp.concatenate([gates[(g * C_HPG + hg) * N_BRANCH + branch:(g * C_HPG + hg) * N_BRANCH + branch + 1, :]
                                for hg in range(C_HPG)], axis=1)

    outs = [gate_row(g, 0) * o_cmp[g] + gate_row(g, 1) * o_slc[g] + gate_row(g, 2) * o_win[g]
            for g in range(C_GROUPS)]
    out_t = jnp.concatenate([outs[g][:, hg * T:(hg + 1) * T] for hg in range(C_HPG) for g in range(C_GROUPS)], axis=0)
    o_ref[0] = out_t.T.astype(o_ref.dtype)


def _nsa_prompt(qc, gt, cmp, row, layer, win, bias_c, wsel_t):
    _, b, seq, _ = row.shape
    nch = cmp.shape[1]
    n_cmp = (seq - CMP_LEN) // CMP_STRIDE + 1
    nselp = wsel_t.shape[0]
    assert seq // SEL_BLOCK >= SEL_TOPN and WINDOW == 2 * T and nselp >= seq // SEL_BLOCK
    kvw = 2 * C_KV_W
    nk = seq // T
    full = lambda a: pl.BlockSpec(a.shape, lambda i, j: (0,) * a.ndim)
    return pl.pallas_call(
        functools.partial(_nsa_prompt_kernel, seq=seq, n_cmp=n_cmp),
        grid=(b, nk),
        in_specs=[pl.BlockSpec((1, T, C_W), lambda i, j: (i, j, 0)),
                  pl.BlockSpec((1, T, GT_PAD), lambda i, j: (i, j, 0)),
                  pl.BlockSpec((1, nch, kvw), lambda i, j: (i, 0, 0)),
                  pl.BlockSpec((None, 1, seq, kvw), lambda i, j: (layer, i, 0, KS_OFF // kvw)),
                  pl.BlockSpec((1, seq, kvw), lambda i, j: (i, 0, 0)),
                  full(bias_c), full(wsel_t)],
        out_specs=pl.BlockSpec((1, T, C_W), lambda i, j: (i, j, 0)),
        out_shape=jax.ShapeDtypeStruct((b, seq, C_W), MXU_DTYPE),
        scratch_shapes=[pltpu.VMEM((seq, C_KV_W), MXU_DTYPE), pltpu.VMEM((nk, C_KV_W, T), MXU_DTYPE),
                        pltpu.VMEM((seq, C_KV_W), MXU_DTYPE), pltpu.VMEM((nk, C_KV_W, T), MXU_DTYPE),
                        pltpu.VMEM((nch, C_KV_W), MXU_DTYPE), pltpu.VMEM((C_KV_W, nch), MXU_DTYPE),
                        pltpu.VMEM((C_GROUPS, nselp, T), F32), pltpu.VMEM((C_GROUPS, HEAD_DIM, C_HPG * T), F32)],
        compiler_params=_cparams("arbitrary", "arbitrary"),
        name="nsa_prompt",
    )(qc, gt, cmp, row, win, bias_c, wsel_t)


PG = 16
PG_CMP = 32
NEW_PAD = LANES


def _rows_iota(n_rep, n):
    return jnp.concatenate([lax.broadcasted_iota(jnp.int32, (n, 1), 0)] * n_rep, axis=0)


def _pad_rows(x, rows):
    return jnp.concatenate([x, jnp.zeros((rows - x.shape[0], x.shape[1]), x.dtype)], axis=0)


def _page_specs(layer, width, col_block, n_ops):
    def page_map(b, s, pt, *, k):
        return (layer, pt[b, s * n_ops + k], 0, col_block)
    return [pl.BlockSpec((1, 1, PAGE_SIZE, width), functools.partial(page_map, k=k)) for k in range(n_ops)]


def _step_bias_spec(shape, steps):
    return pl.BlockSpec((1,) + shape[1:], lambda b, s, pt: (jnp.where(s == steps - 1, 1, 0),) + (0,) * (len(shape) - 1))


def _new_token_logits(qs, k_new, bias_new, nq):
    s = _dot_nt(qs, _pad_rows(k_new, NEW_PAD).astype(MXU_DTYPE)) + bias_new
    j = _lane_iota(NEW_PAD)
    i = _rows_iota(qs.shape[0] // nq, nq)
    return jnp.where((j < nq) & (j <= i), s, NEG)


def _nsa_sample_pre_kernel(q_ref, cmp_ref, state_ref, winnew_ref, bias_ref, wsel_ref, sel_ref, ocw_ref,
                           *, past, n_cmp, nq):
    q = q_ref[0]
    wb = state_ref.shape[2]
    wk = bias_ref.shape[-1]
    rows = C_HPG * nq
    kc = cmp_ref[0, :, 0:C_KV_W].astype(MXU_DTYPE)
    vc = cmp_ref[0, :, C_KV_W:2 * C_KV_W].astype(MXU_DTYPE)
    pad = jnp.zeros((wk - wb - nq, C_KV_W), F32)
    kw = jnp.concatenate([state_ref[0, 0, :, 0:C_KV_W], winnew_ref[0, :, 0:C_KV_W], pad], axis=0).astype(MXU_DTYPE)
    vw = jnp.concatenate([state_ref[0, 0, :, C_KV_W:], winnew_ref[0, :, C_KV_W:], pad], axis=0).astype(MXU_DTYPE)
    nsl = wsel_ref.shape[1]
    lane = _lane_iota(nsl)
    big = jnp.int32(nsl)
    i_q = lax.broadcasted_iota(jnp.int32, (nq, 1), 0)
    q_pos = past + i_q
    q_pos4 = past + _rows_iota(C_HPG, nq)
    scores = []
    for g in range(C_GROUPS):
        qs = _stack_group_queries(q, g)
        p, o_cmp = _cmp_attention(qs, kc, vc, q_pos4, n_cmp)
        pg = p[0:nq] + p[nq:2 * nq] + p[2 * nq:3 * nq] + p[3 * nq:4 * nq]
        ph, plo = _split_hi_lo(pg)
        p_slc = _dot(ph, wsel_ref[...]) + _dot(plo, wsel_ref[...])
        own = jnp.right_shift(q_pos, int(math.log2(SEL_BLOCK)))
        forced = (lane == 0) | (lane == own) | (lane == own - 1)
        scores.append(jnp.where(lane <= own, jnp.where(forced, jnp.inf, p_slc), -jnp.inf))
        s = _dot_nt(qs, kw) +bias_ref[g * C_HPG:(g + 1) * C_HPG].reshape(rows, wk)
        j = _lane_iota(wk)
        rel = wb + _rows_iota(C_HPG, nq) - j
        ok = (j < wb + nq) & (rel >= 0) & (rel < WINDOW)
        s = jnp.where(ok, s, NEG)
        pw = jnp.exp(s - jnp.max(s, axis=-1, keepdims=True))
        o_win = _dot(pw.astype(MXU_DTYPE), vw) / jnp.sum(pw, axis=-1, keepdims=True)
        ocw_ref[0, g * rows:(g + 1) * rows, 0:C_KV_W] = o_cmp
        ocw_ref[0, g * rows:(g + 1) * rows, C_KV_W:] = o_win
    score = jnp.concatenate(scores, axis=0)
    sel = jnp.zeros(score.shape, F32)
    for _ in range(SEL_TOPN):
        v = jnp.max(score, axis=-1, keepdims=True)
        idx = jnp.min(jnp.where(score == v, lane, big), axis=-1, keepdims=True)
        hit = lane == idx
        sel = jnp.where(hit & (v > -jnp.inf), 1.0, sel)
        score = jnp.where(hit, -jnp.inf, score)
    for g in range(C_GROUPS):
        sel_ref[0, g * rows:(g + 1) * rows, :] = jnp.concatenate([sel[g * nq:(g + 1) * nq]] * C_HPG, axis=0)


def _nsa_sample_pre(qc, cmp, state_win, layer, win_new, bias_w, wsel, past):
    bs, nq, _ = qc.shape
    nch = cmp.shape[1]
    wb = state_win.shape[2]
    n_cmp = (past + nq - CMP_LEN) // CMP_STRIDE + 1
    rows = C_HEADS * nq
    full = lambda a: pl.BlockSpec(a.shape, lambda b: (0,) * a.ndim)
    return pl.pallas_call(
        functools.partial(_nsa_sample_pre_kernel, past=past, n_cmp=n_cmp, nq=nq),
        grid=(bs,),
        in_specs=[pl.BlockSpec((1, nq, C_W), lambda b: (b, 0, 0)),
                  pl.BlockSpec((1, nch, 2 * C_KV_W), lambda b: (b, 0, 0)),
                  pl.BlockSpec((1, 1, wb, WIN_ROW), lambda b: (layer, b, 0, 0)),
                  pl.BlockSpec((1, nq, WIN_ROW), lambda b: (b, 0, 0)),
                  full(bias_w), full(wsel)],
        out_specs=[pl.BlockSpec((1, rows, wsel.shape[1]), lambda b: (b, 0, 0)),
                   pl.BlockSpec((1, rows, 2 * C_KV_W), lambda b: (b, 0, 0))],
        out_shape=[jax.ShapeDtypeStruct((bs, rows, wsel.shape[1]), F32),
                   jax.ShapeDtypeStruct((bs, rows, 2 * C_KV_W), F32)],
        compiler_params=_cparams("arbitrary"),
        name="nsa_sample_pre",
    )(qc, cmp, state_win, win_new, bias_w, wsel)


def _nsa_sample_slc_kernel(pt_ref, *refs, nq):
    del pt_ref
    pages = refs[:PG]
    (q_ref, gt_ref, selw_ref, exp_ref, bias_ref, rownew_ref, biasnew_ref, ocw_ref,
     o_ref, m_ref, l_ref, acc_ref) = refs[PG:]
    rows = C_HPG * nq
    q = q_ref[0]

    def init():
        m_ref[...] = jnp.full_like(m_ref, NEG)
        l_ref[...] = jnp.zeros_like(l_ref)
        acc_ref[...] = jnp.zeros_like(acc_ref)

    def main():
        k = jnp.concatenate([p[0, 0, :, 0:C_KV_W] for p in pages], axis=0).astype(MXU_DTYPE)
        v = jnp.concatenate([p[0, 0, :, C_KV_W:] for p in pages], axis=0).astype(MXU_DTYPE)
        for g in range(C_GROUPS):
            rs = slice(g * rows, (g + 1) * rows)
            qs = _stack_group_queries(q, g)
            s = _dot_nt(qs, k) + bias_ref[0, g * C_HPG:(g + 1) * C_HPG].reshape(rows, PG * PAGE_SIZE)
            ok = _dot(selw_ref[0, 0, rs, :].astype(MXU_DTYPE), exp_ref[...]) > 0.5
            p, alpha, m, l = _online_update(jnp.where(ok, s, NEG), m_ref[rs], l_ref[rs])
            m_ref[rs] = m
            l_ref[rs] = l
            acc_ref[rs] = acc_ref[rs] * alpha + _dot(p.astype(MXU_DTYPE), v)

    def final():
        gt = gt_ref[0]
        k_new = rownew_ref[0, :, 0:C_KV_W]
        v_new = _pad_rows(rownew_ref[0, :, C_KV_W:], NEW_PAD).astype(MXU_DTYPE)
        outs = []
        for g in range(C_GROUPS):
            rs = slice(g * rows, (g + 1) * rows)
            qs = _stack_group_queries(q, g)
            s = _new_token_logits(qs, k_new, biasnew_ref[g * C_HPG:(g + 1) * C_HPG].reshape(rows, NEW_PAD), nq)
            p, alpha, _, l = _online_update(s, m_ref[rs], l_ref[rs])
            o_slc = (acc_ref[rs] * alpha + _dot(p.astype(MXU_DTYPE), v_new)) / l
            ocw = ocw_ref[0, rs, :]
            outs.append(_stack_gate(gt, g, 0) * ocw[:, 0:C_KV_W] + _stack_gate(gt, g, 1) * o_slc
                        + _stack_gate(gt, g, 2) * ocw[:, C_KV_W:])
        g0 = _lane_iota(C_KV_W) < HEAD_DIM
        for hg in range(C_HPG):
            chunk = jnp.where(g0, outs[0][hg * nq:(hg + 1) * nq], outs[1][hg * nq:(hg + 1) * nq])
            o_ref[0, :, hg * C_KV_W:(hg + 1) * C_KV_W] = chunk.astype(o_ref.dtype)

    return init, main, final


def _nsa_sample_slc(cache, layer, page_table, qc, gt, selw, expand, bias_steps, row_new, bias_new, ocw):
    bs, nq, _ = qc.shape
    steps = page_table.shape[1] // PG
    rows = C_HEADS * nq
    kvw = 2 * C_KV_W
    per_seq = lambda shape: pl.BlockSpec((1,) + shape, lambda b, s, pt: (b,) + (0,) * len(shape))
    full = lambda a: pl.BlockSpec(a.shape, lambda b, s, pt: (0,) * a.ndim)
    return dict(
        body=functools.partial(_nsa_sample_slc_kernel, nq=nq),
        in_specs=_page_specs(layer, kvw, KS_OFF // kvw, PG)
        + [per_seq((nq, C_W)), per_seq((nq, GT_PAD)),
           pl.BlockSpec((1, 1, rows, LANES), lambda b, s, pt: (b, s, 0, 0)),
           full(expand), _step_bias_spec(bias_steps.shape, steps),
           pl.BlockSpec((1, nq, kvw), lambda b, s, pt: (b, 0, KS_OFF // kvw)),
           full(bias_new), per_seq((rows, kvw))],
        args=[cache] * PG + [qc, gt, selw, expand, bias_steps, row_new, bias_new, ocw],
        out_spec=per_seq((nq, C_W)),
        out_shape=jax.ShapeDtypeStruct((bs, nq, C_W), F32),
        scratch=[pltpu.VMEM((rows, 1), F32), pltpu.VMEM((rows, 1), F32), pltpu.VMEM((rows, C_KV_W), F32)])


def _diff_sample_kernel(pt_ref, *refs, nq):
    del pt_ref
    pages = refs[:PG]
    dp_ref, q_ref, bias_ref, rownew_ref, biasnew_ref, subw_ref, o_ref, m_ref, l_ref, acc_ref = refs[PG:]

    def init():
        m_ref[...] = jnp.full_like(m_ref, NEG)
        l_ref[...] = jnp.zeros_like(l_ref)
        acc_ref[...] = jnp.zeros_like(acc_ref)

    q = q_ref[0]
    lane_q = _lane_iota(B_QK_W)
    qs = jnp.concatenate([jnp.where(_in_range(lane_q, c * B_QK_DIM, (c + 1) * B_QK_DIM), q, 0.0)
                          for c in range(2 * B_HEADS)], axis=0).astype(MXU_DTYPE)

    def head_rows(b):
        return jnp.concatenate([b[c // 2] for c in range(2 * B_HEADS)], axis=0)

    def main():
        k = jnp.concatenate([p[0, 0, :, 0:B_QK_W] for p in pages], axis=0).astype(MXU_DTYPE)
        v = jnp.concatenate([p[0, 0, :, B_QK_W:] for p in pages], axis=0).astype(MXU_DTYPE)
        s = _dot_nt(qs, k) + head_rows(bias_ref[0])
        p, alpha, m, l = _online_update(s, m_ref[...], l_ref[...])
        m_ref[...] = m
        l_ref[...] = l
        acc_ref[...] = acc_ref[...] * alpha + _dot(p.astype(MXU_DTYPE), v)

    def final():
        k_new = rownew_ref[0, :, 0:B_QK_W]
        v_new = _pad_rows(rownew_ref[0, :, B_QK_W:], NEW_PAD).astype(MXU_DTYPE)
        s = _new_token_logits(qs, k_new, head_rows(biasnew_ref[...]), nq)
        p, alpha, _, l = _online_update(s, m_ref[...], l_ref[...])
        o = (acc_ref[...] * alpha + _dot(p.astype(MXU_DTYPE), v_new)) / l
        lane_v = _lane_iota(B_W)
        maps = [jnp.zeros((nq, B_W), F32), jnp.zeros((nq, B_W), F32)]
        for c in range(2 * B_HEADS):
            hm = _in_range(lane_v, (c // 2) * B_V_DIM, (c // 2 + 1) * B_V_DIM)
            maps[c % 2] = jnp.where(hm, o[c * nq:(c + 1) * nq], maps[c % 2])
        out = _diff_finish(maps[0], maps[1], dp_ref[:, 0:1], lane_v) * subw_ref[...] * dp_ref[:, 1:2]
        o_ref[0] = out.astype(o_ref.dtype)

    return init, main, final


def _diff_sample(cache, layer, page_table, dp, qb, bias_steps, row_new, bias_new, subw):
    bs, nq, _ = qb.shape
    steps = page_table.shape[1] // PG
    rows = 2 * B_HEADS * nq
    kvw = B_QK_W + B_W
    per_seq = lambda shape: pl.BlockSpec((1,) + shape, lambda b, s, pt: (b,) + (0,) * len(shape))
    full = lambda a: pl.BlockSpec(a.shape, lambda b, s, pt: (0,) * a.ndim)
    return dict(
        body=functools.partial(_diff_sample_kernel, nq=nq),
        in_specs=_page_specs(layer, kvw, KB_OFF // kvw, PG)
        + [full(dp), per_seq((nq, B_QK_W)), _step_bias_spec(bias_steps.shape, steps),
           pl.BlockSpec((1, nq, kvw), lambda b, s, pt: (b, 0, KB_OFF // kvw)), full(bias_new), full(subw)],
        args=[cache] * PG + [dp, qb, bias_steps, row_new, bias_new, subw],
        out_spec=per_seq((nq, B_W)),
        out_shape=jax.ShapeDtypeStruct((bs, nq, B_W), F32),
        scratch=[pltpu.VMEM((rows, 1), F32), pltpu.VMEM((rows, 1), F32), pltpu.VMEM((rows, B_W), F32)])


def _moba_sample_kernel(pt_ref, *refs, nq, n_past_blocks):
    del pt_ref
    pages = refs[:PG]
    q_ref, bias_ref, rownew_ref, biasnew_ref, o_ref, oblk_ref, gate_ref, mst_ref, lst_ref = refs[PG:]
    step = pl.program_id(1)
    rows = A_HEADS * nq
    bps = PG * PAGE_SIZE // MOBA_BLOCK
    nbl = gate_ref.shape[1]
    lane_b = _lane_iota(nbl)
    lane_q = _lane_iota(A_W)
    q = q_ref[0]
    qf = jnp.concatenate([jnp.where(_in_range(lane_q, h * HEAD_DIM, (h + 1) * HEAD_DIM), q, 0.0)
                          for h in range(A_HEADS)], axis=0)
    qs = qf.astype(MXU_DTYPE)

    def head_rows(b):
        return jnp.concatenate([b[h] for h in range(A_HEADS)], axis=0)

    def init():
        gate_ref[...] = jnp.zeros_like(gate_ref)
        mst_ref[...] = jnp.zeros_like(mst_ref)
        lst_ref[...] = jnp.zeros_like(lst_ref)

    def main():
        kf = jnp.concatenate([p[0, 0, :, 0:A_W] for p in pages], axis=0)
        k = kf.astype(MXU_DTYPE)
        v = jnp.concatenate([p[0, 0, :, A_W:] for p in pages], axis=0).astype(MXU_DTYPE)
        s = _dot_nt(qs, k) + head_rows(bias_ref[0])
        gate, mst, lst = gate_ref[...], mst_ref[...], lst_ref[...]
        for j in range(bps):
            cs = slice(j * MOBA_BLOCK, (j + 1) * MOBA_BLOCK)
            sj = s[:, cs]
            mj = jnp.max(sj, axis=-1, keepdims=True)
            pj = jnp.exp(sj - mj)
            n = step * bps + j
            oblk_ref[n] = _dot(pj.astype(MXU_DTYPE), v[cs, :])
            mean = jnp.mean(kf[cs, :], axis=0, keepdims=True)
            hit = lane_b == n
            gate = jnp.where(hit, jnp.sum(qf * mean, axis=-1, keepdims=True), gate)
            mst = jnp.where(hit, mj, mst)
            lst = jnp.where(hit, jnp.sum(pj, axis=-1, keepdims=True), lst)
        gate_ref[...] = gate
        mst_ref[...] = mst
        lst_ref[...] = lst

    def final():
        gate, mst, lst = gate_ref[...], mst_ref[...], lst_ref[...]
        big = jnp.int32(nbl)
        score = jnp.where(lane_b < n_past_blocks, gate, -jnp.inf)
        sel = jnp.zeros((rows, nbl), F32)
        for _ in range(MOBA_TOPK):
            vmax = jnp.max(score, axis=-1, keepdims=True)
            idx = jnp.min(jnp.where(score == vmax, lane_b, big), axis=-1, keepdims=True)
            hit = lane_b == idx
            sel = jnp.where(hit & (vmax > -jnp.inf), 1.0, sel)
            score = jnp.where(hit, -jnp.inf, score)
        k_new = rownew_ref[0, :, 0:A_W]
        v_new = _pad_rows(rownew_ref[0, :, A_W:], NEW_PAD).astype(MXU_DTYPE)
        s_new = _new_token_logits(qs, k_new, head_rows(biasnew_ref[...]), nq)
        m_new = jnp.max(s_new, axis=-1, keepdims=True)
        p_new = jnp.exp(s_new - m_new)
        chosen = sel > 0.5
        m_all = jnp.maximum(m_new, jnp.max(jnp.where(chosen, mst, NEG), axis=-1, keepdims=True))
        w = jnp.where(chosen, jnp.exp(mst - m_all), 0.0)
        w_new = jnp.exp(m_new - m_all)
        l_all = jnp.sum(w * lst, axis=-1, keepdims=True) + w_new * jnp.sum(p_new, axis=-1, keepdims=True)
        o = w_new * _dot(p_new.astype(MXU_DTYPE), v_new)
        for n in range(n_past_blocks):
            o = o + w[:, n:n + 1] * oblk_ref[n]
        o = o / l_all
        out = jnp.zeros((nq, A_W), F32)
        for h in range(A_HEADS):
            out = jnp.where(_in_range(lane_q, h * HEAD_DIM, (h + 1) * HEAD_DIM), o[h * nq:(h + 1) * nq], out)
        o_ref[0] = out.astype(o_ref.dtype)

    return init, main, final


def _moba_sample(cache, layer, page_table, qa, bias_steps, row_new, bias_new, past):
    bs, nq, _ = qa.shape
    steps = page_table.shape[1] // PG
    rows = A_HEADS * nq
    kvw = 2 * A_W
    n_past_blocks = past // MOBA_BLOCK
    nbl = -(-n_past_blocks // LANES) * LANES
    per_seq = lambda shape: pl.BlockSpec((1,) + shape, lambda b, s, pt: (b,) + (0,) * len(shape))
    full = lambda a: pl.BlockSpec(a.shape, lambda b, s, pt: (0,) * a.ndim)
    return dict(
        body=functools.partial(_moba_sample_kernel, nq=nq, n_past_blocks=n_past_blocks),
        in_specs=_page_specs(layer, kvw, KA_OFF // kvw, PG)
        + [per_seq((nq, A_W)), _step_bias_spec(bias_steps.shape, steps),
           pl.BlockSpec((1, nq, kvw), lambda b, s, pt: (b, 0, KA_OFF // kvw)), full(bias_new)],
        args=[cache] * PG + [qa, bias_steps, row_new, bias_new],
        out_spec=per_seq((nq, A_W)),
        out_shape=jax.ShapeDtypeStruct((bs, nq, A_W), F32),
        scratch=[pltpu.VMEM((n_past_blocks, rows, A_W), F32)] + [pltpu.VMEM((rows, nbl), F32)] * 3)


def _sample_stream_kernel(pt_ref, *refs, parts):
    n_in = [len(p["in_specs"]) for p in parts]
    n_sc = [len(p["scratch"]) for p in parts]
    outs = refs[sum(n_in):sum(n_in) + len(parts)]
    i0, s0 = 0, sum(n_in) + len(parts)
    phases = []
    for k, p in enumerate(parts):
        phases.append(p["body"](pt_ref, *refs[i0:i0 + n_in[k]], outs[k], *refs[s0:s0 + n_sc[k]]))
        i0 += n_in[k]
        s0 += n_sc[k]
    step = pl.program_id(1)

    @pl.when(step == 0)
    def _():
        for init, _, _ in phases:
            init()

    for _, main, _ in phases:
        main()

    @pl.when(step == pl.num_programs(1) - 1)
    def _():
        for _, _, final in phases:
            final()


def _sample_stream(page_table, parts):
    bs = page_table.shape[0]
    steps = page_table.shape[1] // PG
    grid_spec = pltpu.PrefetchScalarGridSpec(
        num_scalar_prefetch=1,
        grid=(bs, steps),
        in_specs=[s for p in parts for s in p["in_specs"]],
        out_specs=[p["out_spec"] for p in parts],
        scratch_shapes=[s for p in parts for s in p["scratch"]],
    )
    return pl.pallas_call(
        functools.partial(_sample_stream_kernel, parts=parts),
        grid_spec=grid_spec,
        out_shape=[p["out_shape"] for p in parts],
        compiler_params=_cparams("arbitrary", "arbitrary"),
        name="sample_stream",
    )(page_table, *[a for p in parts for a in p["args"]])


def _t5_bucket(rel):
    n = jnp.maximum(rel, 0)
    nf = jnp.maximum(n, 1).astype(F32)
    large = MAX_EXACT + (jnp.log(nf / MAX_EXACT) / math.log(T5_MAX_DIST / MAX_EXACT)
                         * (N_BUCKETS - MAX_EXACT)).astype(jnp.int32)
    return jnp.where(n < MAX_EXACT, n, jnp.minimum(large, N_BUCKETS - 1))


def _nsa_perm():
    return np.array([g * C_HPG * HEAD_DIM + hg * HEAD_DIM + d
                     for hg in range(C_HPG) for g in range(C_GROUPS) for d in range(HEAD_DIM)], np.int32)


def _sel_weights(nch, n_cmp, nsl):
    r = np.arange(nch)[:, None]
    s = (r - 1) * CMP_STRIDE
    b0 = np.arange(nsl)[None, :] * SEL_BLOCK
    ov = np.clip(np.minimum(s + CMP_LEN, b0 + SEL_BLOCK) - np.maximum(s, b0), 0, None)
    ov = np.where((r >= 1) & (r <= n_cmp), ov, 0)
    return jnp.asarray(ov.astype(np.float32) / np.float32(CMP_STRIDE), MXU_DTYPE)


def _block_expand(n_tiles, lanes, keys_per_tile):
    key = np.arange(keys_per_tile)[None, None, :] + np.arange(n_tiles)[:, None, None] * keys_per_tile
    blk = np.arange(lanes)[None, :, None]
    return jnp.asarray((key // SEL_BLOCK == blk).astype(np.float32), MXU_DTYPE)


def _compress_weights(pos_k, pos_v, k_w1, k_w2, v_w1, v_w2):
    half_rows = CMP_STRIDE * HEAD_DIM

    def first_layer(w1, half):
        return w1[half * half_rows:(half + 1) * half_rows].reshape(CMP_STRIDE // 4, 4 * HEAD_DIM, CMP_HIDDEN)

    def second_layer(w2):
        z = jnp.zeros_like(w2)
        return jnp.stack([jnp.concatenate([w2, z], axis=1), jnp.concatenate([z, w2], axis=1)])

    def pair_pos(pos):
        return pos.astype(F32).reshape(2, CMP_STRIDE // 2, 2 * HEAD_DIM)

    pos2 = jnp.stack([pair_pos(pos_k), pair_pos(pos_v)])
    wlo = jnp.stack([first_layer(k_w1, 0), first_layer(v_w1, 0)]).astype(MXU_DTYPE)
    whi = jnp.stack([first_layer(k_w1, 1), first_layer(v_w1, 1)]).astype(MXU_DTYPE)
    w2b = jnp.stack([second_layer(k_w2), second_layer(v_w2)]).astype(MXU_DTYPE)
    return pos2, wlo, whi, w2b


def kernel(x_prompt, x_sample, cache_kv, state_win, page_table, rel_bias_table, w_in, w_out, lam_q1, lam_k1, lam_q2, lam_k2, diff_subln_w, cmp_pos_k, cmp_pos_v, cmp_k_w1, cmp_k_w2, cmp_v_w1, cmp_v_w2, ln_mix_g, ln_mix_b, ln_ffn_g, ln_ffn_b, router_w1, router_b1, router_w2, router_b2, expert_w1, expert_w3, expert_w2):
    depth = w_in.shape[0]
    bp, seq, _ = x_prompt.shape
    bs, nq, _ = x_sample.shape
    n_pages = page_table.shape[1]
    past = n_pages * PAGE_SIZE
    wb = state_win.shape[2]
    alpha = (2.0 * depth) ** 0.25
    assert cache_kv.shape[2] == PAGE_SIZE and nq < CMP_STRIDE and nq % SUBLANES == 0
    assert past % MOBA_BLOCK == 0 and n_pages % PG == 0 and PG * PAGE_SIZE // SEL_BLOCK <= LANES
    assert wb == WINDOW and wb <= past

    perm = _nsa_perm()
    tab = rel_bias_table.astype(F32)

    def toeplitz(rel):
        bucket = _t5_bucket(jnp.asarray(rel.astype(np.int32))).reshape(1, -1)
        onehot = (bucket == jnp.arange(N_BUCKETS, dtype=jnp.int32)[:, None]).astype(F32)
        out = jnp.dot(tab, onehot, precision=lax.Precision.HIGHEST)
        return out.reshape((tab.shape[0],) + rel.shape)

    assert T + 1 >= T5_MAX_DIST and PG * PAGE_SIZE >= T5_MAX_DIST
    ti = np.arange(T)[None, :] - np.arange(T)[:, None]
    bias_tiles = toeplitz(np.stack([ti + d * T for d in range(3)]))
    ha, hb = A_HEADS, A_HEADS + B_HEADS
    bias_a_p, bias_b_p = bias_tiles[:ha], bias_tiles[ha:hb]
    bias_c_p = jnp.swapaxes(bias_tiles[hb:], 0, 1)
    qi = np.arange(nq)[:, None]
    step_keys = np.arange(PG * PAGE_SIZE)[None, :]
    bias_steps = jnp.swapaxes(toeplitz(np.stack([2 * PG * PAGE_SIZE + qi - step_keys,
                                                 PG * PAGE_SIZE + qi - step_keys])), 0, 1)
    bias_new = toeplitz(qi - np.arange(NEW_PAD)[None, :])
    wk = -(-(wb + nq) // LANES) * LANES
    bias_win = toeplitz(wb + qi - np.arange(wk)[None, :])[hb:]

    nch_p = seq // CMP_STRIDE
    nselp = -(-(seq // SEL_BLOCK) // SUBLANES) * SUBLANES
    wsel_p = _sel_weights(nch_p, (seq - CMP_LEN) // CMP_STRIDE + 1, nselp).T
    nch_s = past // CMP_STRIDE
    nsel_s = past // SEL_BLOCK + 1
    steps = n_pages // PG
    bps = PG * PAGE_SIZE // SEL_BLOCK
    nsl = -(-max(nsel_s, steps * bps) // LANES) * LANES
    wsel_s = _sel_weights(nch_s, (past + nq - CMP_LEN) // CMP_STRIDE + 1, nsl)
    expand_s = _block_expand(1, LANES, PG * PAGE_SIZE)[0]
    pt_prompt = jnp.arange(bp, dtype=jnp.int32)[:, None]
    pg_cmp = math.gcd(PG_CMP, n_pages)

    xp = x_prompt.reshape(bp * seq, D_MODEL)
    xs = x_sample.reshape(bs * nq, D_MODEL)
    kv_p, kv_s = None, None
    win_p, win_s = [], []
    for l in range(depth):
        w = w_in[l]
        w_packed = jnp.concatenate(
            [w[:, :A_W] * HEAD_DIM ** -0.5, w[:, A_W:A_W + B_QK_W] * B_QK_DIM ** -0.5,
             w[:, A_W + B_QK_W:A_W + B_QK_W + C_W][:, perm] * HEAD_DIM ** -0.5,
             jnp.pad(w[:, Q_W - GATE_W:Q_W], ((0, 0), (0, GT_PAD - GATE_W))), w[:, Q_W:]], axis=1).astype(MXU_DTYPE)
        wo = w_out[l]
        w_out_p = jnp.concatenate([wo[:A_W + B_W], wo[A_W + B_W:][perm]], axis=0).astype(MXU_DTYPE)
        lam_init = 0.8 - 0.6 * math.exp(-0.3 * l)
        lam = (jnp.exp(jnp.sum(lam_q1[l].astype(F32) * lam_k1[l].astype(F32)))
               - jnp.exp(jnp.sum(lam_q2[l].astype(F32) * lam_k2[l].astype(F32))) + lam_init)
        dp = jnp.zeros((1, LANES), F32).at[0, 0].set(lam).at[0, 1].set(1.0 - lam_init)
        subw = jnp.tile(diff_subln_w[l].astype(F32), B_HEADS)[None, :]
        cw = _compress_weights(cmp_pos_k[l], cmp_pos_v[l], cmp_k_w1[l], cmp_k_w2[l], cmp_v_w1[l], cmp_v_w2[l])
        wr = jnp.concatenate([router_w1[l], jnp.moveaxis(router_w2[l], 0, 1).reshape(D_MODEL, N_EXPERTS)], axis=1)
        wr = jnp.pad(wr.astype(F32), ((0, 0), (0, ROUTER_LANES - wr.shape[1])))
        wr_hi = wr.astype(MXU_DTYPE)
        wr_cat = jnp.concatenate([wr_hi, (wr - wr_hi.astype(F32)).astype(MXU_DTYPE)], axis=1)
        br = jnp.concatenate([router_b1[l], router_b2[l].reshape(-1)]).astype(F32)
        br = jnp.pad(br, (0, ROUTER_LANES - br.shape[0]))[None, :]
        ew1, ew3, ew2 = expert_w1[l].astype(MXU_DTYPE), expert_w3[l].astype(MXU_DTYPE), expert_w2[l].astype(MXU_DTYPE)
        g_mix, b_mix = ln_mix_g[l].astype(F32)[None, :], ln_mix_b[l].astype(F32)[None, :]
        g_ffn, b_ffn = ln_ffn_g[l].astype(F32)[None, :], ln_ffn_b[l].astype(F32)[None, :]

        qa, qb, qc, gt, kv_p, win = _inproj(xp, w_packed, l, depth, kv_p)
        r3 = lambda a: a.reshape(bp, seq, a.shape[-1])
        rows4, win3 = kv_p.reshape(depth, bp, seq, KV_ROW), r3(win)
        cmp_p = _compress(rows4, l, pt_prompt, KC_OFF // C_KV_W, cw, 1)
        oa = _moba_prompt(r3(qa), rows4, l, bias_a_p)
        ob = _diff_prompt(dp, r3(qb), rows4, l, bias_b_p, jnp.broadcast_to(subw.T, (B_W, T)))
        oc = _nsa_prompt(r3(qc), r3(gt), cmp_p, rows4, l, win3, bias_c_p, wsel_p)
        f2 = lambda a: a.reshape(bp * seq, a.shape[-1])
        xp = _outproj_ln(f2(oa), f2(ob), f2(oc), xp, w_out_p, g_mix, b_mix, alpha)
        xp = _moe_ln(xp, wr_cat, br, ew1, ew3, ew2, g_ffn, b_ffn, alpha)
        win_p.append(win3[:, seq - min(WINDOW, seq):])

        qa, qb, qc, gt, kv_s, win = _inproj(xs, w_packed, l, depth, kv_s)
        s3 = lambda a: a.reshape(bs, nq, a.shape[-1]).astype(F32)
        row3, win3 = s3(kv_s[l]), s3(win)
        cmp_s = _compress(cache_kv, l, page_table, KC_OFF // C_KV_W, cw, pg_cmp)
        sel, ocw = _nsa_sample_pre(s3(qc), cmp_s, state_win, l, win3, bias_win, wsel_s, past)
        selw = sel[:, :, :steps * bps].reshape(bs, C_HEADS * nq, steps, bps)
        selw = jnp.pad(jnp.moveaxis(selw, 2, 1), ((0, 0), (0, 0), (0, 0), (0, LANES - bps)))
        oc, ob, oa = _sample_stream(page_table, [
            _nsa_sample_slc(cache_kv, l, page_table, s3(qc), s3(gt), selw, expand_s, bias_steps[:, hb:], row3,
                            bias_new[hb:], ocw),
            _diff_sample(cache_kv, l, page_table, dp, s3(qb), bias_steps[:, ha:hb], row3, bias_new[ha:hb], subw),
            _moba_sample(cache_kv, l, page_table, s3(qa), bias_steps[:, :ha], row3, bias_new[:ha], past)])
        f2 = lambda a: a.reshape(bs * nq, a.shape[-1]).astype(MXU_DTYPE)
        xs = _outproj_ln(f2(oa), f2(ob), f2(oc), xs, w_out_p, g_mix, b_mix, alpha)
        xs = _moe_ln(xs, wr_cat, br, ew1, ew3, ew2, g_ffn, b_ffn, alpha)
        win_s.append(jnp.concatenate([state_win[l], win3.astype(state_win.dtype)], axis=1)[:, nq:])

    return (xp.reshape(bp, seq, D_MODEL), xs.reshape(bs, nq, D_MODEL),
            kv_p.reshape(depth, bp, seq, KV_ROW), jnp.stack(win_p), kv_s.reshape(depth, bs, nq, KV_ROW),
            jnp.stack(win_s))
```
